```python
import jax, jax.numpy as jnp
from jax import lax
import numpy as np

D_MODEL = 2048
BATCH = 8
SEQ = 8192
DEPTH = 4

HEAD_DIM = 128
D_MIX = D_MODEL
POOL_WIDTH = D_MIX // 4
N_POOL_GROUPS = 4
POOL_GROUP_DIM = POOL_WIDTH // N_POOL_GROUPS
POOL_WINDOWS = (2, 4, 8, 16)
ATTN_WIDTH = D_MIX // 2
N_ATTN_HEADS = ATTN_WIDTH // HEAD_DIM
LRU_WIDTH = D_MIX - POOL_WIDTH - ATTN_WIDTH
N_LRU_BLOCKS = 4
LRU_BLOCK_DIM = LRU_WIDTH // N_LRU_BLOCKS
LRU_CONV_WIDTH = 4
LRU_C = 8.0
D_FF = ((8 * D_MODEL // 3 + 255) // 256) * 256
FFN_CONV_WIDTH = 3
Q_BLOCK = 128
N_IN = POOL_WIDTH + 3 * ATTN_WIDTH + N_ATTN_HEADS + 2 * LRU_WIDTH
EPS = 1e-6

kernel_name = "hymba_style_pool_fox_rglru_convffn"


def rmsnorm(x, g):
    xf = x.astype(jnp.float32)
    y = xf * lax.rsqrt(jnp.mean(xf * xf, axis=-1, keepdims=True) + EPS)
    return (y * g.astype(jnp.float32)).astype(x.dtype)


def causal_dwconv(u, w, b):
    K = w.shape[0]
    S = u.shape[1]
    up = jnp.pad(u, ((0, 0), (K - 1, 0), (0, 0)))
    out = b + up[:, 0:S] * w[0]
    for k in range(1, K):
        out = out + up[:, k:k + S] * w[k]
    return out


def pool_mixer(u, w, scale):
    B, S, _ = u.shape
    uf = u.astype(jnp.float32).reshape(B, S, N_POOL_GROUPS, POOL_GROUP_DIM)
    cs = jnp.cumsum(uf, axis=1)
    pos = jnp.arange(1, S + 1, dtype=jnp.float32)
    outs = []
    for g, win in enumerate(POOL_WINDOWS):
        csg = cs[:, :, g]
        lag = jnp.pad(csg, ((0, 0), (win, 0), (0, 0)))[:, :S]
        mean = (csg - lag) / jnp.minimum(pos, float(win))[None, :, None]
        outs.append(mean - uf[:, :, g])
    d = jnp.stack(outs, axis=2).astype(u.dtype)
    y = jnp.einsum('bsgc,gcd->bsgd', d, w).reshape(B, S, POOL_WIDTH)
    return y * scale


def forgetting_attention(q, k, v, f_logit, b_f):
    B, S, _ = q.shape
    H = N_ATTN_HEADS
    q = q.reshape(B, S, H, HEAD_DIM).transpose(0, 2, 1, 3)
    k = k.reshape(B, S, H, HEAD_DIM).transpose(0, 2, 1, 3)
    v = v.reshape(B, S, H, HEAD_DIM).transpose(0, 2, 1, 3)
    log_f = jax.nn.log_sigmoid(f_logit.astype(jnp.float32) + b_f.astype(jnp.float32))
    F = jnp.cumsum(log_f, axis=1).transpose(0, 2, 1)
    nb = S // Q_BLOCK
    qb = q.reshape(B, H, nb, Q_BLOCK, HEAD_DIM).transpose(2, 0, 1, 3, 4)
    Fb = F.reshape(B, H, nb, Q_BLOCK).transpose(2, 0, 1, 3)
    kpos = jnp.arange(S)
    scale = HEAD_DIM ** -0.5

    def block(args):
        q_blk, F_blk, i = args
        qpos = i * Q_BLOCK + jnp.arange(Q_BLOCK)
        s = (jnp.einsum('bhqd,bhkd->bhqk', q_blk, k).astype(jnp.float32) * scale
             + F_blk[..., None] - F[:, :, None, :])
        s = jnp.where(kpos[None, :] <= qpos[:, None], s, -jnp.inf)
        p = jax.nn.softmax(s, axis=-1).astype(v.dtype)
        return jnp.einsum('bhqk,bhkd->bhqd', p, v)

    o = lax.map(block, (qb, Fb, jnp.arange(nb)))
    return o.transpose(1, 0, 3, 2, 4).reshape(B, S, H * HEAD_DIM)


def rg_lru_branch(xb, yb, conv_w, conv_b, wa, ba, wi, bi, lam):
    B, S, _ = xb.shape
    xc = causal_dwconv(xb, conv_w, conv_b)
    xh = xc.reshape(B, S, N_LRU_BLOCKS, LRU_BLOCK_DIM)
    gate_r = jax.nn.sigmoid((jnp.einsum('bshc,hcd->bshd', xh, wa).reshape(B, S, LRU_WIDTH) + ba).astype(jnp.float32))
    gate_i = jax.nn.sigmoid((jnp.einsum('bshc,hcd->bshd', xh, wi).reshape(B, S, LRU_WIDTH) + bi).astype(jnp.float32))
    log_a = -LRU_C * gate_r * jax.nn.softplus(-lam.astype(jnp.float32))
    a = jnp.exp(log_a)
    inp = jnp.sqrt(-jnp.expm1(2.0 * log_a)) * (gate_i * xc.astype(jnp.float32))

    def combine(left, right):
        a1, b1 = left
        a2, b2 = right
        return a1 * a2, a2 * b1 + b2

    _, h = lax.associative_scan(combine, (a, inp), axis=1)
    return (h * jax.nn.gelu(yb.astype(jnp.float32))).astype(xb.dtype)


def hybrid_mixer(h, w_in, b_f, pool_w, pool_scale, lru_conv_w, lru_conv_b,
                 lru_wa, lru_ba, lru_wi, lru_bi, lru_lambda, w_out):
    z = h @ w_in
    sizes = [POOL_WIDTH, ATTN_WIDTH, ATTN_WIDTH, ATTN_WIDTH, N_ATTN_HEADS, LRU_WIDTH]
    offsets = [int(o) for o in np.cumsum(sizes)]
    zp, zq, zk, zv, zf, zx, zy = jnp.split(z, offsets, axis=-1)
    y_pool = pool_mixer(zp, pool_w, pool_scale)
    y_attn = forgetting_attention(zq, zk, zv, zf, b_f)
    y_lru = rg_lru_branch(zx, zy, lru_conv_w, lru_conv_b, lru_wa, lru_ba, lru_wi, lru_bi, lru_lambda)
    y = jnp.concatenate([y_pool, y_attn.astype(h.dtype), y_lru], axis=-1)
    return y @ w_out


def conv_glu_ffn(h, w_gate, w_up, conv_w, conv_b, w_down):
    g = causal_dwconv(h @ w_gate, conv_w, conv_b)
    return (jax.nn.silu(g) * (h @ w_up)) @ w_down


def _fwd_setup_inputs(seed: int = 0) -> dict:
    key = jax.random.key(seed)
    ks = jax.random.split(key, 26)
    f32 = jnp.float32

    def nrm(k, shape, s):
        return jax.random.normal(k, shape, f32) * s

    u_lam = jax.random.uniform(ks[15], (DEPTH, LRU_WIDTH), f32, 0.9, 0.999)
    s_lam = u_lam ** (1.0 / LRU_C)
    lru_lambda = jnp.log(s_lam) - jnp.log1p(-s_lam)
    return {
        "x": nrm(ks[0], (BATCH, SEQ, D_MODEL), 1.0),
        "c": nrm(ks[1], (BATCH, D_MODEL), 1.0),
        "w_ada": nrm(ks[2], (DEPTH, D_MODEL, 6 * D_MODEL), 0.5 * D_MODEL ** -0.5),
        "b_ada": nrm(ks[3], (DEPTH, 6 * D_MODEL), 0.01),
        "g_mix": 1.0 + nrm(ks[4], (DEPTH, D_MODEL), 0.05),
        "w_in": nrm(ks[5], (DEPTH, D_MODEL, N_IN), D_MODEL ** -0.5),
        "b_f": jax.random.uniform(ks[6], (DEPTH, N_ATTN_HEADS), f32, 1.0, 5.0),
        "pool_w": nrm(ks[7], (DEPTH, N_POOL_GROUPS, POOL_GROUP_DIM, POOL_GROUP_DIM), POOL_GROUP_DIM ** -0.5),
        "pool_scale": 1.0 + nrm(ks[8], (DEPTH, POOL_WIDTH), 0.1),
        "lru_conv_w": nrm(ks[9], (DEPTH, LRU_CONV_WIDTH, LRU_WIDTH), LRU_CONV_WIDTH ** -0.5),
        "lru_conv_b": nrm(ks[10], (DEPTH, LRU_WIDTH), 0.01),
        "lru_wa": nrm(ks[11], (DEPTH, N_LRU_BLOCKS, LRU_BLOCK_DIM, LRU_BLOCK_DIM), LRU_BLOCK_DIM ** -0.5),
        "lru_ba": nrm(ks[12], (DEPTH, LRU_WIDTH), 0.01),
        "lru_wi": nrm(ks[13], (DEPTH, N_LRU_BLOCKS, LRU_BLOCK_DIM, LRU_BLOCK_DIM), LRU_BLOCK_DIM ** -0.5),
        "lru_bi": nrm(ks[14], (DEPTH, LRU_WIDTH), 0.01),
        "lru_lambda": lru_lambda,
        "w_out": nrm(ks[16], (DEPTH, D_MIX, D_MODEL), D_MIX ** -0.5),
        "g_ffn": 1.0 + nrm(ks[17], (DEPTH, D_MODEL), 0.05),
        "w_ffn_gate": nrm(ks[18], (DEPTH, D_MODEL, D_FF), D_MODEL ** -0.5),
        "w_ffn_up": nrm(ks[19], (DEPTH, D_MODEL, D_FF), D_MODEL ** -0.5),
        "ffn_conv_w": nrm(ks[20], (DEPTH, FFN_CONV_WIDTH, D_FF), FFN_CONV_WIDTH ** -0.5),
        "ffn_conv_b": nrm(ks[21], (DEPTH, D_FF), 0.01),
        "w_ffn_down": nrm(ks[22], (DEPTH, D_FF, D_MODEL), D_FF ** -0.5),
        "final_g": 1.0 + nrm(ks[23], (D_MODEL,), 0.05),
    }


def _fwd_reference(x, c, w_ada, b_ada, g_mix, w_in, b_f, pool_w, pool_scale, lru_conv_w, lru_conv_b,
              lru_wa, lru_ba, lru_wi, lru_bi, lru_lambda, w_out, g_ffn, w_ffn_gate, w_ffn_up,
              ffn_conv_w, ffn_conv_b, w_ffn_down, final_g):
    c_act = jax.nn.silu(c)
    for l in range(DEPTH):
        mod = c_act @ w_ada[l] + b_ada[l]
        sh1, sc1, gt1, sh2, sc2, gt2 = jnp.split(mod[:, None, :], 6, axis=-1)
        h = rmsnorm(x, g_mix[l]) * (1.0 + sc1) + sh1
        x = x + gt1 * hybrid_mixer(h, w_in[l], b_f[l], pool_w[l], pool_scale[l], lru_conv_w[l], lru_conv_b[l],
                                   lru_wa[l], lru_ba[l], lru_wi[l], lru_bi[l], lru_lambda[l], w_out[l])
        h = rmsnorm(x, g_ffn[l]) * (1.0 + sc2) + sh2
        x = x + gt2 * conv_glu_ffn(h, w_ffn_gate[l], w_ffn_up[l], ffn_conv_w[l], ffn_conv_b[l], w_ffn_down[l])
    return rmsnorm(x, final_g)


import jax as _jax
import jax.numpy as _jnp

TWIN_FORMAT = 'train_step'
FWD_PARAMS = ['x', 'c', 'w_ada', 'b_ada', 'g_mix', 'w_in', 'b_f', 'pool_w', 'pool_scale', 'lru_conv_w', 'lru_conv_b', 'lru_wa', 'lru_ba', 'lru_wi', 'lru_bi', 'lru_lambda', 'w_out', 'g_ffn', 'w_ffn_gate', 'w_ffn_up', 'ffn_conv_w', 'ffn_conv_b', 'w_ffn_down', 'final_g']
TWIN_WEIGHTS = ['w_ada', 'b_ada', 'g_mix', 'w_in', 'b_f', 'pool_w', 'pool_scale', 'lru_conv_w', 'lru_conv_b', 'lru_wa', 'lru_ba', 'lru_wi', 'lru_bi', 'lru_lambda', 'w_out', 'g_ffn', 'w_ffn_gate', 'w_ffn_up', 'ffn_conv_w', 'ffn_conv_b', 'w_ffn_down', 'final_g']
TWIN_DIFF_INPUT = 'x'
TWIN_INPUTS = ['x', 'c', 'w_ada', 'b_ada', 'g_mix', 'w_in', 'b_f', 'pool_w', 'pool_scale', 'lru_conv_w', 'lru_conv_b', 'lru_wa', 'lru_ba', 'lru_wi', 'lru_bi', 'lru_lambda', 'w_out', 'g_ffn', 'w_ffn_gate', 'w_ffn_up', 'ffn_conv_w', 'ffn_conv_b', 'w_ffn_down', 'final_g', 'loss_target', 'm_w_ada', 'm_b_ada', 'm_g_mix', 'm_w_in', 'm_b_f', 'm_pool_w', 'm_pool_scale', 'm_lru_conv_w', 'm_lru_conv_b', 'm_lru_wa', 'm_lru_ba', 'm_lru_wi', 'm_lru_bi', 'm_lru_lambda', 'm_w_out', 'm_g_ffn', 'm_w_ffn_gate', 'm_w_ffn_up', 'm_ffn_conv_w', 'm_ffn_conv_b', 'm_w_ffn_down', 'm_final_g', 'v_w_ada', 'v_b_ada', 'v_g_mix', 'v_w_in', 'v_b_f', 'v_pool_w', 'v_pool_scale', 'v_lru_conv_w', 'v_lru_conv_b', 'v_lru_wa', 'v_lru_ba', 'v_lru_wi', 'v_lru_bi', 'v_lru_lambda', 'v_w_out', 'v_g_ffn', 'v_w_ffn_gate', 'v_w_ffn_up', 'v_ffn_conv_w', 'v_ffn_conv_b', 'v_w_ffn_down', 'v_final_g']
TWIN_OUTPUTS = ['loss', 'grad_x', 'grad_w_ada', 'grad_b_ada', 'grad_g_mix', 'grad_w_in', 'grad_b_f', 'grad_pool_w', 'grad_pool_scale', 'grad_lru_conv_w', 'grad_lru_conv_b', 'grad_lru_wa', 'grad_lru_ba', 'grad_lru_wi', 'grad_lru_bi', 'grad_lru_lambda', 'grad_w_out', 'grad_g_ffn', 'grad_w_ffn_gate', 'grad_w_ffn_up', 'grad_ffn_conv_w', 'grad_ffn_conv_b', 'grad_w_ffn_down', 'grad_final_g', 'delta_w_ada', 'delta_b_ada', 'delta_g_mix', 'delta_w_in', 'delta_b_f', 'delta_pool_w', 'delta_pool_scale', 'delta_lru_conv_w', 'delta_lru_conv_b', 'delta_lru_wa', 'delta_lru_ba', 'delta_lru_wi', 'delta_lru_bi', 'delta_lru_lambda', 'delta_w_out', 'delta_g_ffn', 'delta_w_ffn_gate', 'delta_w_ffn_up', 'delta_ffn_conv_w', 'delta_ffn_conv_b', 'delta_w_ffn_down', 'delta_final_g', 'new_m_w_ada', 'new_m_b_ada', 'new_m_g_mix', 'new_m_w_in', 'new_m_b_f', 'new_m_pool_w', 'new_m_pool_scale', 'new_m_lru_conv_w', 'new_m_lru_conv_b', 'new_m_lru_wa', 'new_m_lru_ba', 'new_m_lru_wi', 'new_m_lru_bi', 'new_m_lru_lambda', 'new_m_w_out', 'new_m_g_ffn', 'new_m_w_ffn_gate', 'new_m_w_ffn_up', 'new_m_ffn_conv_w', 'new_m_ffn_conv_b', 'new_m_w_ffn_down', 'new_m_final_g', 'new_v_w_ada', 'new_v_b_ada', 'new_v_g_mix', 'new_v_w_in', 'new_v_b_f', 'new_v_pool_w', 'new_v_pool_scale', 'new_v_lru_conv_w', 'new_v_lru_conv_b', 'new_v_lru_wa', 'new_v_lru_ba', 'new_v_lru_wi', 'new_v_lru_bi', 'new_v_lru_lambda', 'new_v_w_out', 'new_v_g_ffn', 'new_v_w_ffn_gate', 'new_v_w_ffn_up', 'new_v_ffn_conv_w', 'new_v_ffn_conv_b', 'new_v_w_ffn_down', 'new_v_final_g']
TWIN_LEAF_KINDS = {'loss': 'loss', 'grad_x': 'grad_x', 'grad_w_ada': 'grad_w', 'grad_b_ada': 'grad_w', 'grad_g_mix': 'grad_w', 'grad_w_in': 'grad_w', 'grad_b_f': 'grad_w', 'grad_pool_w': 'grad_w', 'grad_pool_scale': 'grad_w', 'grad_lru_conv_w': 'grad_w', 'grad_lru_conv_b': 'grad_w', 'grad_lru_wa': 'grad_w', 'grad_lru_ba': 'grad_w', 'grad_lru_wi': 'grad_w', 'grad_lru_bi': 'grad_w', 'grad_lru_lambda': 'grad_w', 'grad_w_out': 'grad_w', 'grad_g_ffn': 'grad_w', 'grad_w_ffn_gate': 'grad_w', 'grad_w_ffn_up': 'grad_w', 'grad_ffn_conv_w': 'grad_w', 'grad_ffn_conv_b': 'grad_w', 'grad_w_ffn_down': 'grad_w', 'grad_final_g': 'grad_w', 'delta_w_ada': 'delta_w', 'delta_b_ada': 'delta_w', 'delta_g_mix': 'delta_w', 'delta_w_in': 'delta_w', 'delta_b_f': 'delta_w', 'delta_pool_w': 'delta_w', 'delta_pool_scale': 'delta_w', 'delta_lru_conv_w': 'delta_w', 'delta_lru_conv_b': 'delta_w', 'delta_lru_wa': 'delta_w', 'delta_lru_ba': 'delta_w', 'delta_lru_wi': 'delta_w', 'delta_lru_bi': 'delta_w', 'delta_lru_lambda': 'delta_w', 'delta_w_out': 'delta_w', 'delta_g_ffn': 'delta_w', 'delta_w_ffn_gate': 'delta_w', 'delta_w_ffn_up': 'delta_w', 'delta_ffn_conv_w': 'delta_w', 'delta_ffn_conv_b': 'delta_w', 'delta_w_ffn_down': 'delta_w', 'delta_final_g': 'delta_w', 'new_m_w_ada': 'new_m', 'new_m_b_ada': 'new_m', 'new_m_g_mix': 'new_m', 'new_m_w_in': 'new_m', 'new_m_b_f': 'new_m', 'new_m_pool_w': 'new_m', 'new_m_pool_scale': 'new_m', 'new_m_lru_conv_w': 'new_m', 'new_m_lru_conv_b': 'new_m', 'new_m_lru_wa': 'new_m', 'new_m_lru_ba': 'new_m', 'new_m_lru_wi': 'new_m', 'new_m_lru_bi': 'new_m', 'new_m_lru_lambda': 'new_m', 'new_m_w_out': 'new_m', 'new_m_g_ffn': 'new_m', 'new_m_w_ffn_gate': 'new_m', 'new_m_w_ffn_up': 'new_m', 'new_m_ffn_conv_w': 'new_m', 'new_m_ffn_conv_b': 'new_m', 'new_m_w_ffn_down': 'new_m', 'new_m_final_g': 'new_m', 'new_v_w_ada': 'new_v', 'new_v_b_ada': 'new_v', 'new_v_g_mix': 'new_v', 'new_v_w_in': 'new_v', 'new_v_b_f': 'new_v', 'new_v_pool_w': 'new_v', 'new_v_pool_scale': 'new_v', 'new_v_lru_conv_w': 'new_v', 'new_v_lru_conv_b': 'new_v', 'new_v_lru_wa': 'new_v', 'new_v_lru_ba': 'new_v', 'new_v_lru_wi': 'new_v', 'new_v_lru_bi': 'new_v', 'new_v_lru_lambda': 'new_v', 'new_v_w_out': 'new_v', 'new_v_g_ffn': 'new_v', 'new_v_w_ffn_gate': 'new_v', 'new_v_w_ffn_up': 'new_v', 'new_v_ffn_conv_w': 'new_v', 'new_v_ffn_conv_b': 'new_v', 'new_v_w_ffn_down': 'new_v', 'new_v_final_g': 'new_v'}


def _forward(args):
    return _fwd_reference(*[args[k] for k in FWD_PARAMS])


def _output_shape():
    def fwd():
        inp = _fwd_setup_inputs(0)
        return _fwd_reference(*[inp[k] for k in FWD_PARAMS])
    out = _jax.eval_shape(fwd)
    return out.shape, out.dtype

N_MICROBATCH = 1
ADAM_LR = 0.001
ADAM_B1 = 0.9
ADAM_B2 = 0.999
ADAM_EPS = 1e-08
ADAM_WD = 0.01
ADAM_STEP = 10
PER_EXAMPLE_BATCH_AXIS = {'x': 0, 'c': 0, 'loss_target': 0}
SHARED_INPUTS = []
_WEIGHT_DTYPES = {'w_ada': _jnp.float32, 'b_ada': _jnp.float32, 'g_mix': _jnp.float32, 'w_in': _jnp.float32, 'b_f': _jnp.float32, 'pool_w': _jnp.float32, 'pool_scale': _jnp.float32, 'lru_conv_w': _jnp.float32, 'lru_conv_b': _jnp.float32, 'lru_wa': _jnp.float32, 'lru_ba': _jnp.float32, 'lru_wi': _jnp.float32, 'lru_bi': _jnp.float32, 'lru_lambda': _jnp.float32, 'w_out': _jnp.float32, 'g_ffn': _jnp.float32, 'w_ffn_gate': _jnp.float32, 'w_ffn_up': _jnp.float32, 'ffn_conv_w': _jnp.float32, 'ffn_conv_b': _jnp.float32, 'w_ffn_down': _jnp.float32, 'final_g': _jnp.float32}
MOMENT_SCALE = {'w_ada': 6.571308e-02, 'b_ada': 1.229760e-01, 'g_mix': 4.279496e-02, 'w_in': 3.975015e-02, 'b_f': 8.128355e-02, 'pool_w': 3.299676e-02, 'pool_scale': 3.179107e-02, 'lru_conv_w': 8.806462e-02, 'lru_conv_b': 2.703656e-01, 'lru_wa': 8.865680e-03, 'lru_ba': 1.801344e-02, 'lru_wi': 1.894115e-02, 'lru_bi': 4.175222e-02, 'lru_lambda': 5.253327e-02, 'w_out': 4.524773e-02, 'g_ffn': 3.532367e-02, 'w_ffn_gate': 1.597282e-02, 'w_ffn_up': 1.553712e-02, 'ffn_conv_w': 1.592362e-02, 'ffn_conv_b': 1.451150e-02, 'w_ffn_down': 2.575419e-02, 'final_g': 3.205762e+01}


def _to_microbatches(a, axis):
    t = _jnp.moveaxis(a, axis, 0)
    t = t.reshape((N_MICROBATCH, t.shape[0] // N_MICROBATCH) + t.shape[1:])
    return _jnp.moveaxis(t, 1, axis + 1)


def setup_inputs(seed: int = 0) -> dict:
    inp = _fwd_setup_inputs(seed)
    key = _jax.random.fold_in(_jax.random.key(seed), 7919)
    shape, _ = _output_shape()
    out = dict(inp)
    out["loss_target"] = _jax.random.normal(_jax.random.fold_in(key, 0), shape, _jnp.float32)
    for i, name in enumerate(TWIN_WEIGHTS):
        w = inp[name].astype(_jnp.float32)
        if MOMENT_SCALE is None:
            s = _jnp.sqrt(_jnp.mean(_jnp.square(w)) + 1e-30)
        else:
            s = MOMENT_SCALE[name]
        km, kv = _jax.random.split(_jax.random.fold_in(key, i + 1))
        out[name] = w
        out["m_" + name] = s * _jax.random.normal(km, w.shape, _jnp.float32)
        out["v_" + name] = (s * s) * _jax.random.uniform(kv, w.shape, _jnp.float32, 0.5, 1.5)
    if N_MICROBATCH > 1:
        for name, axis in PER_EXAMPLE_BATCH_AXIS.items():
            out[name] = _to_microbatches(out[name], axis)
    return {'x': out['x'], 'c': out['c'], 'w_ada': out['w_ada'], 'b_ada': out['b_ada'], 'g_mix': out['g_mix'], 'w_in': out['w_in'], 'b_f': out['b_f'], 'pool_w': out['pool_w'], 'pool_scale': out['pool_scale'], 'lru_conv_w': out['lru_conv_w'], 'lru_conv_b': out['lru_conv_b'], 'lru_wa': out['lru_wa'], 'lru_ba': out['lru_ba'], 'lru_wi': out['lru_wi'], 'lru_bi': out['lru_bi'], 'lru_lambda': out['lru_lambda'], 'w_out': out['w_out'], 'g_ffn': out['g_ffn'], 'w_ffn_gate': out['w_ffn_gate'], 'w_ffn_up': out['w_ffn_up'], 'ffn_conv_w': out['ffn_conv_w'], 'ffn_conv_b': out['ffn_conv_b'], 'w_ffn_down': out['w_ffn_down'], 'final_g': out['final_g'], 'loss_target': out['loss_target'], 'm_w_ada': out['m_w_ada'], 'm_b_ada': out['m_b_ada'], 'm_g_mix': out['m_g_mix'], 'm_w_in': out['m_w_in'], 'm_b_f': out['m_b_f'], 'm_pool_w': out['m_pool_w'], 'm_pool_scale': out['m_pool_scale'], 'm_lru_conv_w': out['m_lru_conv_w'], 'm_lru_conv_b': out['m_lru_conv_b'], 'm_lru_wa': out['m_lru_wa'], 'm_lru_ba': out['m_lru_ba'], 'm_lru_wi': out['m_lru_wi'], 'm_lru_bi': out['m_lru_bi'], 'm_lru_lambda': out['m_lru_lambda'], 'm_w_out': out['m_w_out'], 'm_g_ffn': out['m_g_ffn'], 'm_w_ffn_gate': out['m_w_ffn_gate'], 'm_w_ffn_up': out['m_w_ffn_up'], 'm_ffn_conv_w': out['m_ffn_conv_w'], 'm_ffn_conv_b': out['m_ffn_conv_b'], 'm_w_ffn_down': out['m_w_ffn_down'], 'm_final_g': out['m_final_g'], 'v_w_ada': out['v_w_ada'], 'v_b_ada': out['v_b_ada'], 'v_g_mix': out['v_g_mix'], 'v_w_in': out['v_w_in'], 'v_b_f': out['v_b_f'], 'v_pool_w': out['v_pool_w'], 'v_pool_scale': out['v_pool_scale'], 'v_lru_conv_w': out['v_lru_conv_w'], 'v_lru_conv_b': out['v_lru_conv_b'], 'v_lru_wa': out['v_lru_wa'], 'v_lru_ba': out['v_lru_ba'], 'v_lru_wi': out['v_lru_wi'], 'v_lru_bi': out['v_lru_bi'], 'v_lru_lambda': out['v_lru_lambda'], 'v_w_out': out['v_w_out'], 'v_g_ffn': out['v_g_ffn'], 'v_w_ffn_gate': out['v_w_ffn_gate'], 'v_w_ffn_up': out['v_w_ffn_up'], 'v_ffn_conv_w': out['v_ffn_conv_w'], 'v_ffn_conv_b': out['v_ffn_conv_b'], 'v_w_ffn_down': out['v_w_ffn_down'], 'v_final_g': out['v_final_g']}


def _loss(weights, diff, rest, loss_target):
    with _jax.named_scope("forward"):
        args = {**rest, TWIN_DIFF_INPUT: diff, **{k: w.astype(_WEIGHT_DTYPES[k]) for k, w in weights.items()}}
        y = _forward(args)
    with _jax.named_scope("loss_head"):
        err = _jnp.square(y.astype(_jnp.float32) - loss_target)
        return 0.5 * _jnp.sum(_jnp.mean(err, axis=-1)) if err.ndim else 0.5 * err


def _adamw(w, g, m, v):
    m = ADAM_B1 * m + (1.0 - ADAM_B1) * g
    v = ADAM_B2 * v + (1.0 - ADAM_B2) * _jnp.square(g)
    m_hat = m / (1.0 - ADAM_B1 ** ADAM_STEP)
    v_hat = v / (1.0 - ADAM_B2 ** ADAM_STEP)
    delta = -ADAM_LR * (m_hat / (_jnp.sqrt(v_hat) + ADAM_EPS) + ADAM_WD * w)
    return delta, m, v


def reference(x, c, w_ada, b_ada, g_mix, w_in, b_f, pool_w, pool_scale, lru_conv_w, lru_conv_b, lru_wa, lru_ba, lru_wi, lru_bi, lru_lambda, w_out, g_ffn, w_ffn_gate, w_ffn_up, ffn_conv_w, ffn_conv_b, w_ffn_down, final_g, loss_target, m_w_ada, m_b_ada, m_g_mix, m_w_in, m_b_f, m_pool_w, m_pool_scale, m_lru_conv_w, m_lru_conv_b, m_lru_wa, m_lru_ba, m_lru_wi, m_lru_bi, m_lru_lambda, m_w_out, m_g_ffn, m_w_ffn_gate, m_w_ffn_up, m_ffn_conv_w, m_ffn_conv_b, m_w_ffn_down, m_final_g, v_w_ada, v_b_ada, v_g_mix, v_w_in, v_b_f, v_pool_w, v_pool_scale, v_lru_conv_w, v_lru_conv_b, v_lru_wa, v_lru_ba, v_lru_wi, v_lru_bi, v_lru_lambda, v_w_out, v_g_ffn, v_w_ffn_gate, v_w_ffn_up, v_ffn_conv_w, v_ffn_conv_b, v_w_ffn_down, v_final_g):
    given = dict(x=x, c=c, w_ada=w_ada, b_ada=b_ada, g_mix=g_mix, w_in=w_in, b_f=b_f, pool_w=pool_w, pool_scale=pool_scale, lru_conv_w=lru_conv_w, lru_conv_b=lru_conv_b, lru_wa=lru_wa, lru_ba=lru_ba, lru_wi=lru_wi, lru_bi=lru_bi, lru_lambda=lru_lambda, w_out=w_out, g_ffn=g_ffn, w_ffn_gate=w_ffn_gate, w_ffn_up=w_ffn_up, ffn_conv_w=ffn_conv_w, ffn_conv_b=ffn_conv_b, w_ffn_down=w_ffn_down, final_g=final_g, loss_target=loss_target, m_w_ada=m_w_ada, m_b_ada=m_b_ada, m_g_mix=m_g_mix, m_w_in=m_w_in, m_b_f=m_b_f, m_pool_w=m_pool_w, m_pool_scale=m_pool_scale, m_lru_conv_w=m_lru_conv_w, m_lru_conv_b=m_lru_conv_b, m_lru_wa=m_lru_wa, m_lru_ba=m_lru_ba, m_lru_wi=m_lru_wi, m_lru_bi=m_lru_bi, m_lru_lambda=m_lru_lambda, m_w_out=m_w_out, m_g_ffn=m_g_ffn, m_w_ffn_gate=m_w_ffn_gate, m_w_ffn_up=m_w_ffn_up, m_ffn_conv_w=m_ffn_conv_w, m_ffn_conv_b=m_ffn_conv_b, m_w_ffn_down=m_w_ffn_down, m_final_g=m_final_g, v_w_ada=v_w_ada, v_b_ada=v_b_ada, v_g_mix=v_g_mix, v_w_in=v_w_in, v_b_f=v_b_f, v_pool_w=v_pool_w, v_pool_scale=v_pool_scale, v_lru_conv_w=v_lru_conv_w, v_lru_conv_b=v_lru_conv_b, v_lru_wa=v_lru_wa, v_lru_ba=v_lru_ba, v_lru_wi=v_lru_wi, v_lru_bi=v_lru_bi, v_lru_lambda=v_lru_lambda, v_w_out=v_w_out, v_g_ffn=v_g_ffn, v_w_ffn_gate=v_w_ffn_gate, v_w_ffn_up=v_w_ffn_up, v_ffn_conv_w=v_ffn_conv_w, v_ffn_conv_b=v_ffn_conv_b, v_w_ffn_down=v_w_ffn_down, v_final_g=v_final_g)
    weights = {n: given[n] for n in TWIN_WEIGHTS}
    shared = {n: given[n] for n in SHARED_INPUTS}
    per_example = {n: given[n] for n in ['x', 'c']}
    grad_fn = _jax.value_and_grad(_loss, argnums=(0, 1))

    def one_microbatch(ex, loss_target):
        ex = dict(ex)
        diff = ex.pop(TWIN_DIFF_INPUT)
        return grad_fn(weights, diff, {**shared, **ex}, loss_target)

    if N_MICROBATCH == 1:
        loss, (grad_w, grad_x) = one_microbatch(per_example, given["loss_target"])
    else:
        def body(carry, xs):
            loss_sum, grad_sum = carry
            l_k, (gw_k, gx_k) = one_microbatch(xs[0], xs[1])
            with _jax.named_scope("update"):
                return (loss_sum + l_k, _jax.tree.map(_jnp.add, grad_sum, gw_k)), gx_k

        init = (_jnp.zeros((), _jnp.float32), _jax.tree.map(_jnp.zeros_like, weights))
        (loss, grad_w), grad_x = _jax.lax.scan(body, init, (per_example, given["loss_target"]))
    with _jax.named_scope("update"):
        delta_w, new_m, new_v = {}, {}, {}
        for n in TWIN_WEIGHTS:
            delta_w[n], new_m[n], new_v[n] = _adamw(weights[n], grad_w[n], given["m_" + n], given["v_" + n])
    return (loss, grad_x, *[grad_w[n] for n in TWIN_WEIGHTS], *[delta_w[n] for n in TWIN_WEIGHTS],
            *[new_m[n] for n in TWIN_WEIGHTS], *[new_v[n] for n in TWIN_WEIGHTS])
```

```python
import functools
import math

import jax
import jax.numpy as jnp
import numpy as np
from jax import lax
from jax.experimental import pallas as pl
from jax.experimental.pallas import tpu as pltpu

F32 = jnp.float32
BF16 = jnp.bfloat16
MESH = pl.DeviceIdType.MESH

EPS = 1e-6
HEAD_DIM = 128
POOL_WIDTH = 512
POOL_WINDOWS = (2, 4, 8, 16)
ATTN_WIDTH = 1024
N_HEADS = 8
LRU_WIDTH = 512
LRU_C = 8.0
N_IN = 4616
ZP, ZQ, ZK, ZV, ZX, ZY, ZF, ZW = 0, 512, 1536, 2560, 3584, 4096, 4608, 5120
FFN_CHUNK = 512
N_CHIPS = 4
N_DEV = 8

ADAM_LR, ADAM_B1, ADAM_B2, ADAM_EPS, ADAM_WD, ADAM_STEP = 0.001, 0.9, 0.999, 1e-08, 0.01, 10

TILES = dict(mm_m=512, mm_n=1024, mm_k=2048, row=512, attn=512, lru=256, cum=512, ew=256)
VMEM_LIMIT = 48 * 2**20


def _params(sem):
    return pltpu.CompilerParams(dimension_semantics=sem, vmem_limit_bytes=VMEM_LIMIT)


def _div_tile(n, pref, align):
    if n <= pref:
        return n
    t = (pref // align) * align
    while t >= align:
        if n % t == 0:
            return t
        t -= align
    raise ValueError(f"no tile for {n}")


def _sigmoid(x):
    return 1.0 / (1.0 + jnp.exp(-x))


def _gelu_parts(x):
    k = math.sqrt(2.0 / math.pi)
    u = k * (x + 0.044715 * x * x * x)
    t = jnp.tanh(u)
    gel = 0.5 * x * (1.0 + t)
    dgel = 0.5 * (1.0 + t) + 0.5 * x * (1.0 - t * t) * k * (1.0 + 3 * 0.044715 * x * x)
    return gel, dgel


def _neg_expm1(y):
    series = -y * (1.0 + y * (0.5 + y * (1.0 / 6 + y * (1.0 / 24 + y * (1.0 / 120)))))
    return jnp.where(y > -0.1, series, 1.0 - jnp.exp(y))


def _dot(a, b):
    return jnp.dot(a.astype(BF16), b.astype(BF16), preferred_element_type=F32)


def _dot_nt(a, b):
    return lax.dot_general(a.astype(BF16), b.astype(BF16), (((1,), (1,)), ((), ())), preferred_element_type=F32)


def _dot3(tri, v):
    hi = v.astype(BF16)
    r1 = v - hi.astype(F32)
    mid = r1.astype(BF16)
    lo = (r1 - mid.astype(F32)).astype(BF16)
    t = tri.astype(BF16)
    return (jnp.dot(t, hi, preferred_element_type=F32) + jnp.dot(t, mid, preferred_element_type=F32)
            + jnp.dot(t, lo, preferred_element_type=F32))


def _colsum(v):
    return jnp.sum(v, axis=0, keepdims=True)


def _rows(n, cols=128):
    return lax.broadcasted_iota(jnp.int32, (n, cols), 0)


def _matmul(a, b, *, nt=False, out_dtype=F32, res=None, gate=None, name="matmul"):
    M, K = a.shape
    N = b.shape[0] if nt else b.shape[1]
    tm = _div_tile(M, TILES["mm_m"], 8)
    tn = _div_tile(N, TILES["mm_n"], 128)
    tk = _div_tile(K, TILES["mm_k"], 128)
    nk = K // tk
    epi = res is not None

    def body(*refs):
        if epi:
            a_ref, b_ref, res_ref, gate_ref, o_ref, x_ref = refs[:6]
        else:
            a_ref, b_ref, o_ref = refs[:3]
        part = _dot_nt(a_ref[...], b_ref[...]) if nt else _dot(a_ref[...], b_ref[...])

        def finish(acc):
            o_ref[...] = acc.astype(o_ref.dtype)
            if epi:
                x_ref[...] = res_ref[...] + gate_ref[...] * acc

        if nk == 1:
            finish(part)
        else:
            acc_ref = refs[-1]
            k = pl.program_id(2)

            @pl.when(k == 0)
            def _():
                acc_ref[...] = part

            @pl.when(k > 0)
            def _():
                acc_ref[...] += part

            @pl.when(k == nk - 1)
            def _():
                finish(acc_ref[...])

    a_spec = pl.BlockSpec((tm, tk), lambda i, j, k: (i, k))
    b_spec = pl.BlockSpec((tn, tk), lambda i, j, k: (j, k)) if nt else pl.BlockSpec((tk, tn), lambda i, j, k: (k, j))
    o_spec = pl.BlockSpec((tm, tn), lambda i, j, k: (i, j))
    in_specs, args = [a_spec, b_spec], [a, b]
    out_specs, out_shape = o_spec, jax.ShapeDtypeStruct((M, N), out_dtype)
    if epi:
        in_specs += [o_spec, pl.BlockSpec((1, tn), lambda i, j, k: (0, j))]
        args += [res, gate]
        out_specs = (o_spec, o_spec)
        out_shape = (out_shape, jax.ShapeDtypeStruct((M, N), F32))
    return pl.pallas_call(
        body, name=name, grid=(M // tm, N // tn, nk), in_specs=in_specs, out_specs=out_specs, out_shape=out_shape,
        scratch_shapes=[pltpu.VMEM((tm, tn), F32)] if nk > 1 else [],
        compiler_params=_params(("parallel", "parallel", "arbitrary")),
    )(*args)


def _norm_mod(x, g, sc, sh):
    S, D = x.shape
    tr = _div_tile(S, TILES["row"], 8)

    def body(x_ref, g_ref, sc_ref, sh_ref, h_ref):
        xf = x_ref[...]
        r = lax.rsqrt(jnp.mean(xf * xf, axis=-1, keepdims=True) + EPS)
        h_ref[...] = (((xf * r) * g_ref[...]) * (1.0 + sc_ref[...]) + sh_ref[...]).astype(h_ref.dtype)

    row = pl.BlockSpec((tr, D), lambda i: (i, 0))
    vec = pl.BlockSpec((1, D), lambda i: (0, 0))
    return pl.pallas_call(
        body, name="norm_mod", grid=(S // tr,), in_specs=[row, vec, vec, vec], out_specs=row,
        out_shape=jax.ShapeDtypeStruct((S, D), BF16), compiler_params=_params(("parallel",)),
    )(x, g, sc, sh)


def _norm_mod_bwd(x, dh, dres, g, sc):
    S, D = x.shape
    tr = _div_tile(S, TILES["ew"], 8)

    def body(x_ref, dh_ref, dres_ref, g_ref, sc_ref, dx_ref, dg_ref, dsc_ref, dsh_ref):
        xf, dh_ = x_ref[...], dh_ref[...]
        r = lax.rsqrt(jnp.mean(xf * xf, axis=-1, keepdims=True) + EPS)
        xhat = xf * r
        dxhat = dh_ * (g_ref[...] * (1.0 + sc_ref[...]))
        dx_ref[...] = dres_ref[...] + r * (dxhat - xhat * jnp.mean(dxhat * xhat, axis=-1, keepdims=True))
        t = _colsum(dh_ * xhat)

        @pl.when(pl.program_id(0) == 0)
        def _():
            dg_ref[...] = jnp.zeros_like(dg_ref)
            dsc_ref[...] = jnp.zeros_like(dsc_ref)
            dsh_ref[...] = jnp.zeros_like(dsh_ref)

        dg_ref[...] += t * (1.0 + sc_ref[...])
        dsc_ref[...] += t * g_ref[...]
        dsh_ref[...] += _colsum(dh_)

    row = pl.BlockSpec((tr, D), lambda i: (i, 0))
    vec = pl.BlockSpec((1, D), lambda i: (0, 0))
    vshape = jax.ShapeDtypeStruct((1, D), F32)
    return pl.pallas_call(
        body, name="norm_mod_bwd", grid=(S // tr,), in_specs=[row, row, row, vec, vec], out_specs=(row, vec, vec, vec),
        out_shape=(jax.ShapeDtypeStruct((S, D), F32), vshape, vshape, vshape), compiler_params=_params(("arbitrary",)),
    )(x, dh, dres, g, sc)


def _gate_bwd(dx, m, gt):
    S, D = dx.shape
    tr = _div_tile(S, TILES["row"], 8)

    def body(dx_ref, m_ref, gt_ref, dm_ref, dgt_ref):
        d = dx_ref[...]
        dm_ref[...] = (d * gt_ref[...]).astype(dm_ref.dtype)

        @pl.when(pl.program_id(0) == 0)
        def _():
            dgt_ref[...] = jnp.zeros_like(dgt_ref)

        dgt_ref[...] += _colsum(d * m_ref[...])

    row = pl.BlockSpec((tr, D), lambda i: (i, 0))
    vec = pl.BlockSpec((1, D), lambda i: (0, 0))
    return pl.pallas_call(
        body, name="gate_bwd", grid=(S // tr,), in_specs=[row, row, vec], out_specs=(row, vec),
        out_shape=(jax.ShapeDtypeStruct((S, D), BF16), jax.ShapeDtypeStruct((1, D), F32)),
        compiler_params=_params(("arbitrary",)),
    )(dx, m, gt)


def _loss_head(x, target, g):
    S, D = x.shape
    tr = _div_tile(S, TILES["ew"], 8)

    def body(x_ref, t_ref, g_ref, dx_ref, dg_ref, loss_ref):
        xf = x_ref[...]
        r = lax.rsqrt(jnp.mean(xf * xf, axis=-1, keepdims=True) + EPS)
        xhat = xf * r
        err = xhat * g_ref[...] - t_ref[...]
        dy = err * (1.0 / D)
        dxhat = dy * g_ref[...]
        dx_ref[...] = r * (dxhat - xhat * jnp.mean(dxhat * xhat, axis=-1, keepdims=True))

        @pl.when(pl.program_id(0) == 0)
        def _():
            dg_ref[...] = jnp.zeros_like(dg_ref)
            loss_ref[...] = jnp.zeros_like(loss_ref)

        dg_ref[...] += _colsum(dy * xhat)
        loss_ref[...] += 0.5 * jnp.sum(jnp.mean(err * err, axis=-1, keepdims=True))

    row = pl.BlockSpec((tr, D), lambda i: (i, 0))
    vec = pl.BlockSpec((1, D), lambda i: (0, 0))
    one = pl.BlockSpec((1, 128), lambda i: (0, 0))
    return pl.pallas_call(
        body, name="loss_head", grid=(S // tr,), in_specs=[row, row, vec], out_specs=(row, vec, one),
        out_shape=(jax.ShapeDtypeStruct((S, D), F32), jax.ShapeDtypeStruct((1, D), F32),
                   jax.ShapeDtypeStruct((1, 128), F32)),
        compiler_params=_params(("arbitrary",)),
    )(x, target, g)


POOL_HALO = 16


def _pool_delta(ext, u, first_pos, tr):
    pos = (first_pos + _rows(tr) + 1).astype(F32)
    outs = []
    for gi, win in enumerate(POOL_WINDOWS):
        s = ext[:, gi * 128:(gi + 1) * 128]
        d = 1
        while d < win:
            s = s + pltpu.roll(s, d, 0)
            d *= 2
        outs.append(s[POOL_HALO:] / jnp.minimum(pos, float(win)) - u[:, gi * 128:(gi + 1) * 128])
    return outs


def _pool_fwd(z, w, scale):
    S = z.shape[0]
    tr = _div_tile(S, TILES["row"], POOL_HALO)
    hb = tr // POOL_HALO

    def body(z_ref, halo_ref, w_ref, sc_ref, y_ref):
        i = pl.program_id(0)
        u = z_ref[...]
        halo = jnp.where(i > 0, halo_ref[...], 0.0)
        ds_ = _pool_delta(jnp.concatenate([halo, u], axis=0), u, i * tr, tr)
        for gi in range(4):
            y_ref[:, gi * 128:(gi + 1) * 128] = _dot(ds_[gi], w_ref[gi]) * sc_ref[:, gi * 128:(gi + 1) * 128]

    return pl.pallas_call(
        body, name="pool_fwd", grid=(S // tr,),
        in_specs=[pl.BlockSpec((tr, POOL_WIDTH), lambda i: (i, 0)),
                  pl.BlockSpec((POOL_HALO, POOL_WIDTH), lambda i: (jnp.maximum(i * hb - 1, 0), 0)),
                  pl.BlockSpec((4, 128, 128), lambda i: (0, 0, 0)), pl.BlockSpec((1, POOL_WIDTH), lambda i: (0, 0))],
        out_specs=pl.BlockSpec((tr, POOL_WIDTH), lambda i: (i, 0)),
        out_shape=jax.ShapeDtypeStruct((S, POOL_WIDTH), F32), compiler_params=_params(("parallel",)),
    )(z, z, w, scale)


def _pool_bwd(z, dy, w, scale):
    S = z.shape[0]
    tr = _div_tile(S, TILES["row"], POOL_HALO)
    hb = tr // POOL_HALO
    nt = S // tr

    def body(z_ref, halo_ref, dy_ref, dyn_ref, w_ref, sc_ref, dz_ref, dw_ref, dsc_ref):
        i = pl.program_id(0)
        u = z_ref[...]
        halo = jnp.where(i > 0, halo_ref[...], 0.0)
        ds_ = _pool_delta(jnp.concatenate([halo, u], axis=0), u, i * tr, tr)
        dy_ext = jnp.concatenate([dy_ref[...], jnp.where(i < nt - 1, dyn_ref[...], 0.0)], axis=0)
        pos = (i * tr + _rows(tr + POOL_HALO) + 1).astype(F32)

        @pl.when(i == 0)
        def _():
            dw_ref[...] = jnp.zeros_like(dw_ref)
            dsc_ref[...] = jnp.zeros_like(dsc_ref)

        for gi, win in enumerate(POOL_WINDOWS):
            cols = slice(gi * 128, (gi + 1) * 128)
            dyg = dy_ext[:, cols]
            dys = dyg * sc_ref[:, cols]
            dsc_ref[:, cols] += _colsum(dyg[:tr] * _dot(ds_[gi], w_ref[gi]))
            dw_ref[gi] += _dot(ds_[gi].T, dys[:tr])
            dd = _dot_nt(dys, w_ref[gi])
            e = dd / jnp.minimum(pos, float(win))
            d = 1
            while d < win:
                e = e + pltpu.roll(e, tr + POOL_HALO - d, 0)
                d *= 2
            dz_ref[:, cols] = (e[:tr] - dd[:tr]).astype(dz_ref.dtype)

    return pl.pallas_call(
        body, name="pool_bwd", grid=(nt,),
        in_specs=[pl.BlockSpec((tr, POOL_WIDTH), lambda i: (i, 0)),
                  pl.BlockSpec((POOL_HALO, POOL_WIDTH), lambda i: (jnp.maximum(i * hb - 1, 0), 0)),
                  pl.BlockSpec((tr, POOL_WIDTH), lambda i: (i, 0)),
                  pl.BlockSpec((POOL_HALO, POOL_WIDTH), lambda i: (jnp.minimum((i + 1) * hb, nt * hb - 1), 0)),
                  pl.BlockSpec((4, 128, 128), lambda i: (0, 0, 0)), pl.BlockSpec((1, POOL_WIDTH), lambda i: (0, 0))],
        out_specs=(pl.BlockSpec((tr, POOL_WIDTH), lambda i: (i, 0)), pl.BlockSpec((4, 128, 128), lambda i: (0, 0, 0)),
                   pl.BlockSpec((1, POOL_WIDTH), lambda i: (0, 0))),
        out_shape=(jax.ShapeDtypeStruct((S, POOL_WIDTH), BF16), jax.ShapeDtypeStruct((4, 128, 128), F32),
                   jax.ShapeDtypeStruct((1, POOL_WIDTH), F32)),
        compiler_params=_params(("arbitrary",)),
    )(z, z, dy, dy, w, scale)


def _log_sigmoid(x):
    return jnp.minimum(x, 0.0) - jnp.log(1.0 + jnp.exp(-jnp.abs(x)))


def _forget_cumsum(z, b_f):
    S = z.shape[0]
    tr = _div_tile(S, TILES["cum"], 8)
    zf_block = ZF // 128

    def body(z_ref, b_ref, f_ref, carry):
        @pl.when(pl.program_id(0) == 0)
        def _():
            carry[...] = jnp.zeros_like(carry)

        lf = _log_sigmoid(z_ref[...] + b_ref[...])
        tri = lax.broadcasted_iota(jnp.int32, (tr, tr), 1) <= lax.broadcasted_iota(jnp.int32, (tr, tr), 0)
        f_ref[...] = _dot3(tri, lf) + carry[...]
        carry[...] += _colsum(lf)

    return pl.pallas_call(
        body, name="forget_cumsum", grid=(S // tr,),
        in_specs=[pl.BlockSpec((tr, 128), lambda i: (i, zf_block)), pl.BlockSpec((1, 128), lambda i: (0, 0))],
        out_specs=pl.BlockSpec((tr, 128), lambda i: (i, 0)), out_shape=jax.ShapeDtypeStruct((S, 128), F32),
        scratch_shapes=[pltpu.VMEM((1, 128), F32)], compiler_params=_params(("arbitrary",)),
    )(z, b_f)


def _forget_cumsum_bwd(z, b_f, dF):
    S = z.shape[0]
    tr = _div_tile(S, TILES["cum"], 8)
    nt = S // tr
    zf_block = ZF // 128

    def body(z_ref, b_ref, df_ref, dz_ref, db_ref, carry):
        @pl.when(pl.program_id(0) == 0)
        def _():
            carry[...] = jnp.zeros_like(carry)
            db_ref[...] = jnp.zeros_like(db_ref)

        dF_ = df_ref[...]
        tri = lax.broadcasted_iota(jnp.int32, (tr, tr), 1) >= lax.broadcasted_iota(jnp.int32, (tr, tr), 0)
        dlf = _dot3(tri, dF_) + carry[...]
        carry[...] += _colsum(dF_)
        lane = lax.broadcasted_iota(jnp.int32, (tr, 128), 1)
        dzf = jnp.where(lane < N_HEADS, dlf * _sigmoid(-(z_ref[...] + b_ref[...])), 0.0)
        dz_ref[...] = dzf.astype(dz_ref.dtype)
        db_ref[...] += _colsum(dzf)

    return pl.pallas_call(
        body, name="forget_cumsum_bwd", grid=(nt,),
        in_specs=[pl.BlockSpec((tr, 128), lambda i: (nt - 1 - i, zf_block)), pl.BlockSpec((1, 128), lambda i: (0, 0)),
                  pl.BlockSpec((tr, 128), lambda i: (nt - 1 - i, 0))],
        out_specs=(pl.BlockSpec((tr, 128), lambda i: (nt - 1 - i, 0)), pl.BlockSpec((1, 128), lambda i: (0, 0))),
        out_shape=(jax.ShapeDtypeStruct((S, 128), BF16), jax.ShapeDtypeStruct((1, 128), F32)),
        scratch_shapes=[pltpu.VMEM((1, 128), F32)], compiler_params=_params(("arbitrary",)),
    )(z, b_f, dF)


NEG = -1e30
ATTN_SCALE = HEAD_DIM ** -0.5


def _attn_scores(q, k, f_row, f_col, qi, kj, t):
    s = _dot_nt(q, k) * ATTN_SCALE + f_row - f_col
    row = qi * t + lax.broadcasted_iota(jnp.int32, (t, t), 0)
    col = kj * t + lax.broadcasted_iota(jnp.int32, (t, t), 1)
    return jnp.where(col <= row, s, NEG)


FIRST, LAST, SECOND_PASS = 1, 2, 4


def _tri_schedule(n, by_key=False, passes=1):
    outer, inner, flags = [], [], []
    for a in range(n):
        partners = list(range(a, n)) if by_key else list(range(a + 1))
        for ps in range(passes):
            for idx, b in enumerate(partners):
                f = FIRST if ps == 0 and idx == 0 else 0
                f |= LAST if ps == passes - 1 and idx == len(partners) - 1 else 0
                f |= SECOND_PASS if ps == 1 else 0
                outer.append(a)
                inner.append(b)
                flags.append(f)
    return [jnp.asarray(np.array(v, np.int32)) for v in (outer, inner, flags)]


def _flash_call(body, name, sched, in_specs, out_specs, out_shape, scratch):
    grid_spec = pltpu.PrefetchScalarGridSpec(
        num_scalar_prefetch=3, grid=(N_HEADS, int(sched[0].shape[0])), in_specs=in_specs, out_specs=out_specs,
        scratch_shapes=scratch)
    return pl.pallas_call(body, name=name, grid_spec=grid_spec, out_shape=out_shape,
                          compiler_params=_params(("parallel", "arbitrary")))


def _flash_fwd(z, f_col, f_row):
    S = z.shape[0]
    t = _div_tile(S, TILES["attn"], 128)
    n = S // t
    qb, kb, vb = ZQ // 128, ZK // 128, ZV // 128

    sched = _tri_schedule(n)

    def body(qt, kt, ft, q_ref, k_ref, v_ref, fq_ref, fk_ref, o_ref, lse_ref, m_sc, l_sc, acc_sc):
        step = pl.program_id(1)
        qi, kj, fl = qt[step], kt[step], ft[step]

        @pl.when((fl & FIRST) != 0)
        def _():
            m_sc[...] = jnp.full_like(m_sc, NEG)
            l_sc[...] = jnp.zeros_like(l_sc)
            acc_sc[...] = jnp.zeros_like(acc_sc)

        s = _attn_scores(q_ref[...], k_ref[...], fq_ref[...], fk_ref[...], qi, kj, t)
        m_new = jnp.maximum(m_sc[...], jnp.max(s, axis=1, keepdims=True))
        alpha = jnp.exp(m_sc[...] - m_new)
        p = jnp.exp(s - m_new)
        l_sc[...] = alpha * l_sc[...] + jnp.sum(p, axis=1, keepdims=True)
        acc_sc[...] = alpha * acc_sc[...] + _dot(p, v_ref[...])
        m_sc[...] = m_new

        @pl.when((fl & LAST) != 0)
        def _():
            o_ref[...] = acc_sc[...] / l_sc[...]
            lse_ref[...] = m_sc[...] + jnp.log(l_sc[...])

    def kv(block):
        return pl.BlockSpec((t, 128), lambda h, s, qt, kt, ft: (kt[s], block + h))

    col = pl.BlockSpec((None, t, 1), lambda h, s, qt, kt, ft: (h, qt[s], 0))
    return _flash_call(
        body, "flash_fwd", sched,
        [pl.BlockSpec((t, 128), lambda h, s, qt, kt, ft: (qt[s], qb + h)), kv(kb), kv(vb), col,
         pl.BlockSpec((None, 1, t), lambda h, s, qt, kt, ft: (h, 0, kt[s]))],
        (pl.BlockSpec((t, 128), lambda h, s, qt, kt, ft: (qt[s], h)), col),
        (jax.ShapeDtypeStruct((S, ATTN_WIDTH), F32), jax.ShapeDtypeStruct((N_HEADS, S, 1), F32)),
        [pltpu.VMEM((t, 1), F32), pltpu.VMEM((t, 1), F32), pltpu.VMEM((t, 128), F32)],
    )(*sched, z, z, z, f_col, f_row)


def _flash_bwd_dq(z, dy, lse, f_col, f_row):
    S = z.shape[0]
    t = _div_tile(S, TILES["attn"], 128)
    sched = _tri_schedule(S // t, passes=2)
    qb, kb, vb = ZQ // 128, ZK // 128, ZV // 128
    dob = POOL_WIDTH // 128

    def body(qt, kt, ft, q_ref, k_ref, v_ref, do_ref, lse_ref, fq_ref, fk_ref, dq_ref, delta_ref, dfq_ref, acc_sc):
        step = pl.program_id(1)
        qi, kj, fl = qt[step], kt[step], ft[step]

        @pl.when((fl & FIRST) != 0)
        def _():
            delta_ref[...] = jnp.zeros_like(delta_ref)
            dfq_ref[...] = jnp.zeros_like(dfq_ref)
            acc_sc[...] = jnp.zeros_like(acc_sc)

        s = _attn_scores(q_ref[...], k_ref[...], fq_ref[...], fk_ref[...], qi, kj, t)
        p = jnp.exp(s - lse_ref[...])
        dp = _dot_nt(do_ref[...], v_ref[...])

        @pl.when((fl & SECOND_PASS) == 0)
        def _():
            delta_ref[...] += jnp.sum(p * dp, axis=1, keepdims=True)

        @pl.when((fl & SECOND_PASS) != 0)
        def _():
            ds = p * (dp - delta_ref[...])
            dfq_ref[...] += jnp.sum(ds, axis=1, keepdims=True)
            acc_sc[...] += _dot(ds, k_ref[...])

        @pl.when((fl & LAST) != 0)
        def _():
            dq_ref[...] = (acc_sc[...] * ATTN_SCALE).astype(dq_ref.dtype)

    def kv(block):
        return pl.BlockSpec((t, 128), lambda h, s, qt, kt, ft: (kt[s], block + h))

    def qs(block):
        return pl.BlockSpec((t, 128), lambda h, s, qt, kt, ft: (qt[s], block + h))

    col = pl.BlockSpec((None, t, 1), lambda h, s, qt, kt, ft: (h, qt[s], 0))
    return _flash_call(
        body, "flash_bwd_dq", sched,
        [qs(qb), kv(kb), kv(vb), qs(dob), col, col, pl.BlockSpec((None, 1, t), lambda h, s, qt, kt, ft: (h, 0, kt[s]))],
        (pl.BlockSpec((t, 128), lambda h, s, qt, kt, ft: (qt[s], h)), col, col),
        (jax.ShapeDtypeStruct((S, ATTN_WIDTH), BF16), jax.ShapeDtypeStruct((N_HEADS, S, 1), F32),
         jax.ShapeDtypeStruct((N_HEADS, S, 1), F32)),
        [pltpu.VMEM((t, 128), F32)],
    )(*sched, z, z, z, dy, lse, f_col, f_row)


def _flash_bwd_dkv(z, dy, lse_row, delta_row, f_col, f_row):
    S = z.shape[0]
    t = _div_tile(S, TILES["attn"], 128)
    sched = _tri_schedule(S // t, by_key=True)
    qb, kb, vb = ZQ // 128, ZK // 128, ZV // 128
    dob = POOL_WIDTH // 128

    def body(kt, qt, ft, q_ref, k_ref, v_ref, do_ref, lse_ref, delta_ref, fq_ref, fk_ref, dk_ref, dv_ref, df_ref,
             dk_sc, dv_sc, df_sc):
        step = pl.program_id(1)
        kj, qi, fl = kt[step], qt[step], ft[step]

        @pl.when((fl & FIRST) != 0)
        def _():
            dk_sc[...] = jnp.zeros_like(dk_sc)
            dv_sc[...] = jnp.zeros_like(dv_sc)
            df_sc[...] = jnp.zeros_like(df_sc)

        st = _dot_nt(k_ref[...], q_ref[...]) * ATTN_SCALE + fq_ref[...] - fk_ref[...]
        krow = kj * t + lax.broadcasted_iota(jnp.int32, (t, t), 0)
        qcol = qi * t + lax.broadcasted_iota(jnp.int32, (t, t), 1)
        pt = jnp.where(krow <= qcol, jnp.exp(st - lse_ref[...]), 0.0)
        dv_sc[...] += _dot(pt, do_ref[...])
        dst = pt * (_dot_nt(v_ref[...], do_ref[...]) - delta_ref[...])
        dk_sc[...] += _dot(dst, q_ref[...])
        df_sc[...] += jnp.sum(dst, axis=1, keepdims=True)

        @pl.when((fl & LAST) != 0)
        def _():
            dk_ref[...] = (dk_sc[...] * ATTN_SCALE).astype(dk_ref.dtype)
            dv_ref[...] = dv_sc[...].astype(dv_ref.dtype)
            df_ref[...] = -df_sc[...]

    def qs(block):
        return pl.BlockSpec((t, 128), lambda h, s, kt, qt, ft: (qt[s], block + h))

    def kv(block):
        return pl.BlockSpec((t, 128), lambda h, s, kt, qt, ft: (kt[s], block + h))

    qrow = pl.BlockSpec((None, 1, t), lambda h, s, kt, qt, ft: (h, 0, qt[s]))
    kcol = pl.BlockSpec((None, t, 1), lambda h, s, kt, qt, ft: (h, kt[s], 0))
    out = pl.BlockSpec((t, 128), lambda h, s, kt, qt, ft: (kt[s], h))
    return _flash_call(
        body, "flash_bwd_dkv", sched, [qs(qb), kv(kb), kv(vb), qs(dob), qrow, qrow, qrow, kcol], (out, out, kcol),
        (jax.ShapeDtypeStruct((S, ATTN_WIDTH), BF16), jax.ShapeDtypeStruct((S, ATTN_WIDTH), BF16),
         jax.ShapeDtypeStruct((N_HEADS, S, 1), F32)),
        [pltpu.VMEM((t, 128), F32), pltpu.VMEM((t, 128), F32), pltpu.VMEM((t, 1), F32)],
    )(*sched, z, z, z, dy, lse_row, delta_row, f_row, f_col)


LRU_HALO = 8


def _lru_gates(ext, cw_ref, cb_ref, wa_ref, ba_ref, wi_ref, bi_ref, lam_ref, tr):
    taps = [pltpu.roll(ext, 3 - k, 0)[LRU_HALO:] if k < 3 else ext[LRU_HALO:] for k in range(4)]
    xc = cb_ref[...] + taps[0] * cw_ref[0:1, :]
    for k in range(1, 4):
        xc = xc + taps[k] * cw_ref[k:k + 1, :]
    ga = jnp.concatenate([_dot(xc[:, g * 128:(g + 1) * 128], wa_ref[g]) for g in range(4)], axis=1) + ba_ref[...]
    gi = jnp.concatenate([_dot(xc[:, g * 128:(g + 1) * 128], wi_ref[g]) for g in range(4)], axis=1) + bi_ref[...]
    r, ig = _sigmoid(ga), _sigmoid(gi)
    nl = -lam_ref[...]
    sp = jnp.maximum(nl, 0.0) + jnp.log(1.0 + jnp.exp(-jnp.abs(nl)))
    la = -LRU_C * r * sp
    a = jnp.exp(la)
    mult = jnp.sqrt(_neg_expm1(2.0 * la))
    return xc, r, ig, sp, a, mult, taps


def _lru_specs(tr, nt, rev):
    hb = tr // LRU_HALO
    ti = (lambda i: nt - 1 - i) if rev else (lambda i: i)
    zx_b, zy_b = ZX // LRU_WIDTH, ZY // LRU_WIDTH
    cur = lambda b: pl.BlockSpec((tr, LRU_WIDTH), lambda i: (ti(i), b))
    prev = lambda b: pl.BlockSpec((LRU_HALO, LRU_WIDTH), lambda i: (jnp.maximum(ti(i) * hb - 1, 0), b))
    vec = pl.BlockSpec((1, LRU_WIDTH), lambda i: (0, 0))
    cw = pl.BlockSpec((4, LRU_WIDTH), lambda i: (0, 0))
    blk = pl.BlockSpec((4, 128, 128), lambda i: (0, 0, 0))
    return ti, cur, prev, vec, cw, blk, zx_b, zy_b


def _lru_fwd(z, cw, cb, wa, ba, wi, bi, lam):
    S = z.shape[0]
    tr = _div_tile(S, TILES["lru"], 8)
    nt = S // tr
    ti, cur, prev, vec, cwspec, blk, zx_b, zy_b = _lru_specs(tr, nt, False)

    def body(zx_ref, halo_ref, zy_ref, cw_ref, cb_ref, wa_ref, ba_ref, wi_ref, bi_ref, lam_ref, y_ref, h_ref, carry):
        i = pl.program_id(0)

        @pl.when(i == 0)
        def _():
            carry[...] = jnp.zeros_like(carry)

        ext = jnp.concatenate([jnp.where(i > 0, halo_ref[...], 0.0), zx_ref[...]], axis=0)
        xc, r, ig, sp, a, mult, _ = _lru_gates(ext, cw_ref, cb_ref, wa_ref, ba_ref, wi_ref, bi_ref, lam_ref, tr)
        A, B = a, mult * (ig * xc)
        row = _rows(tr, LRU_WIDTH)
        d = 1
        while d < tr:
            a_sh = jnp.where(row >= d, pltpu.roll(A, d, 0), 1.0)
            b_sh = jnp.where(row >= d, pltpu.roll(B, d, 0), 0.0)
            B = A * b_sh + B
            A = A * a_sh
            d *= 2
        h = B + A * carry[...]
        h_ref[...] = h
        carry[...] = h_ref[pl.ds(tr - 1, 1), :]
        y_ref[...] = h * _gelu_parts(zy_ref[...])[0]

    out = pl.BlockSpec((tr, LRU_WIDTH), lambda i: (i, 0))
    shape = jax.ShapeDtypeStruct((S, LRU_WIDTH), F32)
    return pl.pallas_call(
        body, name="lru_fwd", grid=(nt,),
        in_specs=[cur(zx_b), prev(zx_b), cur(zy_b), cwspec, vec, blk, vec, blk, vec, vec],
        out_specs=(out, out), out_shape=(shape, shape), scratch_shapes=[pltpu.VMEM((1, LRU_WIDTH), F32)],
        compiler_params=_params(("arbitrary",)),
    )(z, z, z, cw, cb, wa, ba, wi, bi, lam)


def _lru_bwd(z, dy, hs, cw, cb, wa, ba, wi, bi, lam):
    S = z.shape[0]
    tr = _div_tile(S, TILES["lru"], 8)
    nt = S // tr
    ti, cur, prev, vec, cwspec, blk, zx_b, zy_b = _lru_specs(tr, nt, True)
    dy_b = (POOL_WIDTH + ATTN_WIDTH) // LRU_WIDTH

    def body(zx_ref, halo_ref, zy_ref, dy_ref, h_ref, hprev_ref, cw_ref, cb_ref, wa_ref, ba_ref, wi_ref, bi_ref, lam_ref,
             dzx_ref, dzy_ref, dcw_ref, dcb_ref, dwa_ref, dba_ref, dwi_ref, dbi_ref, dlam_ref, gcarry, dxc_next, tmp):
        i = pl.program_id(0)
        t_idx = nt - 1 - i

        @pl.when(i == 0)
        def _():
            gcarry[...] = jnp.zeros_like(gcarry)
            dxc_next[...] = jnp.zeros_like(dxc_next)
            for ref in (dcw_ref, dcb_ref, dwa_ref, dba_ref, dwi_ref, dbi_ref, dlam_ref):
                ref[...] = jnp.zeros_like(ref)

        ext = jnp.concatenate([jnp.where(t_idx > 0, halo_ref[...], 0.0), zx_ref[...]], axis=0)
        xc, r, ig, sp, a, mult, taps = _lru_gates(ext, cw_ref, cb_ref, wa_ref, ba_ref, wi_ref, bi_ref, lam_ref, tr)
        h = h_ref[...]
        gel, dgel = _gelu_parts(zy_ref[...])
        dy_ = dy_ref[...]
        dzy_ref[...] = (dy_ * h * dgel).astype(dzy_ref.dtype)
        row = _rows(tr, LRU_WIDTH)
        B = dy_ * gel + jnp.where(row == tr - 1, gcarry[...], 0.0)
        A = jnp.where(row < tr - 1, pltpu.roll(a, tr - 1, 0), 0.0)
        d = 1
        while d < tr:
            keep = row < tr - d
            b_sh = jnp.where(keep, pltpu.roll(B, tr - d, 0), 0.0)
            a_sh = jnp.where(keep, pltpu.roll(A, tr - d, 0), 0.0)
            B = B + A * b_sh
            A = A * a_sh
            d *= 2
        g = B
        tmp[...] = a * g
        gcarry[...] = tmp[pl.ds(0, 1), :]
        h_ext = jnp.concatenate([jnp.where(t_idx > 0, hprev_ref[...], 0.0), h], axis=0)
        hprev = pltpu.roll(h_ext, 1, 0)[LRU_HALO:]
        t1 = g * mult
        dig = t1 * xc
        dxc = t1 * ig
        dla = (g * hprev) * a - (g * (ig * xc)) * (a * a) / mult
        dr = dla * (-LRU_C * sp)
        dga = dr * r * (1.0 - r)
        dgi = dig * ig * (1.0 - ig)
        dlam_ref[...] += _colsum(dla * (-LRU_C * r)) * (-_sigmoid(-lam_ref[...]))
        dba_ref[...] += _colsum(dga)
        dbi_ref[...] += _colsum(dgi)
        parts = []
        for gidx in range(4):
            cols = slice(gidx * 128, (gidx + 1) * 128)
            xct = xc[:, cols].T
            dwa_ref[gidx] += _dot(xct, dga[:, cols])
            dwi_ref[gidx] += _dot(xct, dgi[:, cols])
            parts.append(_dot_nt(dga[:, cols], wa_ref[gidx]) + _dot_nt(dgi[:, cols], wi_ref[gidx]))
        dxc = dxc + jnp.concatenate(parts, axis=1)
        dcb_ref[...] += _colsum(dxc)
        for k in range(4):
            dcw_ref[k:k + 1, :] += _colsum(dxc * taps[k])
        ext_d = jnp.concatenate([dxc, dxc_next[...]], axis=0)
        dzx = dxc * cw_ref[3:4, :]
        for k in range(3):
            dzx = dzx + pltpu.roll(ext_d, tr + LRU_HALO - (3 - k), 0)[:tr] * cw_ref[k:k + 1, :]
        dzx_ref[...] = dzx.astype(dzx_ref.dtype)
        dxc_next[...] = dxc[:LRU_HALO]

    rev = pl.BlockSpec((tr, LRU_WIDTH), lambda i: (nt - 1 - i, 0))
    hb = tr // LRU_HALO
    hprev_spec = pl.BlockSpec((LRU_HALO, LRU_WIDTH), lambda i: (jnp.maximum((nt - 1 - i) * hb - 1, 0), 0))
    dy_spec = pl.BlockSpec((tr, LRU_WIDTH), lambda i: (nt - 1 - i, dy_b))
    vshape = jax.ShapeDtypeStruct((1, LRU_WIDTH), F32)
    bshape = jax.ShapeDtypeStruct((4, 128, 128), F32)
    return pl.pallas_call(
        body, name="lru_bwd", grid=(nt,),
        in_specs=[cur(zx_b), prev(zx_b), cur(zy_b), dy_spec, rev, hprev_spec, cwspec, vec, blk, vec, blk, vec, vec],
        out_specs=(rev, rev, cwspec, vec, blk, vec, blk, vec, vec),
        out_shape=(jax.ShapeDtypeStruct((S, LRU_WIDTH), BF16), jax.ShapeDtypeStruct((S, LRU_WIDTH), BF16),
                   jax.ShapeDtypeStruct((4, LRU_WIDTH), F32), vshape, bshape, vshape, bshape, vshape, vshape),
        scratch_shapes=[pltpu.VMEM((1, LRU_WIDTH), F32), pltpu.VMEM((LRU_HALO, LRU_WIDTH), F32),
                        pltpu.VMEM((tr, LRU_WIDTH), F32)],
        compiler_params=_params(("arbitrary",)),
    )(z, z, z, dy, hs, hs, cw, cb, wa, ba, wi, bi, lam)


FFN_HALO = 8


def _ffn_act(au, cw, cb):
    S, F2 = au.shape
    F = F2 // 2
    tc = FFN_CHUNK
    tr = _div_tile(S, TILES["row"], 8)
    hb = tr // FFN_HALO

    def body(au_ref, halo_ref, cw_ref, cb_ref, p_ref):
        i = pl.program_id(0)
        a_ = au_ref[:, :tc]
        ext = jnp.concatenate([jnp.where(i > 0, halo_ref[:, :tc], 0.0), a_], axis=0)
        gc = cb_ref[...] + a_ * cw_ref[2:3, :]
        for k in range(2):
            gc = gc + pltpu.roll(ext, 2 - k, 0)[FFN_HALO:] * cw_ref[k:k + 1, :]
        p_ref[...] = (gc * _sigmoid(gc) * au_ref[:, tc:]).astype(p_ref.dtype)

    return pl.pallas_call(
        body, name="ffn_act", grid=(S // tr, F // tc),
        in_specs=[pl.BlockSpec((tr, 2 * tc), lambda i, j: (i, j)),
                  pl.BlockSpec((FFN_HALO, 2 * tc), lambda i, j: (jnp.maximum(i * hb - 1, 0), j)),
                  pl.BlockSpec((3, tc), lambda i, j: (0, j)), pl.BlockSpec((1, tc), lambda i, j: (0, j))],
        out_specs=pl.BlockSpec((tr, tc), lambda i, j: (i, j)), out_shape=jax.ShapeDtypeStruct((S, F), BF16),
        compiler_params=_params(("parallel", "parallel")),
    )(au, au, cw, cb)


def _ffn_act_bwd(au, dp, cw, cb):
    S, F2 = au.shape
    F = F2 // 2
    tc = FFN_CHUNK
    tr = _div_tile(S, TILES["ew"], 8)
    hb = tr // FFN_HALO
    nt = S // tr
    H = FFN_HALO

    def body(au_ref, prev_ref, next_ref, dp_ref, dpn_ref, cw_ref, cb_ref, dau_ref, dcw_ref, dcb_ref):
        i = pl.program_id(1)
        last = i == nt - 1
        a_ext = jnp.concatenate([jnp.where(i > 0, prev_ref[:, :tc], 0.0), au_ref[:, :tc], next_ref[:, :tc]], axis=0)
        u_ext = jnp.concatenate([au_ref[:, tc:], next_ref[:, tc:]], axis=0)
        dp_ext = jnp.concatenate([dp_ref[...], jnp.where(last, 0.0, dpn_ref[...])], axis=0)
        taps = [pltpu.roll(a_ext, 2 - k, 0)[H:] if k < 2 else a_ext[H:] for k in range(3)]
        gc = cb_ref[...] + taps[0] * cw_ref[0:1, :] + taps[1] * cw_ref[1:2, :] + taps[2] * cw_ref[2:3, :]
        sig = _sigmoid(gc)
        dgc = dp_ext * u_ext * (sig * (1.0 + gc * (1.0 - sig)))
        da = dgc[:tr] * cw_ref[2:3, :]
        for k in range(2):
            da = da + pltpu.roll(dgc, tr + H - (2 - k), 0)[:tr] * cw_ref[k:k + 1, :]
        dau_ref[:, :tc] = da.astype(dau_ref.dtype)
        dau_ref[:, tc:] = (dp_ref[...] * (gc[:tr] * sig[:tr])).astype(dau_ref.dtype)

        @pl.when(i == 0)
        def _():
            dcw_ref[...] = jnp.zeros_like(dcw_ref)
            dcb_ref[...] = jnp.zeros_like(dcb_ref)

        dcb_ref[...] += _colsum(dgc[:tr])
        for k in range(3):
            dcw_ref[k:k + 1, :] += _colsum(dgc[:tr] * taps[k][:tr])

    return pl.pallas_call(
        body, name="ffn_act_bwd", grid=(F // tc, nt),
        in_specs=[pl.BlockSpec((tr, 2 * tc), lambda j, i: (i, j)),
                  pl.BlockSpec((H, 2 * tc), lambda j, i: (jnp.maximum(i * hb - 1, 0), j)),
                  pl.BlockSpec((H, 2 * tc), lambda j, i: (jnp.minimum((i + 1) * hb, nt * hb - 1), j)),
                  pl.BlockSpec((tr, tc), lambda j, i: (i, j)),
                  pl.BlockSpec((H, tc), lambda j, i: (jnp.minimum((i + 1) * hb, nt * hb - 1), j)),
                  pl.BlockSpec((3, tc), lambda j, i: (0, j)), pl.BlockSpec((1, tc), lambda j, i: (0, j))],
        out_specs=(pl.BlockSpec((tr, 2 * tc), lambda j, i: (i, j)), pl.BlockSpec((3, tc), lambda j, i: (0, j)),
                   pl.BlockSpec((1, tc), lambda j, i: (0, j))),
        out_shape=(jax.ShapeDtypeStruct((S, F2), BF16), jax.ShapeDtypeStruct((3, F), F32), jax.ShapeDtypeStruct((1, F), F32)),
        compiler_params=_params(("parallel", "arbitrary")),
    )(au, au, au, dp, dp, cw, cb)


def _sum_parts(parts, sel, rows, cols, name):
    tr = _div_tile(rows, 2048, 8)

    def body(sel_ref, *refs):
        acc = refs[0][...]
        for r in refs[1:-1]:
            acc = acc + r[...]
        refs[-1][...] = acc

    def spec(index):
        if isinstance(index, int):
            return pl.BlockSpec((None, tr, cols), lambda i, s: (index, i, 0))
        k, mul, off = index
        return pl.BlockSpec((None, tr, cols), lambda i, s: (s[k] * mul + off, i, 0))

    grid_spec = pltpu.PrefetchScalarGridSpec(
        num_scalar_prefetch=1, grid=(rows // tr,), in_specs=[spec(ix) for _, ix in parts],
        out_specs=pl.BlockSpec((tr, cols), lambda i, s: (i, 0)))
    return pl.pallas_call(
        body, name=name, grid_spec=grid_spec, out_shape=jax.ShapeDtypeStruct((rows, cols), F32),
        compiler_params=_params(("parallel",)),
    )(sel, *[a for a, _ in parts])


def _pair_sum(g8, got, sel):
    n, rows, cols = got.shape
    tr = _div_tile(rows, 2048, 8)

    def body(sel_ref, a_ref, b_ref, o_ref):
        o_ref[...] = a_ref[...] + b_ref[...]

    grid_spec = pltpu.PrefetchScalarGridSpec(
        num_scalar_prefetch=1, grid=(n, rows // tr),
        in_specs=[pl.BlockSpec((None, tr, cols), lambda q, i, s: (2 * q + s[0], i, 0)),
                  pl.BlockSpec((None, tr, cols), lambda q, i, s: (q, i, 0))],
        out_specs=pl.BlockSpec((None, tr, cols), lambda q, i, s: (q, i, 0)))
    return pl.pallas_call(
        body, name="pair_sum", grid_spec=grid_spec, out_shape=jax.ShapeDtypeStruct((n, rows, cols), F32),
        compiler_params=_params(("parallel", "parallel")),
    )(sel, g8, got)


def _adamw_math(w, g, m, v):
    m2 = ADAM_B1 * m + (1.0 - ADAM_B1) * g
    v2 = ADAM_B2 * v + (1.0 - ADAM_B2) * (g * g)
    m_hat = m2 / (1.0 - ADAM_B1 ** ADAM_STEP)
    v_hat = v2 / (1.0 - ADAM_B2 ** ADAM_STEP)
    return -ADAM_LR * (m_hat / (jnp.sqrt(v_hat) + ADAM_EPS) + ADAM_WD * w), m2, v2


def _adamw(w, g, m, v):
    R, C = w.shape
    tr = _div_tile(R, TILES["ew"], 8)

    def body(w_ref, g_ref, m_ref, v_ref, d_ref, m2_ref, v2_ref):
        d_ref[...], m2_ref[...], v2_ref[...] = _adamw_math(w_ref[...], g_ref[...], m_ref[...], v_ref[...])

    spec = pl.BlockSpec((tr, C), lambda i: (i, 0))
    shape = jax.ShapeDtypeStruct((R, C), F32)
    return pl.pallas_call(
        body, name="adamw", grid=(R // tr,), in_specs=[spec] * 4, out_specs=(spec,) * 3, out_shape=(shape,) * 3,
        compiler_params=_params(("parallel",)),
    )(w, g, m, v)


def _ada_grad_adamw(cact_t, dmod, w, m, v):
    L, D, N = w.shape
    tm, tn = _div_tile(D, 256, 8), _div_tile(N, 1024, 128)

    def body(c_ref, d_ref, w_ref, m_ref, v_ref, g_ref, dl_ref, m2_ref, v2_ref):
        g = c_ref[:, 0:1] * d_ref[0:1, :]
        for b in range(1, N_DEV):
            g = g + c_ref[:, b:b + 1] * d_ref[b:b + 1, :]
        g_ref[...] = g
        dl_ref[...], m2_ref[...], v2_ref[...] = _adamw_math(w_ref[...], g, m_ref[...], v_ref[...])

    big = pl.BlockSpec((None, tm, tn), lambda l, i, j: (l, i, j))
    shape = jax.ShapeDtypeStruct((L, D, N), F32)
    return pl.pallas_call(
        body, name="ada_grad_adamw", grid=(L, D // tm, N // tn),
        in_specs=[pl.BlockSpec((tm, N_DEV), lambda l, i, j: (i, 0)),
                  pl.BlockSpec((None, N_DEV, tn), lambda l, i, j: (l, 0, j)), big, big, big],
        out_specs=(big,) * 4, out_shape=(shape,) * 4, compiler_params=_params(("parallel", "parallel", "parallel")),
    )(cact_t, dmod, w, m, v)


def _silu_rows(c):
    def body(c_ref, o_ref):
        x = c_ref[...]
        o_ref[...] = x * _sigmoid(x)

    return pl.pallas_call(body, name="silu_rows", out_shape=jax.ShapeDtypeStruct(c.shape, F32))(c)


ANY = pl.BlockSpec(memory_space=pl.ANY)


def _position():
    return lax.axis_index("x"), lax.axis_index("y"), lax.axis_index("c")


def _other_chips(x, y):
    return [(1 - x, y), (x, 1 - y), (1 - x, 1 - y)]


def _allgather8(v):
    R, C = v.shape

    def body(v_ref, out_ref, send_sems, recv_sems, local_sem):
        x, y, c = _position()
        me = 4 * x + 2 * y + c
        mine = pltpu.make_async_copy(v_ref, out_ref.at[me], local_sem)
        mine.start()
        sends, recvs = [], []
        for k in range(1, N_DEV):
            px, py, pc = (x + (k >> 2)) % 2, (y + ((k >> 1) & 1)) % 2, (c + (k & 1)) % 2
            sends.append(pltpu.make_async_remote_copy(
                src_ref=v_ref, dst_ref=out_ref.at[me], send_sem=send_sems.at[k - 1], recv_sem=recv_sems.at[k - 1],
                device_id=(px, py, pc), device_id_type=MESH))
            recvs.append(pltpu.make_async_remote_copy(
                src_ref=v_ref, dst_ref=out_ref.at[4 * px + 2 * py + pc], send_sem=send_sems.at[k - 1],
                recv_sem=recv_sems.at[k - 1], device_id=(px, py, pc), device_id_type=MESH))
        for cp in sends:
            cp.start()
        for cp in recvs:
            cp.wait_recv()
        for cp in sends:
            cp.wait_send()
        mine.wait()

    return pl.pallas_call(
        body, name="comm_allgather8", out_shape=jax.ShapeDtypeStruct((N_DEV, R, C), v.dtype), in_specs=[ANY],
        out_specs=ANY,
        scratch_shapes=[pltpu.SemaphoreType.DMA((N_DEV - 1,)), pltpu.SemaphoreType.DMA((N_DEV - 1,)),
                        pltpu.SemaphoreType.DMA],
    )(v)


def _chip_allgather(v):
    R, C = v.shape
    half = R // 2

    def body(v_ref, out_ref, send_sems, recv_sems, local_sem):
        x, y, c = _position()
        q = 2 * x + y
        chips = _other_chips(x, y)
        my_half, other_half = pl.ds(c * half, half), pl.ds((1 - c) * half, half)
        mine = pltpu.make_async_copy(v_ref, out_ref.at[q], local_sem)
        mine.start()

        def copy(k, src, dst, to):
            return pltpu.make_async_remote_copy(src_ref=src, dst_ref=dst, send_sem=send_sems.at[k],
                                                recv_sem=recv_sems.at[k], device_id=to, device_id_type=MESH)

        first = [copy(j, v_ref.at[my_half], out_ref.at[q, my_half], (*chip, c)) for j, chip in enumerate(chips)]
        for cp in first:
            cp.start()
        passed = []
        for j, (px, py) in enumerate(chips):
            landed = out_ref.at[2 * px + py, my_half]
            copy(j, landed, landed, (px, py, c)).wait_recv()
            cp = copy(3 + j, landed, landed, (x, y, 1 - c))
            cp.start()
            passed.append(cp)
        for j, (px, py) in enumerate(chips):
            landed = out_ref.at[2 * px + py, other_half]
            copy(3 + j, landed, landed, (x, y, 1 - c)).wait_recv()
        for cp in first + passed:
            cp.wait_send()
        mine.wait()

    return pl.pallas_call(
        body, name="comm_chip_allgather", out_shape=jax.ShapeDtypeStruct((N_CHIPS, R, C), v.dtype), in_specs=[ANY],
        out_specs=ANY,
        scratch_shapes=[pltpu.SemaphoreType.DMA((6,)), pltpu.SemaphoreType.DMA((6,)), pltpu.SemaphoreType.DMA],
    )(v)


def _sibling_swap_halves(g):
    n, R, C = g.shape
    half = R // 2

    def body(g_ref, out_ref, send_sem, recv_sem):
        x, y, c = _position()
        cp = pltpu.make_async_remote_copy(
            src_ref=g_ref.at[:, pl.ds((1 - c) * half, half), :], dst_ref=out_ref, send_sem=send_sem, recv_sem=recv_sem,
            device_id=(x, y, 1 - c), device_id_type=MESH)
        cp.start()
        cp.wait()

    return pl.pallas_call(
        body, name="comm_sibling_swap", out_shape=jax.ShapeDtypeStruct((n, half, C), g.dtype), in_specs=[ANY],
        out_specs=ANY, scratch_shapes=[pltpu.SemaphoreType.DMA, pltpu.SemaphoreType.DMA],
    )(g)


def _chip_scatter(s):
    n, R, C = s.shape

    def body(s_ref, out_ref, send_sems, recv_sems):
        x, y, c = _position()
        cps = [pltpu.make_async_remote_copy(
            src_ref=s_ref.at[2 * px + py], dst_ref=out_ref.at[j], send_sem=send_sems.at[j], recv_sem=recv_sems.at[j],
            device_id=(px, py, c), device_id_type=MESH) for j, (px, py) in enumerate(_other_chips(x, y))]
        for cp in cps:
            cp.start()
        for cp in cps:
            cp.wait()

    return pl.pallas_call(
        body, name="comm_chip_scatter", out_shape=jax.ShapeDtypeStruct((3, R, C), s.dtype), in_specs=[ANY],
        out_specs=ANY, scratch_shapes=[pltpu.SemaphoreType.DMA((3,)), pltpu.SemaphoreType.DMA((3,))],
    )(s)


def _sibling_allgather(v):
    R, C = v.shape

    def body(v_ref, out_ref, send_sem, recv_sem, local_sem):
        x, y, c = _position()
        mine = pltpu.make_async_copy(v_ref, out_ref.at[c], local_sem)
        mine.start()
        send = pltpu.make_async_remote_copy(src_ref=v_ref, dst_ref=out_ref.at[c], send_sem=send_sem, recv_sem=recv_sem,
                                            device_id=(x, y, 1 - c), device_id_type=MESH)
        recv = pltpu.make_async_remote_copy(src_ref=v_ref, dst_ref=out_ref.at[1 - c], send_sem=send_sem,
                                            recv_sem=recv_sem, device_id=(x, y, 1 - c), device_id_type=MESH)
        send.start()
        recv.wait_recv()
        send.wait_send()
        mine.wait()

    return pl.pallas_call(
        body, name="comm_sibling_allgather", out_shape=jax.ShapeDtypeStruct((2, R, C), v.dtype), in_specs=[ANY],
        out_specs=ANY, scratch_shapes=[pltpu.SemaphoreType.DMA, pltpu.SemaphoreType.DMA, pltpu.SemaphoreType.DMA],
    )(v)


def _reduce_scatter(g, c_idx, q_idx):
    n, R, C = g.shape
    half = R // 2
    got = _sibling_swap_halves(g)
    sel = jnp.stack([c_idx, q_idx]).astype(jnp.int32)
    g8 = g.reshape(2 * n, half, C)
    pair = _pair_sum(g8, got, sel)
    arrived = _chip_scatter(pair)
    mine = _sum_parts([(pair, (1, 1, 0)), (arrived, 0), (arrived, 1), (arrived, 2)], sel, half, C, "chip_sum")
    return _sibling_allgather(mine).reshape(R, C)


def _layer_fwd(x, mod, p):
    sh1, sc1, gt1, sh2, sc2, gt2 = mod
    h1 = _norm_mod(x, p["g_mix"], sc1, sh1)
    z = _matmul(h1, p["w_in"], name="mm_in")
    y_pool = _pool_fwd(z, p["pool_w"], p["pool_scale"])
    F = _forget_cumsum(z, p["b_f"])
    Fh = F[:, :N_HEADS].T
    f_col, f_row = Fh[:, :, None], Fh[:, None, :]
    o, lse = _flash_fwd(z, f_col, f_row)
    y_lru, hs = _lru_fwd(z, p["lru_conv_w"], p["lru_conv_b"], p["lru_wa"], p["lru_ba"], p["lru_wi"], p["lru_bi"],
                         p["lru_lambda"])
    y = jnp.concatenate([y_pool.astype(BF16), o.astype(BF16), y_lru.astype(BF16)], axis=1)
    m1, x_mid = _matmul(y, p["w_out"], res=x, gate=gt1, name="mm_out")
    h2 = _norm_mod(x_mid, p["g_ffn"], sc2, sh2)
    au = _matmul(h2, p["w_gu"], name="mm_gu")
    pa = _ffn_act(au, p["ffn_conv_w"], p["ffn_conv_b"])
    m2, x_out = _matmul(pa, p["w_down"], res=x_mid, gate=gt2, name="mm_down")
    saved = dict(x=x, h1=h1, z=z, f_col=f_col, f_row=f_row, lse=lse, hs=hs, y=y, m1=m1, x_mid=x_mid, h2=h2, au=au,
                 pa=pa, m2=m2)
    return x_out, saved


def _layer_bwd(dx_out, mod, p, s):
    sh1, sc1, gt1, sh2, sc2, gt2 = mod
    g = {}
    dm2, dgt2 = _gate_bwd(dx_out, s["m2"], gt2)
    dpa = _matmul(dm2, p["w_down"], nt=True, name="mm_down_dx")
    g["w_down"] = _matmul(s["pa"].T, dm2, name="mm_down_dw")
    dau, g["ffn_conv_w"], g["ffn_conv_b"] = _ffn_act_bwd(s["au"], dpa, p["ffn_conv_w"], p["ffn_conv_b"])
    dh2 = _matmul(dau, p["w_gu"], nt=True, name="mm_gu_dx")
    g["w_gu"] = _matmul(s["h2"].T, dau, name="mm_gu_dw")
    dx_mid, g["g_ffn"], dsc2, dsh2 = _norm_mod_bwd(s["x_mid"], dh2, dx_out, p["g_ffn"], sc2)
    dm1, dgt1 = _gate_bwd(dx_mid, s["m1"], gt1)
    dy = _matmul(dm1, p["w_out"], nt=True, name="mm_out_dx")
    g["w_out"] = _matmul(s["y"].T, dm1, name="mm_out_dw")
    z = s["z"]
    (dzx, dzy, g["lru_conv_w"], g["lru_conv_b"], g["lru_wa"], g["lru_ba"], g["lru_wi"], g["lru_bi"],
     g["lru_lambda"]) = _lru_bwd(z, dy, s["hs"], p["lru_conv_w"], p["lru_conv_b"], p["lru_wa"], p["lru_ba"], p["lru_wi"],
                                 p["lru_bi"], p["lru_lambda"])
    dq, delta, dfq = _flash_bwd_dq(z, dy, s["lse"], s["f_col"], s["f_row"])
    row = lambda a: jnp.transpose(a, (0, 2, 1))
    dk, dv, dfk = _flash_bwd_dkv(z, dy, row(s["lse"]), row(delta), s["f_col"], s["f_row"])
    dF_pad = jnp.pad((dfq + dfk)[:, :, 0].T, ((0, 0), (0, 128 - N_HEADS)))
    dzf, db_f = _forget_cumsum_bwd(z, p["b_f"], dF_pad)
    g["b_f"] = db_f[:, :N_HEADS]
    dzp, g["pool_w"], g["pool_scale"] = _pool_bwd(z, dy, p["pool_w"], p["pool_scale"])
    S = z.shape[0]
    dz = jnp.concatenate([dzp, dq, dk, dv, dzx, dzy, dzf, jnp.zeros((S, ZW - ZF - 128), BF16)], axis=1)
    dh1 = _matmul(dz, p["w_in"], nt=True, name="mm_in_dx")
    g["w_in"] = _matmul(s["h1"].T, dz, name="mm_in_dw")
    dx_in, g["g_mix"], dsc1, dsh1 = _norm_mod_bwd(s["x"], dh1, dx_mid, p["g_mix"], sc1)
    return dx_in, g, (dsh1, dsc1, dgt1, dsh2, dsc2, dgt2)


def _local_step(x, target, mods, layers, final_g):
    saved = []
    for mod, p in zip(mods, layers):
        x, s = _layer_fwd(x, mod, p)
        saved.append(s)
    dx, dfinal_g, loss = _loss_head(x, target, final_g)
    grads, dmods = [None] * len(layers), [None] * len(layers)
    for l in reversed(range(len(layers))):
        dx, grads[l], dmods[l] = _layer_bwd(dx, mods[l], layers[l], saved[l])
    return loss, dx, grads, dmods, dfinal_g


def _pad_in_cols(w):
    D = w.shape[0]
    return jnp.concatenate([w[:, :3584], w[:, 3592:N_IN], w[:, 3584:3592], jnp.zeros((D, ZW - N_IN), w.dtype)], axis=1)


def _unpad_in_cols(w):
    return jnp.concatenate([w[:, :3584], w[:, ZF:ZF + N_HEADS], w[:, 3584:ZF]], axis=1)


def _interleave(wg, wu):
    D, F = wg.shape
    n = F // FFN_CHUNK
    return jnp.stack([wg.reshape(D, n, FFN_CHUNK), wu.reshape(D, n, FFN_CHUNK)], axis=2).reshape(D, 2 * F)


def _deinterleave(w):
    D, F2 = w.shape
    w4 = w.reshape(D, F2 // (2 * FFN_CHUNK), 2, FFN_CHUNK)
    return w4[:, :, 0].reshape(D, F2 // 2), w4[:, :, 1].reshape(D, F2 // 2)


def _cols_to_chips(w):
    R, N = w.shape
    return jnp.transpose(w.reshape(R, N_CHIPS, N // N_CHIPS), (1, 0, 2)).reshape(N_CHIPS, -1)


def _chips_to_cols(w, R):
    return jnp.transpose(w.reshape(N_CHIPS, R, -1), (1, 0, 2)).reshape(R, -1)


BIG = ("w_in", "w_out", "w_ffn_gate", "w_ffn_up", "w_ffn_down")
SMALL = ("b_ada", "g_mix", "b_f", "pool_w", "pool_scale", "lru_conv_w", "lru_conv_b", "lru_wa", "lru_ba", "lru_wi",
         "lru_bi", "lru_lambda", "g_ffn", "ffn_conv_w", "ffn_conv_b", "final_g")
SHARDED_SMALL = ("lru_conv_w", "ffn_conv_w")
WEIGHTS = ("w_ada", "b_ada", "g_mix", "w_in", "b_f", "pool_w", "pool_scale", "lru_conv_w", "lru_conv_b", "lru_wa",
           "lru_ba", "lru_wi", "lru_bi", "lru_lambda", "w_out", "g_ffn", "w_ffn_gate", "w_ffn_up", "ffn_conv_w",
           "ffn_conv_b", "w_ffn_down", "final_g")


def _pack(arrays):
    flat = jnp.concatenate([a.reshape(-1).astype(F32) for a in arrays])
    n = -(-flat.shape[0] // 1024) * 1024
    return jnp.pad(flat, (0, n - flat.shape[0])).reshape(n // 128, 128)


def _unpack(packed, shapes):
    flat = packed.reshape(-1)
    out, off = [], 0
    for shp in shapes:
        n = int(np.prod(shp))
        out.append(flat[off:off + n].reshape(shp))
        off += n
    return out


def kernel(x, c, w_ada, b_ada, g_mix, w_in, b_f, pool_w, pool_scale, lru_conv_w, lru_conv_b, lru_wa, lru_ba, lru_wi, lru_bi, lru_lambda, w_out, g_ffn, w_ffn_gate, w_ffn_up, ffn_conv_w, ffn_conv_b, w_ffn_down, final_g, loss_target, m_w_ada, m_b_ada, m_g_mix, m_w_in, m_b_f, m_pool_w, m_pool_scale, m_lru_conv_w, m_lru_conv_b, m_lru_wa, m_lru_ba, m_lru_wi, m_lru_bi, m_lru_lambda, m_w_out, m_g_ffn, m_w_ffn_gate, m_w_ffn_up, m_ffn_conv_w, m_ffn_conv_b, m_w_ffn_down, m_final_g, v_w_ada, v_b_ada, v_g_mix, v_w_in, v_b_f, v_pool_w, v_pool_scale, v_lru_conv_w, v_lru_conv_b, v_lru_wa, v_lru_ba, v_lru_wi, v_lru_bi, v_lru_lambda, v_w_out, v_g_ffn, v_w_ffn_gate, v_w_ffn_up, v_ffn_conv_w, v_ffn_conv_b, v_w_ffn_down, v_final_g):
    env = dict(locals())
    W = {n: env[n] for n in WEIGHTS}
    M = {n: env["m_" + n] for n in WEIGHTS}
    V = {n: env["v_" + n] for n in WEIGHTS}
    L, D = g_mix.shape
    S = x.shape[1]
    F = 4 * w_ffn_gate.shape[2]
    ix, iy, ic = lax.axis_index("x"), lax.axis_index("y"), lax.axis_index("c")
    q = 2 * ix + iy
    me = 2 * q + ic

    head = _allgather8(_pack([c, lru_conv_w, ffn_conv_w]))
    nlc, nfc = lru_conv_w.size, ffn_conv_w.size
    c_all = head.reshape(N_DEV, -1)[:, :D]
    lru_cw = jnp.concatenate([head[2 * k].reshape(-1)[D:D + nlc].reshape(L, 4, -1) for k in range(N_CHIPS)], axis=2)
    ffn_cw = jnp.concatenate([head[2 * k].reshape(-1)[D + nlc:D + nlc + nfc].reshape(L, 3, -1) for k in range(N_CHIPS)],
                             axis=2)
    cact = _silu_rows(c_all)

    NA = w_ada.shape[2]
    mod_part = jnp.stack([_matmul(cact, w_ada[l], name="mm_ada") for l in range(L)])
    mod_all = _allgather8(mod_part.reshape(L * N_DEV, NA)).reshape(N_DEV, L, N_DEV, NA)
    mod_full = jnp.concatenate([mod_all[2 * k] for k in range(N_CHIPS)], axis=2)
    mod_mine = lax.dynamic_index_in_dim(mod_full, me, axis=1, keepdims=False) + b_ada
    mods = [[mod_mine[l, k * D:(k + 1) * D].reshape(1, D) for k in range(6)] for l in range(L)]

    layers = []
    for l in range(L):
        local = jnp.concatenate([W[n][l].reshape(-1).astype(BF16) for n in BIG]).reshape(-1, 128)
        full = _chip_allgather(local).reshape(N_CHIPS, -1)
        off, parts = 0, {}
        for n in BIG:
            size = W[n][l].size
            parts[n] = full[:, off:off + size]
            off += size
        wg, wu = _chips_to_cols(parts["w_ffn_gate"], D), _chips_to_cols(parts["w_ffn_up"], D)
        layers.append(dict(
            w_in=_pad_in_cols(_chips_to_cols(parts["w_in"], D)), w_out=parts["w_out"].reshape(D, D),
            w_gu=_interleave(wg, wu), w_down=parts["w_ffn_down"].reshape(F, D),
            g_mix=g_mix[l][None], g_ffn=g_ffn[l][None], b_f=jnp.pad(b_f[l], (0, 128 - N_HEADS))[None],
            pool_w=pool_w[l], pool_scale=pool_scale[l][None], lru_conv_w=lru_cw[l], lru_conv_b=lru_conv_b[l][None],
            lru_wa=lru_wa[l], lru_ba=lru_ba[l][None], lru_wi=lru_wi[l], lru_bi=lru_bi[l][None],
            lru_lambda=lru_lambda[l][None], ffn_conv_w=ffn_cw[l], ffn_conv_b=ffn_conv_b[l][None]))

    loss, dx, grads, dmods, dfinal_g = _local_step(x[0], loss_target[0], mods, layers, final_g[None])

    G = {n: [] for n in BIG}
    for l in range(L):
        gl = grads[l]
        dwg, dwu = _deinterleave(gl["w_gu"])
        flat = jnp.concatenate([
            _cols_to_chips(_unpad_in_cols(gl["w_in"])), gl["w_out"].reshape(N_CHIPS, -1), _cols_to_chips(dwg),
            _cols_to_chips(dwu), gl["w_down"].reshape(N_CHIPS, -1)], axis=1)
        red = _reduce_scatter(flat.reshape(N_CHIPS, -1, 128), ic, q).reshape(-1)
        off = 0
        for n in BIG:
            size = W[n][l].size
            G[n].append(red[off:off + size].reshape(W[n][l].shape))
            off += size
    G = {n: jnp.stack(G[n]) for n in BIG}

    stack = lambda name: jnp.stack([grads[l][name] for l in range(L)])
    dmod = jnp.stack([jnp.concatenate(dmods[l], axis=1)[0] for l in range(L)])
    small = dict(b_ada=dmod, g_mix=stack("g_mix"), b_f=stack("b_f"), pool_w=stack("pool_w"),
                 pool_scale=stack("pool_scale"), lru_conv_w=stack("lru_conv_w"), lru_conv_b=stack("lru_conv_b"),
                 lru_wa=stack("lru_wa"), lru_ba=stack("lru_ba"), lru_wi=stack("lru_wi"), lru_bi=stack("lru_bi"),
                 lru_lambda=stack("lru_lambda"), g_ffn=stack("g_ffn"), ffn_conv_w=stack("ffn_conv_w"),
                 ffn_conv_b=stack("ffn_conv_b"), final_g=dfinal_g)
    packed = _pack([small[n] for n in SMALL] + [loss[0, :1]])
    everyone = _allgather8(packed)
    zero_sel = jnp.zeros((2,), jnp.int32)
    total = _sum_parts([(everyone, k) for k in range(N_DEV)], zero_sel, packed.shape[0], 128, "device_sum")
    sums = _unpack(total, [small[n].shape for n in SMALL] + [(1,)])
    loss_total = sums[-1][0]
    for n, a in zip(SMALL, sums[:-1]):
        a = a.reshape((L, -1, a.shape[-1])) if n in SHARDED_SMALL else a.reshape(W[n].shape)
        if n in SHARDED_SMALL:
            a = lax.dynamic_slice_in_dim(a, q * W[n].shape[2], W[n].shape[2], axis=2)
        G[n] = a

    dmod_all = everyone.reshape(N_DEV, -1)[:, :L * 6 * D].reshape(N_DEV, L, 6 * D)
    dmod_cols = jnp.transpose(lax.dynamic_slice_in_dim(dmod_all, q * NA, NA, axis=2), (1, 0, 2))
    G["w_ada"], d_ada, m_ada, v_ada = _ada_grad_adamw(cact.T, dmod_cols, w_ada, m_w_ada, v_w_ada)
    delta, new_m, new_v = {"w_ada": d_ada}, {"w_ada": m_ada}, {"w_ada": v_ada}

    for n in BIG:
        cols = W[n].shape[-1]
        outs = _adamw(*[a.reshape(-1, cols) for a in (W[n], G[n], M[n], V[n])])
        delta[n], new_m[n], new_v[n] = [o.reshape(W[n].shape) for o in outs]
    outs = _adamw(*[_pack([t[n] for n in SMALL]) for t in (W, G, M, V)])
    shapes = [W[n].shape for n in SMALL]
    for tgt, o in zip((delta, new_m, new_v), outs):
        for n, a in zip(SMALL, _unpack(o, shapes)):
            tgt[n] = a

    return (loss_total, dx[None], *[G[n] for n in WEIGHTS], *[delta[n] for n in WEIGHTS],
            *[new_m[n] for n in WEIGHTS], *[new_v[n] for n in WEIGHTS])
```

```python
import functools
import math

import jax
import jax.numpy as jnp
import numpy as np
from jax import lax
from jax.experimental import pallas as pl
from jax.experimental.pallas import tpu as pltpu

F32 = jnp.float32
BF16 = jnp.bfloat16
MESH = pl.DeviceIdType.MESH

EPS = 1e-6
HEAD_DIM = 128
POOL_WIDTH = 512
POOL_WINDOWS = (2, 4, 8, 16)
ATTN_WIDTH = 1024
N_HEADS = 8
LRU_WIDTH = 512
LRU_C = 8.0
N_IN = 4616
ZP, ZQ, ZK, ZV, ZX, ZY, ZF, ZW = 0, 512, 1536, 2560, 3584, 4096, 4608, 5120
FFN_CHUNK = 512
N_CHIPS = 4
N_DEV = 8

ADAM_LR, ADAM_B1, ADAM_B2, ADAM_EPS, ADAM_WD, ADAM_STEP = 0.001, 0.9, 0.999, 1e-08, 0.01, 10

TILES = dict(mm_m=512, mm_n=1024, mm_k=2048, row=512, attn=512, lru=256, cum=512, ew=256)
VMEM_LIMIT = 48 * 2**20


def _params(sem):
    return pltpu.CompilerParams(dimension_semantics=sem, vmem_limit_bytes=VMEM_LIMIT)


def _div_tile(n, pref, align):
    if n <= pref:
        return n
    t = (pref // align) * align
    while t >= align:
        if n % t == 0:
            return t
        t -= align
    raise ValueError(f"no tile for {n}")


def _sigmoid(x):
    return 1.0 / (1.0 + jnp.exp(-x))


def _gelu_parts(x):
    k = math.sqrt(2.0 / math.pi)
    u = k * (x + 0.044715 * x * x * x)
    t = jnp.tanh(u)
    gel = 0.5 * x * (1.0 + t)
    dgel = 0.5 * (1.0 + t) + 0.5 * x * (1.0 - t * t) * k * (1.0 + 3 * 0.044715 * x * x)
    return gel, dgel


def _neg_expm1(y):
    series = -y * (1.0 + y * (0.5 + y * (1.0 / 6 + y * (1.0 / 24 + y * (1.0 / 120)))))
    return jnp.where(y > -0.1, series, 1.0 - jnp.exp(y))


def _dot(a, b):
    return jnp.dot(a.astype(BF16), b.astype(BF16), preferred_element_type=F32)


def _dot_nt(a, b):
    return lax.dot_general(a.astype(BF16), b.astype(BF16), (((1,), (1,)), ((), ())), preferred_element_type=F32)


def _dot3(tri, v):
    hi = v.astype(BF16)
    r1 = v - hi.astype(F32)
    mid = r1.astype(BF16)
    lo = (r1 - mid.astype(F32)).astype(BF16)
    t = tri.astype(BF16)
    return (jnp.dot(t, hi, preferred_element_type=F32) + jnp.dot(t, mid, preferred_element_type=F32)
            + jnp.dot(t, lo, preferred_element_type=F32))


def _colsum(v):
    return jnp.sum(v, axis=0, keepdims=True)


def _rows(n, cols=128):
    return lax.broadcasted_iota(jnp.int32, (n, cols), 0)


def _matmul(a, b, *, nt=False, out_dtype=F32, res=None, gate=None, gu=None, name="matmul"):
    M, K = a.shape
    if gu == "b":
        N = b.shape[1] if nt else 2 * b.shape[2]
    else:
        N = b.shape[0] if nt else b.shape[1]
    tm = _div_tile(M, TILES["mm_m"], 8)
    tn = FFN_CHUNK if gu is not None and not nt else _div_tile(N, TILES["mm_n"], 128)
    tk = FFN_CHUNK if gu == "b" and nt else _div_tile(K, TILES["mm_k"], 128)
    nk = K // tk
    epi = res is not None

    def body(*refs):
        if epi:
            a_ref, b_ref, res_ref, gate_ref, o_ref, x_ref = refs[:6]
        else:
            a_ref, b_ref, o_ref = refs[:3]
        part = _dot_nt(a_ref[...], b_ref[...]) if nt else _dot(a_ref[...], b_ref[...])

        def finish(acc):
            o_ref[...] = acc.astype(o_ref.dtype)
            if epi:
                x_ref[...] = res_ref[...] + gate_ref[...] * acc

        if nk == 1:
            finish(part)
        else:
            acc_ref = refs[-1]
            k = pl.program_id(2)

            @pl.when(k == 0)
            def _():
                acc_ref[...] = part

            @pl.when(k > 0)
            def _():
                acc_ref[...] += part

            @pl.when(k == nk - 1)
            def _():
                finish(acc_ref[...])

    a_spec = pl.BlockSpec((tm, tk), lambda i, j, k: (i, k))
    b_spec = pl.BlockSpec((tn, tk), lambda i, j, k: (j, k)) if nt else pl.BlockSpec((tk, tn), lambda i, j, k: (k, j))
    o_spec = pl.BlockSpec((tm, tn), lambda i, j, k: (i, j))
    out_shape = jax.ShapeDtypeStruct((M, N), out_dtype)
    if gu == "b" and nt:
        b_spec = pl.BlockSpec((None, tn, tk), lambda i, j, k: (k % 2, j, k // 2))
    elif gu == "b":
        b_spec = pl.BlockSpec((None, tk, tn), lambda i, j, k: (j % 2, k, j // 2))
    elif gu == "out":
        o_spec = pl.BlockSpec((None, tm, tn), lambda i, j, k: (j % 2, i, j // 2))
        out_shape = jax.ShapeDtypeStruct((2, M, N // 2), out_dtype)
    in_specs, args = [a_spec, b_spec], [a, b]
    out_specs = o_spec
    if epi:
        in_specs += [o_spec, pl.BlockSpec((1, tn), lambda i, j, k: (0, j))]
        args += [res, gate]
        out_specs = (o_spec, o_spec)
        out_shape = (out_shape, jax.ShapeDtypeStruct((M, N), F32))
    return pl.pallas_call(
        body, name=name, grid=(M // tm, N // tn, nk), in_specs=in_specs, out_specs=out_specs, out_shape=out_shape,
        scratch_shapes=[pltpu.VMEM((tm, tn), F32)] if nk > 1 else [],
        compiler_params=_params(("parallel", "parallel", "arbitrary")),
    )(*args)


def _norm_mod(x, g, sc, sh):
    S, D = x.shape
    tr = _div_tile(S, TILES["row"], 8)

    def body(x_ref, g_ref, sc_ref, sh_ref, h_ref):
        xf = x_ref[...]
        r = lax.rsqrt(jnp.mean(xf * xf, axis=-1, keepdims=True) + EPS)
        h_ref[...] = (((xf * r) * g_ref[...]) * (1.0 + sc_ref[...]) + sh_ref[...]).astype(h_ref.dtype)

    row = pl.BlockSpec((tr, D), lambda i: (i, 0))
    vec = pl.BlockSpec((1, D), lambda i: (0, 0))
    return pl.pallas_call(
        body, name="norm_mod", grid=(S // tr,), in_specs=[row, vec, vec, vec], out_specs=row,
        out_shape=jax.ShapeDtypeStruct((S, D), BF16), compiler_params=_params(("parallel",)),
    )(x, g, sc, sh)


def _norm_mod_bwd(x, dh, dres, g, sc):
    S, D = x.shape
    tr = _div_tile(S, TILES["ew"], 8)

    def body(x_ref, dh_ref, dres_ref, g_ref, sc_ref, dx_ref, dg_ref, dsc_ref, dsh_ref):
        xf, dh_ = x_ref[...], dh_ref[...]
        r = lax.rsqrt(jnp.mean(xf * xf, axis=-1, keepdims=True) + EPS)
        xhat = xf * r
        dxhat = dh_ * (g_ref[...] * (1.0 + sc_ref[...]))
        dx_ref[...] = dres_ref[...] + r * (dxhat - xhat * jnp.mean(dxhat * xhat, axis=-1, keepdims=True))
        t = _colsum(dh_ * xhat)

        @pl.when(pl.program_id(0) == 0)
        def _():
            dg_ref[...] = jnp.zeros_like(dg_ref)
            dsc_ref[...] = jnp.zeros_like(dsc_ref)
            dsh_ref[...] = jnp.zeros_like(dsh_ref)

        dg_ref[...] += t * (1.0 + sc_ref[...])
        dsc_ref[...] += t * g_ref[...]
        dsh_ref[...] += _colsum(dh_)

    row = pl.BlockSpec((tr, D), lambda i: (i, 0))
    vec = pl.BlockSpec((1, D), lambda i: (0, 0))
    vshape = jax.ShapeDtypeStruct((1, D), F32)
    return pl.pallas_call(
        body, name="norm_mod_bwd", grid=(S // tr,), in_specs=[row, row, row, vec, vec], out_specs=(row, vec, vec, vec),
        out_shape=(jax.ShapeDtypeStruct((S, D), F32), vshape, vshape, vshape), compiler_params=_params(("arbitrary",)),
    )(x, dh, dres, g, sc)


def _gate_bwd(dx, m, gt):
    S, D = dx.shape
    tr = _div_tile(S, TILES["row"], 8)

    def body(dx_ref, m_ref, gt_ref, dm_ref, dgt_ref):
        d = dx_ref[...]
        dm_ref[...] = (d * gt_ref[...]).astype(dm_ref.dtype)

        @pl.when(pl.program_id(0) == 0)
        def _():
            dgt_ref[...] = jnp.zeros_like(dgt_ref)

        dgt_ref[...] += _colsum(d * m_ref[...])

    row = pl.BlockSpec((tr, D), lambda i: (i, 0))
    vec = pl.BlockSpec((1, D), lambda i: (0, 0))
    return pl.pallas_call(
        body, name="gate_bwd", grid=(S // tr,), in_specs=[row, row, vec], out_specs=(row, vec),
        out_shape=(jax.ShapeDtypeStruct((S, D), BF16), jax.ShapeDtypeStruct((1, D), F32)),
        compiler_params=_params(("arbitrary",)),
    )(dx, m, gt)


def _loss_head(x, target, g):
    S, D = x.shape
    tr = _div_tile(S, TILES["ew"], 8)

    def body(x_ref, t_ref, g_ref, dx_ref, dg_ref, loss_ref):
        xf = x_ref[...]
        r = lax.rsqrt(jnp.mean(xf * xf, axis=-1, keepdims=True) + EPS)
        xhat = xf * r
        err = xhat * g_ref[...] - t_ref[...]
        dy = err * (1.0 / D)
        dxhat = dy * g_ref[...]
        dx_ref[...] = r * (dxhat - xhat * jnp.mean(dxhat * xhat, axis=-1, keepdims=True))

        @pl.when(pl.program_id(0) == 0)
        def _():
            dg_ref[...] = jnp.zeros_like(dg_ref)
            loss_ref[...] = jnp.zeros_like(loss_ref)

        dg_ref[...] += _colsum(dy * xhat)
        loss_ref[...] += 0.5 * jnp.sum(jnp.mean(err * err, axis=-1, keepdims=True))

    row = pl.BlockSpec((tr, D), lambda i: (i, 0))
    vec = pl.BlockSpec((1, D), lambda i: (0, 0))
    one = pl.BlockSpec((1, 128), lambda i: (0, 0))
    return pl.pallas_call(
        body, name="loss_head", grid=(S // tr,), in_specs=[row, row, vec], out_specs=(row, vec, one),
        out_shape=(jax.ShapeDtypeStruct((S, D), F32), jax.ShapeDtypeStruct((1, D), F32),
                   jax.ShapeDtypeStruct((1, 128), F32)),
        compiler_params=_params(("arbitrary",)),
    )(x, target, g)


POOL_HALO = 16


def _pool_delta(ext, u, first_pos, tr):
    pos = (first_pos + _rows(tr) + 1).astype(F32)
    outs = []
    for gi, win in enumerate(POOL_WINDOWS):
        s = ext[:, gi * 128:(gi + 1) * 128]
        d = 1
        while d < win:
            s = s + pltpu.roll(s, d, 0)
            d *= 2
        outs.append(s[POOL_HALO:] / jnp.minimum(pos, float(win)) - u[:, gi * 128:(gi + 1) * 128])
    return outs


def _pool_fwd(z, w, scale):
    S = z.shape[0]
    tr = _div_tile(S, TILES["row"], POOL_HALO)
    hb = tr // POOL_HALO

    def body(z_ref, halo_ref, w_ref, sc_ref, y_ref):
        i = pl.program_id(0)
        u = z_ref[...]
        halo = jnp.where(i > 0, halo_ref[...], 0.0)
        ds_ = _pool_delta(jnp.concatenate([halo, u], axis=0), u, i * tr, tr)
        for gi in range(4):
            y_ref[:, gi * 128:(gi + 1) * 128] = _dot(ds_[gi], w_ref[gi]) * sc_ref[:, gi * 128:(gi + 1) * 128]

    return pl.pallas_call(
        body, name="pool_fwd", grid=(S // tr,),
        in_specs=[pl.BlockSpec((tr, POOL_WIDTH), lambda i: (i, 0)),
                  pl.BlockSpec((POOL_HALO, POOL_WIDTH), lambda i: (jnp.maximum(i * hb - 1, 0), 0)),
                  pl.BlockSpec((4, 128, 128), lambda i: (0, 0, 0)), pl.BlockSpec((1, POOL_WIDTH), lambda i: (0, 0))],
        out_specs=pl.BlockSpec((tr, POOL_WIDTH), lambda i: (i, 0)),
        out_shape=jax.ShapeDtypeStruct((S, POOL_WIDTH), F32), compiler_params=_params(("parallel",)),
    )(z, z, w, scale)


def _pool_bwd(z, dy, w, scale):
    S = z.shape[0]
    tr = _div_tile(S, TILES["row"], POOL_HALO)
    hb = tr // POOL_HALO
    nt = S // tr

    def body(z_ref, halo_ref, dy_ref, dyn_ref, w_ref, sc_ref, dz_ref, dw_ref, dsc_ref):
        i = pl.program_id(0)
        u = z_ref[...]
        halo = jnp.where(i > 0, halo_ref[...], 0.0)
        ds_ = _pool_delta(jnp.concatenate([halo, u], axis=0), u, i * tr, tr)
        dy_ext = jnp.concatenate([dy_ref[...], jnp.where(i < nt - 1, dyn_ref[...], 0.0)], axis=0)
        pos = (i * tr + _rows(tr + POOL_HALO) + 1).astype(F32)

        @pl.when(i == 0)
        def _():
            dw_ref[...] = jnp.zeros_like(dw_ref)
            dsc_ref[...] = jnp.zeros_like(dsc_ref)

        for gi, win in enumerate(POOL_WINDOWS):
            cols = slice(gi * 128, (gi + 1) * 128)
            dyg = dy_ext[:, cols]
            dys = dyg * sc_ref[:, cols]
            dsc_ref[:, cols] += _colsum(dyg[:tr] * _dot(ds_[gi], w_ref[gi]))
            dw_ref[gi] += _dot(ds_[gi].T, dys[:tr])
            dd = _dot_nt(dys, w_ref[gi])
            e = dd / jnp.minimum(pos, float(win))
            d = 1
            while d < win:
                e = e + pltpu.roll(e, tr + POOL_HALO - d, 0)
                d *= 2
            dz_ref[:, cols] = (e[:tr] - dd[:tr]).astype(dz_ref.dtype)

    return pl.pallas_call(
        body, name="pool_bwd", grid=(nt,),
        in_specs=[pl.BlockSpec((tr, POOL_WIDTH), lambda i: (i, 0)),
                  pl.BlockSpec((POOL_HALO, POOL_WIDTH), lambda i: (jnp.maximum(i * hb - 1, 0), 0)),
                  pl.BlockSpec((tr, POOL_WIDTH), lambda i: (i, 0)),
                  pl.BlockSpec((POOL_HALO, POOL_WIDTH), lambda i: (jnp.minimum((i + 1) * hb, nt * hb - 1), 0)),
                  pl.BlockSpec((4, 128, 128), lambda i: (0, 0, 0)), pl.BlockSpec((1, POOL_WIDTH), lambda i: (0, 0))],
        out_specs=(pl.BlockSpec((tr, POOL_WIDTH), lambda i: (i, 0)), pl.BlockSpec((4, 128, 128), lambda i: (0, 0, 0)),
                   pl.BlockSpec((1, POOL_WIDTH), lambda i: (0, 0))),
        out_shape=(jax.ShapeDtypeStruct((S, POOL_WIDTH), BF16), jax.ShapeDtypeStruct((4, 128, 128), F32),
                   jax.ShapeDtypeStruct((1, POOL_WIDTH), F32)),
        compiler_params=_params(("arbitrary",)),
    )(z, z, dy, dy, w, scale)


def _log_sigmoid(x):
    return jnp.minimum(x, 0.0) - jnp.log(1.0 + jnp.exp(-jnp.abs(x)))


def _forget_cumsum(z, b_f):
    S = z.shape[0]
    tr = _div_tile(S, TILES["cum"], 8)
    zf_block = ZF // 128

    def body(z_ref, b_ref, f_ref, carry):
        @pl.when(pl.program_id(0) == 0)
        def _():
            carry[...] = jnp.zeros_like(carry)

        lf = _log_sigmoid(z_ref[...] + b_ref[...])
        tri = lax.broadcasted_iota(jnp.int32, (tr, tr), 1) <= lax.broadcasted_iota(jnp.int32, (tr, tr), 0)
        f_ref[...] = _dot3(tri, lf) + carry[...]
        carry[...] += _colsum(lf)

    return pl.pallas_call(
        body, name="forget_cumsum", grid=(S // tr,),
        in_specs=[pl.BlockSpec((tr, 128), lambda i: (i, zf_block)), pl.BlockSpec((1, 128), lambda i: (0, 0))],
        out_specs=pl.BlockSpec((tr, 128), lambda i: (i, 0)), out_shape=jax.ShapeDtypeStruct((S, 128), F32),
        scratch_shapes=[pltpu.VMEM((1, 128), F32)], compiler_params=_params(("arbitrary",)),
    )(z, b_f)


def _forget_cumsum_bwd(z, b_f, dF):
    S = z.shape[0]
    tr = _div_tile(S, TILES["cum"], 8)
    nt = S // tr
    zf_block = ZF // 128

    def body(z_ref, b_ref, df_ref, dz_ref, db_ref, carry):
        @pl.when(pl.program_id(0) == 0)
        def _():
            carry[...] = jnp.zeros_like(carry)
            db_ref[...] = jnp.zeros_like(db_ref)

        dF_ = df_ref[...]
        tri = lax.broadcasted_iota(jnp.int32, (tr, tr), 1) >= lax.broadcasted_iota(jnp.int32, (tr, tr), 0)
        dlf = _dot3(tri, dF_) + carry[...]
        carry[...] += _colsum(dF_)
        lane = lax.broadcasted_iota(jnp.int32, (tr, 128), 1)
        dzf = jnp.where(lane < N_HEADS, dlf * _sigmoid(-(z_ref[...] + b_ref[...])), 0.0)
        dz_ref[...] = dzf.astype(dz_ref.dtype)
        db_ref[...] += _colsum(dzf)

    return pl.pallas_call(
        body, name="forget_cumsum_bwd", grid=(nt,),
        in_specs=[pl.BlockSpec((tr, 128), lambda i: (nt - 1 - i, zf_block)), pl.BlockSpec((1, 128), lambda i: (0, 0)),
                  pl.BlockSpec((tr, 128), lambda i: (nt - 1 - i, 0))],
        out_specs=(pl.BlockSpec((tr, 128), lambda i: (nt - 1 - i, 0)), pl.BlockSpec((1, 128), lambda i: (0, 0))),
        out_shape=(jax.ShapeDtypeStruct((S, 128), BF16), jax.ShapeDtypeStruct((1, 128), F32)),
        scratch_shapes=[pltpu.VMEM((1, 128), F32)], compiler_params=_params(("arbitrary",)),
    )(z, b_f, dF)


NEG = -1e30
ATTN_SCALE = HEAD_DIM ** -0.5


def _attn_scores(q, k, f_row, f_col, diagonal):
    s = _dot_nt(q, k) * ATTN_SCALE + f_row - f_col
    if not diagonal:
        return s
    t = s.shape[0]
    return jnp.where(lax.broadcasted_iota(jnp.int32, (t, t), 1) <= lax.broadcasted_iota(jnp.int32, (t, t), 0), s, NEG)


def _on_block_kind(qi, kj, fn):
    @pl.when(qi == kj)
    def _():
        fn(True)

    @pl.when(qi != kj)
    def _():
        fn(False)


FIRST, LAST, SECOND_PASS = 1, 2, 4


def _tri_schedule(n, by_key=False, passes=1):
    outer, inner, flags = [], [], []
    for a in range(n):
        partners = list(range(a, n)) if by_key else list(range(a + 1))
        for ps in range(passes):
            for idx, b in enumerate(partners):
                f = FIRST if ps == 0 and idx == 0 else 0
                f |= LAST if ps == passes - 1 and idx == len(partners) - 1 else 0
                f |= SECOND_PASS if ps == 1 else 0
                outer.append(a)
                inner.append(b)
                flags.append(f)
    return [jnp.asarray(np.array(v, np.int32)) for v in (outer, inner, flags)]


def _flash_call(body, name, sched, in_specs, out_specs, out_shape, scratch):
    grid_spec = pltpu.PrefetchScalarGridSpec(
        num_scalar_prefetch=3, grid=(N_HEADS, int(sched[0].shape[0])), in_specs=in_specs, out_specs=out_specs,
        scratch_shapes=scratch)
    return pl.pallas_call(body, name=name, grid_spec=grid_spec, out_shape=out_shape,
                          compiler_params=_params(("parallel", "arbitrary")))


def _flash_fwd(z, f_col, f_row):
    S = z.shape[0]
    t = _div_tile(S, TILES["attn"], 128)
    n = S // t
    qb, kb, vb = ZQ // 128, ZK // 128, ZV // 128

    sched = _tri_schedule(n)

    def body(qt, kt, ft, q_ref, k_ref, v_ref, fq_ref, fk_ref, o_ref, lse_ref, m_sc, l_sc, acc_sc):
        step = pl.program_id(1)
        qi, kj, fl = qt[step], kt[step], ft[step]

        @pl.when((fl & FIRST) != 0)
        def _():
            m_sc[...] = jnp.full_like(m_sc, NEG)
            l_sc[...] = jnp.zeros_like(l_sc)
            acc_sc[...] = jnp.zeros_like(acc_sc)

        def update(diagonal):
            s = _attn_scores(q_ref[...], k_ref[...], fq_ref[...], fk_ref[...], diagonal)
            m_new = jnp.maximum(m_sc[...], jnp.max(s, axis=1, keepdims=True))
            alpha = jnp.exp(m_sc[...] - m_new)
            p = jnp.exp(s - m_new)
            l_sc[...] = alpha * l_sc[...] + jnp.sum(p, axis=1, keepdims=True)
            acc_sc[...] = alpha * acc_sc[...] + _dot(p, v_ref[...])
            m_sc[...] = m_new

        _on_block_kind(qi, kj, update)

        @pl.when((fl & LAST) != 0)
        def _():
            o_ref[...] = acc_sc[...] / l_sc[...]
            lse_ref[...] = m_sc[...] + jnp.log(l_sc[...])

    def kv(block):
        return pl.BlockSpec((t, 128), lambda h, s, qt, kt, ft: (kt[s], block + h))

    col = pl.BlockSpec((None, t, 1), lambda h, s, qt, kt, ft: (h, qt[s], 0))
    return _flash_call(
        body, "flash_fwd", sched,
        [pl.BlockSpec((t, 128), lambda h, s, qt, kt, ft: (qt[s], qb + h)), kv(kb), kv(vb), col,
         pl.BlockSpec((None, 1, t), lambda h, s, qt, kt, ft: (h, 0, kt[s]))],
        (pl.BlockSpec((t, 128), lambda h, s, qt, kt, ft: (qt[s], h)), col),
        (jax.ShapeDtypeStruct((S, ATTN_WIDTH), F32), jax.ShapeDtypeStruct((N_HEADS, S, 1), F32)),
        [pltpu.VMEM((t, 1), F32), pltpu.VMEM((t, 1), F32), pltpu.VMEM((t, 128), F32)],
    )(*sched, z, z, z, f_col, f_row)


def _flash_bwd_dq(z, dy, lse, f_col, f_row):
    S = z.shape[0]
    t = _div_tile(S, TILES["attn"], 128)
    sched = _tri_schedule(S // t, passes=2)
    qb, kb, vb = ZQ // 128, ZK // 128, ZV // 128
    dob = POOL_WIDTH // 128

    def body(qt, kt, ft, q_ref, k_ref, v_ref, do_ref, lse_ref, fq_ref, fk_ref, dq_ref, delta_ref, dfq_ref, acc_sc):
        step = pl.program_id(1)
        qi, kj, fl = qt[step], kt[step], ft[step]

        @pl.when((fl & FIRST) != 0)
        def _():
            delta_ref[...] = jnp.zeros_like(delta_ref)
            dfq_ref[...] = jnp.zeros_like(dfq_ref)
            acc_sc[...] = jnp.zeros_like(acc_sc)

        def update(diagonal):
            s = _attn_scores(q_ref[...], k_ref[...], fq_ref[...], fk_ref[...], diagonal)
            p = jnp.exp(s - lse_ref[...])
            dp = _dot_nt(do_ref[...], v_ref[...])

            @pl.when((fl & SECOND_PASS) == 0)
            def _():
                delta_ref[...] += jnp.sum(p * dp, axis=1, keepdims=True)

            @pl.when((fl & SECOND_PASS) != 0)
            def _():
                ds = p * (dp - delta_ref[...])
                dfq_ref[...] += jnp.sum(ds, axis=1, keepdims=True)
                acc_sc[...] += _dot(ds, k_ref[...])

        _on_block_kind(qi, kj, update)

        @pl.when((fl & LAST) != 0)
        def _():
            dq_ref[...] = (acc_sc[...] * ATTN_SCALE).astype(dq_ref.dtype)

    def kv(block):
        return pl.BlockSpec((t, 128), lambda h, s, qt, kt, ft: (kt[s], block + h))

    def qs(block):
        return pl.BlockSpec((t, 128), lambda h, s, qt, kt, ft: (qt[s], block + h))

    col = pl.BlockSpec((None, t, 1), lambda h, s, qt, kt, ft: (h, qt[s], 0))
    return _flash_call(
        body, "flash_bwd_dq", sched,
        [qs(qb), kv(kb), kv(vb), qs(dob), col, col, pl.BlockSpec((None, 1, t), lambda h, s, qt, kt, ft: (h, 0, kt[s]))],
        (pl.BlockSpec((t, 128), lambda h, s, qt, kt, ft: (qt[s], h)), col, col),
        (jax.ShapeDtypeStruct((S, ATTN_WIDTH), BF16), jax.ShapeDtypeStruct((N_HEADS, S, 1), F32),
         jax.ShapeDtypeStruct((N_HEADS, S, 1), F32)),
        [pltpu.VMEM((t, 128), F32)],
    )(*sched, z, z, z, dy, lse, f_col, f_row)


def _flash_bwd_dkv(z, dy, lse_row, delta_row, f_col, f_row):
    S = z.shape[0]
    t = _div_tile(S, TILES["attn"], 128)
    sched = _tri_schedule(S // t, by_key=True)
    qb, kb, vb = ZQ // 128, ZK // 128, ZV // 128
    dob = POOL_WIDTH // 128

    def body(kt, qt, ft, q_ref, k_ref, v_ref, do_ref, lse_ref, delta_ref, fq_ref, fk_ref, dk_ref, dv_ref, df_ref,
             dk_sc, dv_sc, df_sc):
        step = pl.program_id(1)
        kj, qi, fl = kt[step], qt[step], ft[step]

        @pl.when((fl & FIRST) != 0)
        def _():
            dk_sc[...] = jnp.zeros_like(dk_sc)
            dv_sc[...] = jnp.zeros_like(dv_sc)
            df_sc[...] = jnp.zeros_like(df_sc)

        def update(diagonal):
            pt = jnp.exp(_dot_nt(k_ref[...], q_ref[...]) * ATTN_SCALE + fq_ref[...] - fk_ref[...] - lse_ref[...])
            if diagonal:
                krow = lax.broadcasted_iota(jnp.int32, (t, t), 0)
                qcol = lax.broadcasted_iota(jnp.int32, (t, t), 1)
                pt = jnp.where(krow <= qcol, pt, 0.0)
            dv_sc[...] += _dot(pt, do_ref[...])
            dst = pt * (_dot_nt(v_ref[...], do_ref[...]) - delta_ref[...])
            dk_sc[...] += _dot(dst, q_ref[...])
            df_sc[...] += jnp.sum(dst, axis=1, keepdims=True)

        _on_block_kind(qi, kj, update)

        @pl.when((fl & LAST) != 0)
        def _():
            dk_ref[...] = (dk_sc[...] * ATTN_SCALE).astype(dk_ref.dtype)
            dv_ref[...] = dv_sc[...].astype(dv_ref.dtype)
            df_ref[...] = -df_sc[...]

    def qs(block):
        return pl.BlockSpec((t, 128), lambda h, s, kt, qt, ft: (qt[s], block + h))

    def kv(block):
        return pl.BlockSpec((t, 128), lambda h, s, kt, qt, ft: (kt[s], block + h))

    qrow = pl.BlockSpec((None, 1, t), lambda h, s, kt, qt, ft: (h, 0, qt[s]))
    kcol = pl.BlockSpec((None, t, 1), lambda h, s, kt, qt, ft: (h, kt[s], 0))
    out = pl.BlockSpec((t, 128), lambda h, s, kt, qt, ft: (kt[s], h))
    return _flash_call(
        body, "flash_bwd_dkv", sched, [qs(qb), kv(kb), kv(vb), qs(dob), qrow, qrow, qrow, kcol], (out, out, kcol),
        (jax.ShapeDtypeStruct((S, ATTN_WIDTH), BF16), jax.ShapeDtypeStruct((S, ATTN_WIDTH), BF16),
         jax.ShapeDtypeStruct((N_HEADS, S, 1), F32)),
        [pltpu.VMEM((t, 128), F32), pltpu.VMEM((t, 128), F32), pltpu.VMEM((t, 1), F32)],
    )(*sched, z, z, z, dy, lse_row, delta_row, f_row, f_col)


LRU_HALO = 8


def _lru_gates(ext, cw_ref, cb_ref, wa_ref, ba_ref, wi_ref, bi_ref, lam_ref, tr):
    taps = [pltpu.roll(ext, 3 - k, 0)[LRU_HALO:] if k < 3 else ext[LRU_HALO:] for k in range(4)]
    xc = cb_ref[...] + taps[0] * cw_ref[0:1, :]
    for k in range(1, 4):
        xc = xc + taps[k] * cw_ref[k:k + 1, :]
    ga = jnp.concatenate([_dot(xc[:, g * 128:(g + 1) * 128], wa_ref[g]) for g in range(4)], axis=1) + ba_ref[...]
    gi = jnp.concatenate([_dot(xc[:, g * 128:(g + 1) * 128], wi_ref[g]) for g in range(4)], axis=1) + bi_ref[...]
    r, ig = _sigmoid(ga), _sigmoid(gi)
    nl = -lam_ref[...]
    sp = jnp.maximum(nl, 0.0) + jnp.log(1.0 + jnp.exp(-jnp.abs(nl)))
    la = -LRU_C * r * sp
    a = jnp.exp(la)
    mult = jnp.sqrt(_neg_expm1(2.0 * la))
    return xc, r, ig, sp, a, mult, taps


def _lru_specs(tr, nt, rev):
    hb = tr // LRU_HALO
    ti = (lambda i: nt - 1 - i) if rev else (lambda i: i)
    zx_b, zy_b = ZX // LRU_WIDTH, ZY // LRU_WIDTH
    cur = lambda b: pl.BlockSpec((tr, LRU_WIDTH), lambda i: (ti(i), b))
    prev = lambda b: pl.BlockSpec((LRU_HALO, LRU_WIDTH), lambda i: (jnp.maximum(ti(i) * hb - 1, 0), b))
    vec = pl.BlockSpec((1, LRU_WIDTH), lambda i: (0, 0))
    cw = pl.BlockSpec((4, LRU_WIDTH), lambda i: (0, 0))
    blk = pl.BlockSpec((4, 128, 128), lambda i: (0, 0, 0))
    return ti, cur, prev, vec, cw, blk, zx_b, zy_b


def _lru_fwd(z, cw, cb, wa, ba, wi, bi, lam):
    S = z.shape[0]
    tr = _div_tile(S, TILES["lru"], 8)
    nt = S // tr
    ti, cur, prev, vec, cwspec, blk, zx_b, zy_b = _lru_specs(tr, nt, False)

    def body(zx_ref, halo_ref, zy_ref, cw_ref, cb_ref, wa_ref, ba_ref, wi_ref, bi_ref, lam_ref, y_ref, h_ref, carry):
        i = pl.program_id(0)

        @pl.when(i == 0)
        def _():
            carry[...] = jnp.zeros_like(carry)

        ext = jnp.concatenate([jnp.where(i > 0, halo_ref[...], 0.0), zx_ref[...]], axis=0)
        xc, r, ig, sp, a, mult, _ = _lru_gates(ext, cw_ref, cb_ref, wa_ref, ba_ref, wi_ref, bi_ref, lam_ref, tr)
        A, B = a, mult * (ig * xc)
        row = _rows(tr, LRU_WIDTH)
        d = 1
        while d < tr:
            a_sh = jnp.where(row >= d, pltpu.roll(A, d, 0), 1.0)
            b_sh = jnp.where(row >= d, pltpu.roll(B, d, 0), 0.0)
            B = A * b_sh + B
            A = A * a_sh
            d *= 2
        h = B + A * carry[...]
        h_ref[...] = h
        carry[...] = h_ref[pl.ds(tr - 1, 1), :]
        y_ref[...] = h * _gelu_parts(zy_ref[...])[0]

    out = pl.BlockSpec((tr, LRU_WIDTH), lambda i: (i, 0))
    shape = jax.ShapeDtypeStruct((S, LRU_WIDTH), F32)
    return pl.pallas_call(
        body, name="lru_fwd", grid=(nt,),
        in_specs=[cur(zx_b), prev(zx_b), cur(zy_b), cwspec, vec, blk, vec, blk, vec, vec],
        out_specs=(out, out), out_shape=(shape, shape), scratch_shapes=[pltpu.VMEM((1, LRU_WIDTH), F32)],
        compiler_params=_params(("arbitrary",)),
    )(z, z, z, cw, cb, wa, ba, wi, bi, lam)


def _lru_bwd(z, dy, hs, cw, cb, wa, ba, wi, bi, lam):
    S = z.shape[0]
    tr = _div_tile(S, TILES["lru"], 8)
    nt = S // tr
    ti, cur, prev, vec, cwspec, blk, zx_b, zy_b = _lru_specs(tr, nt, True)
    dy_b = (POOL_WIDTH + ATTN_WIDTH) // LRU_WIDTH

    def body(zx_ref, halo_ref, zy_ref, dy_ref, h_ref, hprev_ref, cw_ref, cb_ref, wa_ref, ba_ref, wi_ref, bi_ref, lam_ref,
             dzx_ref, dzy_ref, dcw_ref, dcb_ref, dwa_ref, dba_ref, dwi_ref, dbi_ref, dlam_ref, gcarry, dxc_next, tmp):
        i = pl.program_id(0)
        t_idx = nt - 1 - i

        @pl.when(i == 0)
        def _():
            gcarry[...] = jnp.zeros_like(gcarry)
            dxc_next[...] = jnp.zeros_like(dxc_next)
            for ref in (dcw_ref, dcb_ref, dwa_ref, dba_ref, dwi_ref, dbi_ref, dlam_ref):
                ref[...] = jnp.zeros_like(ref)

        ext = jnp.concatenate([jnp.where(t_idx > 0, halo_ref[...], 0.0), zx_ref[...]], axis=0)
        xc, r, ig, sp, a, mult, taps = _lru_gates(ext, cw_ref, cb_ref, wa_ref, ba_ref, wi_ref, bi_ref, lam_ref, tr)
        h = h_ref[...]
        gel, dgel = _gelu_parts(zy_ref[...])
        dy_ = dy_ref[...]
        dzy_ref[...] = (dy_ * h * dgel).astype(dzy_ref.dtype)
        row = _rows(tr, LRU_WIDTH)
        B = dy_ * gel + jnp.where(row == tr - 1, gcarry[...], 0.0)
        A = jnp.where(row < tr - 1, pltpu.roll(a, tr - 1, 0), 0.0)
        d = 1
        while d < tr:
            keep = row < tr - d
            b_sh = jnp.where(keep, pltpu.roll(B, tr - d, 0), 0.0)
            a_sh = jnp.where(keep, pltpu.roll(A, tr - d, 0), 0.0)
            B = B + A * b_sh
            A = A * a_sh
            d *= 2
        g = B
        tmp[...] = a * g
        gcarry[...] = tmp[pl.ds(0, 1), :]
        h_ext = jnp.concatenate([jnp.where(t_idx > 0, hprev_ref[...], 0.0), h], axis=0)
        hprev = pltpu.roll(h_ext, 1, 0)[LRU_HALO:]
        t1 = g * mult
        dig = t1 * xc
        dxc = t1 * ig
        dla = (g * hprev) * a - (g * (ig * xc)) * (a * a) / mult
        dr = dla * (-LRU_C * sp)
        dga = dr * r * (1.0 - r)
        dgi = dig * ig * (1.0 - ig)
        dlam_ref[...] += _colsum(dla * (-LRU_C * r)) * (-_sigmoid(-lam_ref[...]))
        dba_ref[...] += _colsum(dga)
        dbi_ref[...] += _colsum(dgi)
        parts = []
        for gidx in range(4):
            cols = slice(gidx * 128, (gidx + 1) * 128)
            xct = xc[:, cols].T
            dwa_ref[gidx] += _dot(xct, dga[:, cols])
            dwi_ref[gidx] += _dot(xct, dgi[:, cols])
            parts.append(_dot_nt(dga[:, cols], wa_ref[gidx]) + _dot_nt(dgi[:, cols], wi_ref[gidx]))
        dxc = dxc + jnp.concatenate(parts, axis=1)
        dcb_ref[...] += _colsum(dxc)
        for k in range(4):
            dcw_ref[k:k + 1, :] += _colsum(dxc * taps[k])
        ext_d = jnp.concatenate([dxc, dxc_next[...]], axis=0)
        dzx = dxc * cw_ref[3:4, :]
        for k in range(3):
            dzx = dzx + pltpu.roll(ext_d, tr + LRU_HALO - (3 - k), 0)[:tr] * cw_ref[k:k + 1, :]
        dzx_ref[...] = dzx.astype(dzx_ref.dtype)
        dxc_next[...] = dxc[:LRU_HALO]

    rev = pl.BlockSpec((tr, LRU_WIDTH), lambda i: (nt - 1 - i, 0))
    hb = tr // LRU_HALO
    hprev_spec = pl.BlockSpec((LRU_HALO, LRU_WIDTH), lambda i: (jnp.maximum((nt - 1 - i) * hb - 1, 0), 0))
    dy_spec = pl.BlockSpec((tr, LRU_WIDTH), lambda i: (nt - 1 - i, dy_b))
    vshape = jax.ShapeDtypeStruct((1, LRU_WIDTH), F32)
    bshape = jax.ShapeDtypeStruct((4, 128, 128), F32)
    return pl.pallas_call(
        body, name="lru_bwd", grid=(nt,),
        in_specs=[cur(zx_b), prev(zx_b), cur(zy_b), dy_spec, rev, hprev_spec, cwspec, vec, blk, vec, blk, vec, vec],
        out_specs=(rev, rev, cwspec, vec, blk, vec, blk, vec, vec),
        out_shape=(jax.ShapeDtypeStruct((S, LRU_WIDTH), BF16), jax.ShapeDtypeStruct((S, LRU_WIDTH), BF16),
                   jax.ShapeDtypeStruct((4, LRU_WIDTH), F32), vshape, bshape, vshape, bshape, vshape, vshape),
        scratch_shapes=[pltpu.VMEM((1, LRU_WIDTH), F32), pltpu.VMEM((LRU_HALO, LRU_WIDTH), F32),
                        pltpu.VMEM((tr, LRU_WIDTH), F32)],
        compiler_params=_params(("arbitrary",)),
    )(z, z, z, dy, hs, hs, cw, cb, wa, ba, wi, bi, lam)


FFN_HALO = 8


def _ffn_act(au, cw, cb):
    S, F2 = au.shape
    F = F2 // 2
    tc = FFN_CHUNK
    tr = _div_tile(S, TILES["row"], 8)
    hb = tr // FFN_HALO

    def body(au_ref, halo_ref, cw_ref, cb_ref, p_ref):
        i = pl.program_id(0)
        a_ = au_ref[:, :tc]
        ext = jnp.concatenate([jnp.where(i > 0, halo_ref[:, :tc], 0.0), a_], axis=0)
        gc = cb_ref[...] + a_ * cw_ref[2:3, :]
        for k in range(2):
            gc = gc + pltpu.roll(ext, 2 - k, 0)[FFN_HALO:] * cw_ref[k:k + 1, :]
        p_ref[...] = (gc * _sigmoid(gc) * au_ref[:, tc:]).astype(p_ref.dtype)

    return pl.pallas_call(
        body, name="ffn_act", grid=(S // tr, F // tc),
        in_specs=[pl.BlockSpec((tr, 2 * tc), lambda i, j: (i, j)),
                  pl.BlockSpec((FFN_HALO, 2 * tc), lambda i, j: (jnp.maximum(i * hb - 1, 0), j)),
                  pl.BlockSpec((3, tc), lambda i, j: (0, j)), pl.BlockSpec((1, tc), lambda i, j: (0, j))],
        out_specs=pl.BlockSpec((tr, tc), lambda i, j: (i, j)), out_shape=jax.ShapeDtypeStruct((S, F), BF16),
        compiler_params=_params(("parallel", "parallel")),
    )(au, au, cw, cb)


def _ffn_act_bwd(au, dp, cw, cb):
    S, F2 = au.shape
    F = F2 // 2
    tc = FFN_CHUNK
    tr = _div_tile(S, TILES["ew"], 8)
    hb = tr // FFN_HALO
    nt = S // tr
    H = FFN_HALO

    def body(au_ref, prev_ref, next_ref, dp_ref, dpn_ref, cw_ref, cb_ref, dau_ref, dcw_ref, dcb_ref):
        i = pl.program_id(1)
        last = i == nt - 1
        a_ext = jnp.concatenate([jnp.where(i > 0, prev_ref[:, :tc], 0.0), au_ref[:, :tc], next_ref[:, :tc]], axis=0)
        u_ext = jnp.concatenate([au_ref[:, tc:], next_ref[:, tc:]], axis=0)
        dp_ext = jnp.concatenate([dp_ref[...], jnp.where(last, 0.0, dpn_ref[...])], axis=0)
        taps = [pltpu.roll(a_ext, 2 - k, 0)[H:] if k < 2 else a_ext[H:] for k in range(3)]
        gc = cb_ref[...] + taps[0] * cw_ref[0:1, :] + taps[1] * cw_ref[1:2, :] + taps[2] * cw_ref[2:3, :]
        sig = _sigmoid(gc)
        dgc = dp_ext * u_ext * (sig * (1.0 + gc * (1.0 - sig)))
        da = dgc[:tr] * cw_ref[2:3, :]
        for k in range(2):
            da = da + pltpu.roll(dgc, tr + H - (2 - k), 0)[:tr] * cw_ref[k:k + 1, :]
        dau_ref[:, :tc] = da.astype(dau_ref.dtype)
        dau_ref[:, tc:] = (dp_ref[...] * (gc[:tr] * sig[:tr])).astype(dau_ref.dtype)

        @pl.when(i == 0)
        def _():
            dcw_ref[...] = jnp.zeros_like(dcw_ref)
            dcb_ref[...] = jnp.zeros_like(dcb_ref)

        dcb_ref[...] += _colsum(dgc[:tr])
        for k in range(3):
            dcw_ref[k:k + 1, :] += _colsum(dgc[:tr] * taps[k][:tr])

    return pl.pallas_call(
        body, name="ffn_act_bwd", grid=(F // tc, nt),
        in_specs=[pl.BlockSpec((tr, 2 * tc), lambda j, i: (i, j)),
                  pl.BlockSpec((H, 2 * tc), lambda j, i: (jnp.maximum(i * hb - 1, 0), j)),
                  pl.BlockSpec((H, 2 * tc), lambda j, i: (jnp.minimum((i + 1) * hb, nt * hb - 1), j)),
                  pl.BlockSpec((tr, tc), lambda j, i: (i, j)),
                  pl.BlockSpec((H, tc), lambda j, i: (jnp.minimum((i + 1) * hb, nt * hb - 1), j)),
                  pl.BlockSpec((3, tc), lambda j, i: (0, j)), pl.BlockSpec((1, tc), lambda j, i: (0, j))],
        out_specs=(pl.BlockSpec((tr, 2 * tc), lambda j, i: (i, j)), pl.BlockSpec((3, tc), lambda j, i: (0, j)),
                   pl.BlockSpec((1, tc), lambda j, i: (0, j))),
        out_shape=(jax.ShapeDtypeStruct((S, F2), BF16), jax.ShapeDtypeStruct((3, F), F32), jax.ShapeDtypeStruct((1, F), F32)),
        compiler_params=_params(("parallel", "arbitrary")),
    )(au, au, au, dp, dp, cw, cb)


def _row_tile(rows, cols):
    return _div_tile(rows, max(8, (2**18 // cols) // 8 * 8), 8)


def _sum_parts(parts, sel, rows, cols, name):
    tr = _row_tile(rows, cols)

    def body(sel_ref, *refs):
        acc = refs[0][...]
        for r in refs[1:-1]:
            acc = acc + r[...]
        refs[-1][...] = acc

    def spec(index):
        if isinstance(index, int):
            return pl.BlockSpec((None, tr, cols), lambda i, s: (index, i, 0))
        k, mul, off = index
        return pl.BlockSpec((None, tr, cols), lambda i, s: (s[k] * mul + off, i, 0))

    grid_spec = pltpu.PrefetchScalarGridSpec(
        num_scalar_prefetch=1, grid=(rows // tr,), in_specs=[spec(ix) for _, ix in parts],
        out_specs=pl.BlockSpec((tr, cols), lambda i, s: (i, 0)))
    return pl.pallas_call(
        body, name=name, grid_spec=grid_spec, out_shape=jax.ShapeDtypeStruct((rows, cols), F32),
        compiler_params=_params(("parallel",)),
    )(sel, *[a for a, _ in parts])


def _pair_sum(g, got, sel):
    n, rows, cols = got.shape
    tr = _row_tile(rows, cols)
    nb = rows // tr

    def body(sel_ref, a_ref, b_ref, o_ref):
        o_ref[...] = a_ref[...] + b_ref[...]

    grid_spec = pltpu.PrefetchScalarGridSpec(
        num_scalar_prefetch=1, grid=(n, nb),
        in_specs=[pl.BlockSpec((None, tr, cols), lambda q, i, s: (q, s[0] * nb + i, 0)),
                  pl.BlockSpec((None, tr, cols), lambda q, i, s: (q, i, 0))],
        out_specs=pl.BlockSpec((None, tr, cols), lambda q, i, s: (q, i, 0)))
    return pl.pallas_call(
        body, name="pair_sum", grid_spec=grid_spec, out_shape=jax.ShapeDtypeStruct((n, rows, cols), F32),
        compiler_params=_params(("parallel", "parallel")),
    )(sel, g, got)


def _chip_sum_cols(pair, arrived, sel):
    n, rows, cg = arrived.shape[1:]
    tr = _row_tile(rows, cg)

    def body(sel_ref, p_ref, a0, a1, a2, o_ref):
        o_ref[...] = ((p_ref[...] + a0[...]) + a1[...]) + a2[...]

    def arr(j):
        return pl.BlockSpec((None, None, tr, cg), lambda k, i, s: (j, k, i, 0))

    grid_spec = pltpu.PrefetchScalarGridSpec(
        num_scalar_prefetch=1, grid=(n, rows // tr),
        in_specs=[pl.BlockSpec((None, tr, cg), lambda k, i, s: (k, i, s[1])), arr(0), arr(1), arr(2)],
        out_specs=pl.BlockSpec((None, tr, cg), lambda k, i, s: (k, i, 0)))
    return pl.pallas_call(
        body, name="chip_sum_cols", grid_spec=grid_spec, out_shape=jax.ShapeDtypeStruct((n, rows, cg), F32),
        compiler_params=_params(("parallel", "parallel")),
    )(sel, pair, arrived, arrived, arrived)


def _adamw_math(w, g, m, v):
    m2 = ADAM_B1 * m + (1.0 - ADAM_B1) * g
    v2 = ADAM_B2 * v + (1.0 - ADAM_B2) * (g * g)
    m_hat = m2 / (1.0 - ADAM_B1 ** ADAM_STEP)
    v_hat = v2 / (1.0 - ADAM_B2 ** ADAM_STEP)
    return -ADAM_LR * (m_hat / (jnp.sqrt(v_hat) + ADAM_EPS) + ADAM_WD * w), m2, v2


def _adamw(w, g, m, v):
    R, C = w.shape
    tr = _div_tile(R, TILES["ew"], 8)

    def body(w_ref, g_ref, m_ref, v_ref, d_ref, m2_ref, v2_ref):
        d_ref[...], m2_ref[...], v2_ref[...] = _adamw_math(w_ref[...], g_ref[...], m_ref[...], v_ref[...])

    spec = pl.BlockSpec((tr, C), lambda i: (i, 0))
    shape = jax.ShapeDtypeStruct((R, C), F32)
    return pl.pallas_call(
        body, name="adamw", grid=(R // tr,), in_specs=[spec] * 4, out_specs=(spec,) * 3, out_shape=(shape,) * 3,
        compiler_params=_params(("parallel",)),
    )(w, g, m, v)


def _ada_grad_adamw(cact_t, dmod, w, m, v):
    L, D, N = w.shape
    tm, tn = _div_tile(D, 256, 8), _div_tile(N, 1024, 128)

    def body(c_ref, d_ref, w_ref, m_ref, v_ref, g_ref, dl_ref, m2_ref, v2_ref):
        g = c_ref[:, 0:1] * d_ref[0:1, :]
        for b in range(1, N_DEV):
            g = g + c_ref[:, b:b + 1] * d_ref[b:b + 1, :]
        g_ref[...] = g
        dl_ref[...], m2_ref[...], v2_ref[...] = _adamw_math(w_ref[...], g, m_ref[...], v_ref[...])

    big = pl.BlockSpec((None, tm, tn), lambda l, i, j: (l, i, j))
    shape = jax.ShapeDtypeStruct((L, D, N), F32)
    return pl.pallas_call(
        body, name="ada_grad_adamw", grid=(L, D // tm, N // tn),
        in_specs=[pl.BlockSpec((tm, N_DEV), lambda l, i, j: (i, 0)),
                  pl.BlockSpec((None, N_DEV, tn), lambda l, i, j: (l, 0, j)), big, big, big],
        out_specs=(big,) * 4, out_shape=(shape,) * 4, compiler_params=_params(("parallel", "parallel", "parallel")),
    )(cact_t, dmod, w, m, v)


def _silu_rows(c):
    def body(c_ref, o_ref):
        x = c_ref[...]
        o_ref[...] = x * _sigmoid(x)

    return pl.pallas_call(body, name="silu_rows", out_shape=jax.ShapeDtypeStruct(c.shape, F32))(c)


ANY = pl.BlockSpec(memory_space=pl.ANY)


def _position():
    return lax.axis_index("x"), lax.axis_index("y"), lax.axis_index("c")


def _other_chips(x, y):
    return [(1 - x, y), (x, 1 - y), (1 - x, 1 - y)]


def _allgather8(v):
    R, C = v.shape

    def body(v_ref, out_ref, send_sems, recv_sems, local_sem):
        x, y, c = _position()
        me = 4 * x + 2 * y + c
        mine = pltpu.make_async_copy(v_ref, out_ref.at[me], local_sem)
        mine.start()
        sends, recvs = [], []
        for k in range(1, N_DEV):
            px, py, pc = (x + (k >> 2)) % 2, (y + ((k >> 1) & 1)) % 2, (c + (k & 1)) % 2
            sends.append(pltpu.make_async_remote_copy(
                src_ref=v_ref, dst_ref=out_ref.at[me], send_sem=send_sems.at[k - 1], recv_sem=recv_sems.at[k - 1],
                device_id=(px, py, pc), device_id_type=MESH))
            recvs.append(pltpu.make_async_remote_copy(
                src_ref=v_ref, dst_ref=out_ref.at[4 * px + 2 * py + pc], send_sem=send_sems.at[k - 1],
                recv_sem=recv_sems.at[k - 1], device_id=(px, py, pc), device_id_type=MESH))
        for cp in sends:
            cp.start()
        for cp in recvs:
            cp.wait_recv()
        for cp in sends:
            cp.wait_send()
        mine.wait()

    return pl.pallas_call(
        body, name="comm_allgather8", out_shape=jax.ShapeDtypeStruct((N_DEV, R, C), v.dtype), in_specs=[ANY],
        out_specs=ANY,
        scratch_shapes=[pltpu.SemaphoreType.DMA((N_DEV - 1,)), pltpu.SemaphoreType.DMA((N_DEV - 1,)),
                        pltpu.SemaphoreType.DMA],
    )(v)


def _remote(src, dst, send_sems, recv_sems, k, to):
    return pltpu.make_async_remote_copy(src_ref=src, dst_ref=dst, send_sem=send_sems.at[k], recv_sem=recv_sems.at[k],
                                        device_id=to, device_id_type=MESH)


def _half_rows(ref, h):
    n = ref.shape[0] // 2
    return ref.at[pl.ds(h * n, n)]


def _gather_weights(l, win_s, wout_s, gate_s, up_s, down_s):
    D, CI = win_s.shape[1:]
    CG = gate_s.shape[2]
    n_t = 5

    def body(win, wout, gate, up, down, win4, wout4, gu, down4, send_sems, recv_sems, local_sems):
        x, y, c = _position()
        q = 2 * x + y
        chips = _other_chips(x, y)
        sibling = (x, y, 1 - c)
        tensors = [(win.at[l], lambda p: win4.at[p]), (wout.at[l], lambda p: wout4.at[p]),
                   (gate.at[l], lambda p: gu.at[0, :, pl.ds(p * CG, CG)]),
                   (up.at[l], lambda p: gu.at[1, :, pl.ds(p * CG, CG)]), (down.at[l], lambda p: down4.at[p])]
        own, first, passed = [], [], []
        for t, (src, dst) in enumerate(tensors):
            cp = pltpu.make_async_copy(src, dst(q), local_sems.at[t])
            cp.start()
            own.append(cp)
            for j, (px, py) in enumerate(chips):
                cp = _remote(_half_rows(src, c), _half_rows(dst(q), c), send_sems, recv_sems, 6 * t + j, (px, py, c))
                cp.start()
                first.append(cp)
        for t, (src, dst) in enumerate(tensors):
            for j, (px, py) in enumerate(chips):
                landed = _half_rows(dst(2 * px + py), c)
                _remote(landed, landed, send_sems, recv_sems, 6 * t + j, (px, py, c)).wait_recv()
                cp = _remote(landed, landed, send_sems, recv_sems, 6 * t + 3 + j, sibling)
                cp.start()
                passed.append(cp)
        for t, (src, dst) in enumerate(tensors):
            for j, (px, py) in enumerate(chips):
                landed = _half_rows(dst(2 * px + py), 1 - c)
                _remote(landed, landed, send_sems, recv_sems, 6 * t + 3 + j, sibling).wait_recv()
        for cp in first + passed:
            cp.wait_send()
        for cp in own:
            cp.wait()

    shapes = (jax.ShapeDtypeStruct((N_CHIPS, D, CI), BF16), jax.ShapeDtypeStruct((N_CHIPS,) + wout_s.shape[1:], BF16),
              jax.ShapeDtypeStruct((2, D, N_CHIPS * CG), BF16), jax.ShapeDtypeStruct((N_CHIPS,) + down_s.shape[1:], BF16))
    return pl.pallas_call(
        body, name="comm_gather_weights", out_shape=shapes, in_specs=[ANY] * n_t, out_specs=(ANY,) * 4,
        scratch_shapes=[pltpu.SemaphoreType.DMA((6 * n_t,)), pltpu.SemaphoreType.DMA((6 * n_t,)),
                        pltpu.SemaphoreType.DMA((n_t,))],
    )(win_s, wout_s, gate_s, up_s, down_s)


def _sibling_swap_halves(gs):
    n_t = len(gs)

    def body(*refs):
        ins, outs, (send_sems, recv_sems) = refs[:n_t], refs[n_t:2 * n_t], refs[2 * n_t:]
        x, y, c = _position()
        cps = []
        for t in range(n_t):
            half = ins[t].shape[1] // 2
            cps.append(_remote(ins[t].at[:, pl.ds((1 - c) * half, half), :], outs[t], send_sems, recv_sems, t,
                               (x, y, 1 - c)))
        for cp in cps:
            cp.start()
        for cp in cps:
            cp.wait()

    shapes = tuple(jax.ShapeDtypeStruct((g.shape[0], g.shape[1] // 2, g.shape[2]), g.dtype) for g in gs)
    return pl.pallas_call(
        body, name="comm_sibling_swap", out_shape=shapes, in_specs=[ANY] * n_t, out_specs=(ANY,) * n_t,
        scratch_shapes=[pltpu.SemaphoreType.DMA((n_t,)), pltpu.SemaphoreType.DMA((n_t,))],
    )(*gs)


def _chip_scatter(leads, gu):
    n_l = len(leads)
    CG = gu.shape[2] // N_CHIPS
    per = n_l + 2

    def body(*refs):
        ins, gu_ref, outs, gu_out = refs[:n_l], refs[n_l], refs[n_l + 1:2 * n_l + 1], refs[2 * n_l + 1]
        send_sems, recv_sems = refs[2 * n_l + 2:]
        x, y, c = _position()
        cps = []
        for j, (px, py) in enumerate(_other_chips(x, y)):
            p = 2 * px + py
            for t in range(n_l):
                cps.append(_remote(ins[t].at[p], outs[t].at[j], send_sems, recv_sems, per * j + t, (px, py, c)))
            for k in range(2):
                cps.append(_remote(gu_ref.at[k, :, pl.ds(p * CG, CG)], gu_out.at[j, k], send_sems, recv_sems,
                                   per * j + n_l + k, (px, py, c)))
        for cp in cps:
            cp.start()
        for cp in cps:
            cp.wait()

    shapes = tuple(jax.ShapeDtypeStruct((3,) + g.shape[1:], g.dtype) for g in leads)
    shapes += (jax.ShapeDtypeStruct((3, 2, gu.shape[1], CG), gu.dtype),)
    return pl.pallas_call(
        body, name="comm_chip_scatter", out_shape=shapes, in_specs=[ANY] * (n_l + 1), out_specs=(ANY,) * (n_l + 1),
        scratch_shapes=[pltpu.SemaphoreType.DMA((3 * per,)), pltpu.SemaphoreType.DMA((3 * per,))],
    )(*leads, gu)


def _sibling_allgather(vs):
    n_t = len(vs)

    def body(*refs):
        ins, outs, (send_sems, recv_sems, local_sems) = refs[:n_t], refs[n_t:2 * n_t], refs[2 * n_t:]
        x, y, c = _position()
        cps, own, recvs = [], [], []
        for t in range(n_t):
            R = ins[t].shape[-2]
            rows = lambda h: outs[t].at[:, pl.ds(h * R, R), :] if len(ins[t].shape) == 3 else outs[t].at[pl.ds(h * R, R)]
            own.append(pltpu.make_async_copy(ins[t], rows(c), local_sems.at[t]))
            cps.append(_remote(ins[t], rows(c), send_sems, recv_sems, t, (x, y, 1 - c)))
            recvs.append(_remote(ins[t], rows(1 - c), send_sems, recv_sems, t, (x, y, 1 - c)))
        for cp in own + cps:
            cp.start()
        for cp in recvs:
            cp.wait_recv()
        for cp in cps:
            cp.wait_send()
        for cp in own:
            cp.wait()

    shapes = tuple(jax.ShapeDtypeStruct(v.shape[:-2] + (2 * v.shape[-2], v.shape[-1]), v.dtype) for v in vs)
    return pl.pallas_call(
        body, name="comm_sibling_allgather", out_shape=shapes, in_specs=[ANY] * n_t, out_specs=(ANY,) * n_t,
        scratch_shapes=[pltpu.SemaphoreType.DMA((n_t,)), pltpu.SemaphoreType.DMA((n_t,)), pltpu.SemaphoreType.DMA((n_t,))],
    )(*vs)


def _reduce_scatter(leads, gu, c_idx, q_idx):
    sel = jnp.stack([c_idx, q_idx]).astype(jnp.int32)
    got = _sibling_swap_halves(list(leads) + [gu])
    pairs = [_pair_sum(g, r, sel) for g, r in zip(list(leads) + [gu], got)]
    arrived = _chip_scatter(pairs[:-1], pairs[-1])
    mine = [_sum_parts([(p, (1, 1, 0)), (a, 0), (a, 1), (a, 2)], sel, p.shape[1], p.shape[2], "chip_sum")
            for p, a in zip(pairs[:-1], arrived[:-1])]
    mine.append(_chip_sum_cols(pairs[-1], arrived[-1], sel))
    return _sibling_allgather(mine)


def _layer_fwd(x, mod, p):
    sh1, sc1, gt1, sh2, sc2, gt2 = mod
    h1 = _norm_mod(x, p["g_mix"], sc1, sh1)
    z = _matmul(h1, p["w_in"], name="mm_in")
    y_pool = _pool_fwd(z, p["pool_w"], p["pool_scale"])
    F = _forget_cumsum(z, p["b_f"])
    Fh = F[:, :N_HEADS].T
    f_col, f_row = Fh[:, :, None], Fh[:, None, :]
    o, lse = _flash_fwd(z, f_col, f_row)
    y_lru, hs = _lru_fwd(z, p["lru_conv_w"], p["lru_conv_b"], p["lru_wa"], p["lru_ba"], p["lru_wi"], p["lru_bi"],
                         p["lru_lambda"])
    y = jnp.concatenate([y_pool.astype(BF16), o.astype(BF16), y_lru.astype(BF16)], axis=1)
    m1, x_mid = _matmul(y, p["w_out"], res=x, gate=gt1, name="mm_out")
    h2 = _norm_mod(x_mid, p["g_ffn"], sc2, sh2)
    au = _matmul(h2, p["w_gu"], gu="b", name="mm_gu")
    pa = _ffn_act(au, p["ffn_conv_w"], p["ffn_conv_b"])
    m2, x_out = _matmul(pa, p["w_down"], res=x_mid, gate=gt2, name="mm_down")
    saved = dict(x=x, h1=h1, z=z, f_col=f_col, f_row=f_row, lse=lse, hs=hs, y=y, m1=m1, x_mid=x_mid, h2=h2, au=au,
                 pa=pa, m2=m2)
    return x_out, saved


def _layer_bwd(dx_out, mod, p, s):
    sh1, sc1, gt1, sh2, sc2, gt2 = mod
    g = {}
    dm2, dgt2 = _gate_bwd(dx_out, s["m2"], gt2)
    dpa = _matmul(dm2, p["w_down"], nt=True, name="mm_down_dx")
    g["w_down"] = _matmul(s["pa"].T, dm2, name="mm_down_dw")
    dau, g["ffn_conv_w"], g["ffn_conv_b"] = _ffn_act_bwd(s["au"], dpa, p["ffn_conv_w"], p["ffn_conv_b"])
    dh2 = _matmul(dau, p["w_gu"], nt=True, gu="b", name="mm_gu_dx")
    g["w_gu"] = _matmul(s["h2"].T, dau, gu="out", name="mm_gu_dw")
    dx_mid, g["g_ffn"], dsc2, dsh2 = _norm_mod_bwd(s["x_mid"], dh2, dx_out, p["g_ffn"], sc2)
    dm1, dgt1 = _gate_bwd(dx_mid, s["m1"], gt1)
    dy = _matmul(dm1, p["w_out"], nt=True, name="mm_out_dx")
    g["w_out"] = _matmul(s["y"].T, dm1, name="mm_out_dw")
    z = s["z"]
    (dzx, dzy, g["lru_conv_w"], g["lru_conv_b"], g["lru_wa"], g["lru_ba"], g["lru_wi"], g["lru_bi"],
     g["lru_lambda"]) = _lru_bwd(z, dy, s["hs"], p["lru_conv_w"], p["lru_conv_b"], p["lru_wa"], p["lru_ba"], p["lru_wi"],
                                 p["lru_bi"], p["lru_lambda"])
    dq, delta, dfq = _flash_bwd_dq(z, dy, s["lse"], s["f_col"], s["f_row"])
    row = lambda a: jnp.transpose(a, (0, 2, 1))
    dk, dv, dfk = _flash_bwd_dkv(z, dy, row(s["lse"]), row(delta), s["f_col"], s["f_row"])
    dF_pad = jnp.pad((dfq + dfk)[:, :, 0].T, ((0, 0), (0, 128 - N_HEADS)))
    dzf, db_f = _forget_cumsum_bwd(z, p["b_f"], dF_pad)
    g["b_f"] = db_f[:, :N_HEADS]
    dzp, g["pool_w"], g["pool_scale"] = _pool_bwd(z, dy, p["pool_w"], p["pool_scale"])
    S = z.shape[0]
    dz = jnp.concatenate([dzp, dq, dk, dv, dzx, dzy, dzf, jnp.zeros((S, ZW - ZF - 128), BF16)], axis=1)
    dh1 = _matmul(dz, p["w_in"], nt=True, name="mm_in_dx")
    g["w_in"] = _matmul(s["h1"].T, dz, name="mm_in_dw")
    dx_in, g["g_mix"], dsc1, dsh1 = _norm_mod_bwd(s["x"], dh1, dx_mid, p["g_mix"], sc1)
    return dx_in, g, (dsh1, dsc1, dgt1, dsh2, dsc2, dgt2)


def _local_step(x, target, mods, layers, final_g):
    saved = []
    for mod, p in zip(mods, layers):
        x, s = _layer_fwd(x, mod, p)
        saved.append(s)
    dx, dfinal_g, loss = _loss_head(x, target, final_g)
    grads, dmods = [None] * len(layers), [None] * len(layers)
    for l in reversed(range(len(layers))):
        dx, grads[l], dmods[l] = _layer_bwd(dx, mods[l], layers[l], saved[l])
    return loss, dx, grads, dmods, dfinal_g


def _pad_in_cols(w):
    D = w.shape[0]
    return jnp.concatenate([w[:, :3584], w[:, 3592:N_IN], w[:, 3584:3592], jnp.zeros((D, ZW - N_IN), w.dtype)], axis=1)


def _unpad_in_cols(w):
    return jnp.concatenate([w[:, :3584], w[:, ZF:ZF + N_HEADS], w[:, 3584:ZF]], axis=1)


BIG = ("w_in", "w_out", "w_ffn_gate", "w_ffn_up", "w_ffn_down")
SMALL = ("b_ada", "g_mix", "b_f", "pool_w", "pool_scale", "lru_conv_w", "lru_conv_b", "lru_wa", "lru_ba", "lru_wi",
         "lru_bi", "lru_lambda", "g_ffn", "ffn_conv_w", "ffn_conv_b", "final_g")
SHARDED_SMALL = ("lru_conv_w", "ffn_conv_w")
WEIGHTS = ("w_ada", "b_ada", "g_mix", "w_in", "b_f", "pool_w", "pool_scale", "lru_conv_w", "lru_conv_b", "lru_wa",
           "lru_ba", "lru_wi", "lru_bi", "lru_lambda", "w_out", "g_ffn", "w_ffn_gate", "w_ffn_up", "ffn_conv_w",
           "ffn_conv_b", "w_ffn_down", "final_g")


def _pack(arrays):
    flat = jnp.concatenate([a.reshape(-1).astype(F32) for a in arrays])
    n = -(-flat.shape[0] // 1024) * 1024
    return jnp.pad(flat, (0, n - flat.shape[0])).reshape(n // 128, 128)


def _unpack(packed, shapes):
    flat = packed.reshape(-1)
    out, off = [], 0
    for shp in shapes:
        n = int(np.prod(shp))
        out.append(flat[off:off + n].reshape(shp))
        off += n
    return out


def kernel(x, c, w_ada, b_ada, g_mix, w_in, b_f, pool_w, pool_scale, lru_conv_w, lru_conv_b, lru_wa, lru_ba, lru_wi, lru_bi, lru_lambda, w_out, g_ffn, w_ffn_gate, w_ffn_up, ffn_conv_w, ffn_conv_b, w_ffn_down, final_g, loss_target, m_w_ada, m_b_ada, m_g_mix, m_w_in, m_b_f, m_pool_w, m_pool_scale, m_lru_conv_w, m_lru_conv_b, m_lru_wa, m_lru_ba, m_lru_wi, m_lru_bi, m_lru_lambda, m_w_out, m_g_ffn, m_w_ffn_gate, m_w_ffn_up, m_ffn_conv_w, m_ffn_conv_b, m_w_ffn_down, m_final_g, v_w_ada, v_b_ada, v_g_mix, v_w_in, v_b_f, v_pool_w, v_pool_scale, v_lru_conv_w, v_lru_conv_b, v_lru_wa, v_lru_ba, v_lru_wi, v_lru_bi, v_lru_lambda, v_w_out, v_g_ffn, v_w_ffn_gate, v_w_ffn_up, v_ffn_conv_w, v_ffn_conv_b, v_w_ffn_down, v_final_g):
    env = dict(locals())
    W = {n: env[n] for n in WEIGHTS}
    M = {n: env["m_" + n] for n in WEIGHTS}
    V = {n: env["v_" + n] for n in WEIGHTS}
    L, D = g_mix.shape
    S = x.shape[1]
    F = 4 * w_ffn_gate.shape[2]
    ix, iy, ic = lax.axis_index("x"), lax.axis_index("y"), lax.axis_index("c")
    q = 2 * ix + iy
    me = 2 * q + ic

    head = _allgather8(_pack([c, lru_conv_w, ffn_conv_w]))
    nlc, nfc = lru_conv_w.size, ffn_conv_w.size
    c_all = head.reshape(N_DEV, -1)[:, :D]
    lru_cw = jnp.concatenate([head[2 * k].reshape(-1)[D:D + nlc].reshape(L, 4, -1) for k in range(N_CHIPS)], axis=2)
    ffn_cw = jnp.concatenate([head[2 * k].reshape(-1)[D + nlc:D + nlc + nfc].reshape(L, 3, -1) for k in range(N_CHIPS)],
                             axis=2)
    cact = _silu_rows(c_all)

    NA = w_ada.shape[2]
    mod_part = jnp.stack([_matmul(cact, w_ada[l], name="mm_ada") for l in range(L)])
    mod_all = _allgather8(mod_part.reshape(L * N_DEV, NA)).reshape(N_DEV, L, N_DEV, NA)
    mod_full = jnp.concatenate([mod_all[2 * k] for k in range(N_CHIPS)], axis=2)
    mod_mine = lax.dynamic_index_in_dim(mod_full, me, axis=1, keepdims=False) + b_ada
    mods = [[mod_mine[l, k * D:(k + 1) * D].reshape(1, D) for k in range(6)] for l in range(L)]

    shards = [W[n].astype(BF16) for n in BIG]
    layers = []
    for l in range(L):
        win4, wout4, gu, down4 = _gather_weights(l, *shards)
        layers.append(dict(
            w_in=_pad_in_cols(jnp.transpose(win4, (1, 0, 2)).reshape(D, N_IN)), w_out=wout4.reshape(D, D),
            w_gu=gu, w_down=down4.reshape(F, D),
            g_mix=g_mix[l][None], g_ffn=g_ffn[l][None], b_f=jnp.pad(b_f[l], (0, 128 - N_HEADS))[None],
            pool_w=pool_w[l], pool_scale=pool_scale[l][None], lru_conv_w=lru_cw[l], lru_conv_b=lru_conv_b[l][None],
            lru_wa=lru_wa[l], lru_ba=lru_ba[l][None], lru_wi=lru_wi[l], lru_bi=lru_bi[l][None],
            lru_lambda=lru_lambda[l][None], ffn_conv_w=ffn_cw[l], ffn_conv_b=ffn_conv_b[l][None]))

    loss, dx, grads, dmods, dfinal_g = _local_step(x[0], loss_target[0], mods, layers, final_g[None])

    G = {n: [] for n in BIG}
    for l in range(L):
        gl = grads[l]
        dwin4 = jnp.transpose(_unpad_in_cols(gl["w_in"]).reshape(D, N_CHIPS, N_IN // N_CHIPS), (1, 0, 2))
        leads = [dwin4, gl["w_out"].reshape(N_CHIPS, D // N_CHIPS, D), gl["w_down"].reshape(N_CHIPS, F // N_CHIPS, D)]
        g_in, g_out, g_down, g_gu = _reduce_scatter(leads, gl["w_gu"], ic, q)
        for n, a in zip(BIG, (g_in, g_out, g_gu[0], g_gu[1], g_down)):
            G[n].append(a)
    G = {n: jnp.stack(G[n]) for n in BIG}

    stack = lambda name: jnp.stack([grads[l][name] for l in range(L)])
    dmod = jnp.stack([jnp.concatenate(dmods[l], axis=1)[0] for l in range(L)])
    small = dict(b_ada=dmod, g_mix=stack("g_mix"), b_f=stack("b_f"), pool_w=stack("pool_w"),
                 pool_scale=stack("pool_scale"), lru_conv_w=stack("lru_conv_w"), lru_conv_b=stack("lru_conv_b"),
                 lru_wa=stack("lru_wa"), lru_ba=stack("lru_ba"), lru_wi=stack("lru_wi"), lru_bi=stack("lru_bi"),
                 lru_lambda=stack("lru_lambda"), g_ffn=stack("g_ffn"), ffn_conv_w=stack("ffn_conv_w"),
                 ffn_conv_b=stack("ffn_conv_b"), final_g=dfinal_g)
    packed = _pack([small[n] for n in SMALL] + [loss[0, :1]])
    everyone = _allgather8(packed)
    zero_sel = jnp.zeros((2,), jnp.int32)
    total = _sum_parts([(everyone, k) for k in range(N_DEV)], zero_sel, packed.shape[0], 128, "device_sum")
    sums = _unpack(total, [small[n].shape for n in SMALL] + [(1,)])
    loss_total = sums[-1][0]
    for n, a in zip(SMALL, sums[:-1]):
        a = a.reshape((L, -1, a.shape[-1])) if n in SHARDED_SMALL else a.reshape(W[n].shape)
        if n in SHARDED_SMALL:
            a = lax.dynamic_slice_in_dim(a, q * W[n].shape[2], W[n].shape[2], axis=2)
        G[n] = a

    dmod_all = everyone.reshape(N_DEV, -1)[:, :L * 6 * D].reshape(N_DEV, L, 6 * D)
    dmod_cols = jnp.transpose(lax.dynamic_slice_in_dim(dmod_all, q * NA, NA, axis=2), (1, 0, 2))
    G["w_ada"], d_ada, m_ada, v_ada = _ada_grad_adamw(cact.T, dmod_cols, w_ada, m_w_ada, v_w_ada)
    delta, new_m, new_v = {"w_ada": d_ada}, {"w_ada": m_ada}, {"w_ada": v_ada}

    for n in BIG:
        cols = W[n].shape[-1]
        outs = _adamw(*[a.reshape(-1, cols) for a in (W[n], G[n], M[n], V[n])])
        delta[n], new_m[n], new_v[n] = [o.reshape(W[n].shape) for o in outs]
    outs = _adamw(*[_pack([t[n] for n in SMALL]) for t in (W, G, M, V)])
    shapes = [W[n].shape for n in SMALL]
    for tgt, o in zip((delta, new_m, new_v), outs):
        for n, a in zip(SMALL, _unpack(o, shapes)):
            tgt[n] = a

    return (loss_total, dx[None], *[G[n] for n in WEIGHTS], *[delta[n] for n in WEIGHTS],
            *[new_m[n] for n in WEIGHTS], *[new_v[n] for n in WEIGHTS])
```

```python
import functools
import math

import jax
import jax.numpy as jnp
import numpy as np
from jax import lax
from jax.experimental import pallas as pl
from jax.experimental.pallas import tpu as pltpu

F32 = jnp.float32
BF16 = jnp.bfloat16
MESH = pl.DeviceIdType.MESH

EPS = 1e-6
HEAD_DIM = 128
POOL_WIDTH = 512
POOL_WINDOWS = (2, 4, 8, 16)
ATTN_WIDTH = 1024
N_HEADS = 8
LRU_WIDTH = 512
LRU_C = 8.0
N_IN = 4616
ZP, ZQ, ZK, ZV, ZX, ZY, ZF, ZW = 0, 512, 1536, 2560, 3584, 4096, 4608, 5120
FFN_CHUNK = 512
N_CHIPS = 4
N_DEV = 8

ADAM_LR, ADAM_B1, ADAM_B2, ADAM_EPS, ADAM_WD, ADAM_STEP = 0.001, 0.9, 0.999, 1e-08, 0.01, 10

TILES = dict(mm_m=512, mm_n=1024, mm_k=2048, row=512, attn=512, lru=256, cum=512, ew=256)
VMEM_LIMIT = 48 * 2**20


def _params(sem):
    return pltpu.CompilerParams(dimension_semantics=sem, vmem_limit_bytes=VMEM_LIMIT)


def _div_tile(n, pref, align):
    if n <= pref:
        return n
    t = (pref // align) * align
    while t >= align:
        if n % t == 0:
            return t
        t -= align
    raise ValueError(f"no tile for {n}")


def _sigmoid(x):
    return 1.0 / (1.0 + jnp.exp(-x))


def _gelu_parts(x):
    k = math.sqrt(2.0 / math.pi)
    u = k * (x + 0.044715 * x * x * x)
    t = jnp.tanh(u)
    gel = 0.5 * x * (1.0 + t)
    dgel = 0.5 * (1.0 + t) + 0.5 * x * (1.0 - t * t) * k * (1.0 + 3 * 0.044715 * x * x)
    return gel, dgel


def _neg_expm1(y):
    series = -y * (1.0 + y * (0.5 + y * (1.0 / 6 + y * (1.0 / 24 + y * (1.0 / 120)))))
    return jnp.where(y > -0.1, series, 1.0 - jnp.exp(y))


def _dot(a, b):
    return jnp.dot(a.astype(BF16), b.astype(BF16), preferred_element_type=F32)


def _dot_nt(a, b):
    return lax.dot_general(a.astype(BF16), b.astype(BF16), (((1,), (1,)), ((), ())), preferred_element_type=F32)


def _dot3(tri, v):
    hi = v.astype(BF16)
    r1 = v - hi.astype(F32)
    mid = r1.astype(BF16)
    lo = (r1 - mid.astype(F32)).astype(BF16)
    t = tri.astype(BF16)
    return (jnp.dot(t, hi, preferred_element_type=F32) + jnp.dot(t, mid, preferred_element_type=F32)
            + jnp.dot(t, lo, preferred_element_type=F32))


def _colsum(v):
    return jnp.sum(v, axis=0, keepdims=True)


def _rows(n, cols=128):
    return lax.broadcasted_iota(jnp.int32, (n, cols), 0)


def _matmul(a, b, *, nt=False, out_dtype=F32, res=None, gate=None, gu=None, name="matmul"):
    M, K = a.shape
    pair = 2 * FFN_CHUNK
    if gu == "b":
        N = b.shape[1] if nt else 2 * b.shape[2]
    else:
        N = b.shape[0] if nt else b.shape[1]
    tm = _div_tile(M, TILES["mm_m"] * (1 if gu is None else 2), 8)
    tn = _div_tile(N, TILES["mm_n"], 128)
    tk = _div_tile(K, TILES["mm_k"], 128)
    if gu == "b" and nt:
        tk = pair
    elif gu == "b":
        tn = pair
    elif gu == "out":
        tn = FFN_CHUNK
    nk = K // tk
    epi = res is not None
    n_b = 2 if gu == "b" else 1

    def body(*refs):
        a_ref, b_refs, rest = refs[0], refs[1:1 + n_b], refs[1 + n_b:]
        if epi:
            res_ref, gate_ref, o_ref, x_ref = rest[:4]
        else:
            o_ref = rest[0]
        if gu == "b" and nt:
            part = (_dot_nt(a_ref[:, :FFN_CHUNK], b_refs[0][...]) + _dot_nt(a_ref[:, FFN_CHUNK:], b_refs[1][...]))
        elif gu == "b":
            part = jnp.concatenate([_dot(a_ref[...], b_refs[0][...]), _dot(a_ref[...], b_refs[1][...])], axis=1)
        else:
            part = _dot_nt(a_ref[...], b_refs[0][...]) if nt else _dot(a_ref[...], b_refs[0][...])

        def finish(acc):
            o_ref[...] = acc.astype(o_ref.dtype)
            if epi:
                x_ref[...] = res_ref[...] + gate_ref[...] * acc

        if nk == 1:
            finish(part)
        else:
            acc_ref = refs[-1]
            k = pl.program_id(2)

            @pl.when(k == 0)
            def _():
                acc_ref[...] = part

            @pl.when(k > 0)
            def _():
                acc_ref[...] += part

            @pl.when(k == nk - 1)
            def _():
                finish(acc_ref[...])

    a_spec = pl.BlockSpec((tm, tk), lambda i, j, k: (i, k))
    o_spec = pl.BlockSpec((tm, tn), lambda i, j, k: (i, j))
    out_shape = jax.ShapeDtypeStruct((M, N), out_dtype)
    if gu == "b" and nt:
        b_specs = [pl.BlockSpec((None, tn, FFN_CHUNK), lambda i, j, k, p=p: (p, j, k)) for p in range(2)]
    elif gu == "b":
        b_specs = [pl.BlockSpec((None, tk, FFN_CHUNK), lambda i, j, k, p=p: (p, k, j)) for p in range(2)]
    elif nt:
        b_specs = [pl.BlockSpec((tn, tk), lambda i, j, k: (j, k))]
    else:
        b_specs = [pl.BlockSpec((tk, tn), lambda i, j, k: (k, j))]
    if gu == "out":
        o_spec = pl.BlockSpec((None, tm, tn), lambda i, j, k: (j % 2, i, j // 2))
        out_shape = jax.ShapeDtypeStruct((2, M, N // 2), out_dtype)
    in_specs, args = [a_spec] + b_specs, [a] + [b] * n_b
    out_specs = o_spec
    if epi:
        in_specs += [o_spec, pl.BlockSpec((1, tn), lambda i, j, k: (0, j))]
        args += [res, gate]
        out_specs = (o_spec, o_spec)
        out_shape = (out_shape, jax.ShapeDtypeStruct((M, N), F32))
    return pl.pallas_call(
        body, name=name, grid=(M // tm, N // tn, nk), in_specs=in_specs, out_specs=out_specs, out_shape=out_shape,
        scratch_shapes=[pltpu.VMEM((tm, tn), F32)] if nk > 1 else [],
        compiler_params=_params(("parallel", "parallel", "arbitrary")),
    )(*args)


def _norm_mod(x, g, sc, sh):
    S, D = x.shape
    tr = _div_tile(S, TILES["row"], 8)

    def body(x_ref, g_ref, sc_ref, sh_ref, h_ref):
        xf = x_ref[...]
        r = lax.rsqrt(jnp.mean(xf * xf, axis=-1, keepdims=True) + EPS)
        h_ref[...] = (((xf * r) * g_ref[...]) * (1.0 + sc_ref[...]) + sh_ref[...]).astype(h_ref.dtype)

    row = pl.BlockSpec((tr, D), lambda i: (i, 0))
    vec = pl.BlockSpec((1, D), lambda i: (0, 0))
    return pl.pallas_call(
        body, name="norm_mod", grid=(S // tr,), in_specs=[row, vec, vec, vec], out_specs=row,
        out_shape=jax.ShapeDtypeStruct((S, D), BF16), compiler_params=_params(("parallel",)),
    )(x, g, sc, sh)


def _norm_mod_bwd(x, dh, dres, g, sc):
    S, D = x.shape
    tr = _div_tile(S, TILES["ew"], 8)

    def body(x_ref, dh_ref, dres_ref, g_ref, sc_ref, dx_ref, dg_ref, dsc_ref, dsh_ref):
        xf, dh_ = x_ref[...], dh_ref[...]
        r = lax.rsqrt(jnp.mean(xf * xf, axis=-1, keepdims=True) + EPS)
        xhat = xf * r
        dxhat = dh_ * (g_ref[...] * (1.0 + sc_ref[...]))
        dx_ref[...] = dres_ref[...] + r * (dxhat - xhat * jnp.mean(dxhat * xhat, axis=-1, keepdims=True))
        t = _colsum(dh_ * xhat)

        @pl.when(pl.program_id(0) == 0)
        def _():
            dg_ref[...] = jnp.zeros_like(dg_ref)
            dsc_ref[...] = jnp.zeros_like(dsc_ref)
            dsh_ref[...] = jnp.zeros_like(dsh_ref)

        dg_ref[...] += t * (1.0 + sc_ref[...])
        dsc_ref[...] += t * g_ref[...]
        dsh_ref[...] += _colsum(dh_)

    row = pl.BlockSpec((tr, D), lambda i: (i, 0))
    vec = pl.BlockSpec((1, D), lambda i: (0, 0))
    vshape = jax.ShapeDtypeStruct((1, D), F32)
    return pl.pallas_call(
        body, name="norm_mod_bwd", grid=(S // tr,), in_specs=[row, row, row, vec, vec], out_specs=(row, vec, vec, vec),
        out_shape=(jax.ShapeDtypeStruct((S, D), F32), vshape, vshape, vshape), compiler_params=_params(("arbitrary",)),
    )(x, dh, dres, g, sc)


def _gate_bwd(dx, m, gt):
    S, D = dx.shape
    tr = _div_tile(S, TILES["row"], 8)

    def body(dx_ref, m_ref, gt_ref, dm_ref, dgt_ref):
        d = dx_ref[...]
        dm_ref[...] = (d * gt_ref[...]).astype(dm_ref.dtype)

        @pl.when(pl.program_id(0) == 0)
        def _():
            dgt_ref[...] = jnp.zeros_like(dgt_ref)

        dgt_ref[...] += _colsum(d * m_ref[...])

    row = pl.BlockSpec((tr, D), lambda i: (i, 0))
    vec = pl.BlockSpec((1, D), lambda i: (0, 0))
    return pl.pallas_call(
        body, name="gate_bwd", grid=(S // tr,), in_specs=[row, row, vec], out_specs=(row, vec),
        out_shape=(jax.ShapeDtypeStruct((S, D), BF16), jax.ShapeDtypeStruct((1, D), F32)),
        compiler_params=_params(("arbitrary",)),
    )(dx, m, gt)


def _loss_head(x, target, g):
    S, D = x.shape
    tr = _div_tile(S, TILES["ew"], 8)

    def body(x_ref, t_ref, g_ref, dx_ref, dg_ref, loss_ref):
        xf = x_ref[...]
        r = lax.rsqrt(jnp.mean(xf * xf, axis=-1, keepdims=True) + EPS)
        xhat = xf * r
        err = xhat * g_ref[...] - t_ref[...]
        dy = err * (1.0 / D)
        dxhat = dy * g_ref[...]
        dx_ref[...] = r * (dxhat - xhat * jnp.mean(dxhat * xhat, axis=-1, keepdims=True))

        @pl.when(pl.program_id(0) == 0)
        def _():
            dg_ref[...] = jnp.zeros_like(dg_ref)
            loss_ref[...] = jnp.zeros_like(loss_ref)

        dg_ref[...] += _colsum(dy * xhat)
        loss_ref[...] += 0.5 * jnp.sum(jnp.mean(err * err, axis=-1, keepdims=True))

    row = pl.BlockSpec((tr, D), lambda i: (i, 0))
    vec = pl.BlockSpec((1, D), lambda i: (0, 0))
    one = pl.BlockSpec((1, 128), lambda i: (0, 0))
    return pl.pallas_call(
        body, name="loss_head", grid=(S // tr,), in_specs=[row, row, vec], out_specs=(row, vec, one),
        out_shape=(jax.ShapeDtypeStruct((S, D), F32), jax.ShapeDtypeStruct((1, D), F32),
                   jax.ShapeDtypeStruct((1, 128), F32)),
        compiler_params=_params(("arbitrary",)),
    )(x, target, g)


POOL_HALO = 16


def _pool_delta(ext, u, first_pos, tr):
    pos = (first_pos + _rows(tr) + 1).astype(F32)
    outs = []
    for gi, win in enumerate(POOL_WINDOWS):
        s = ext[:, gi * 128:(gi + 1) * 128]
        d = 1
        while d < win:
            s = s + pltpu.roll(s, d, 0)
            d *= 2
        outs.append(s[POOL_HALO:] / jnp.minimum(pos, float(win)) - u[:, gi * 128:(gi + 1) * 128])
    return outs


def _pool_fwd(z, w, scale):
    S = z.shape[0]
    tr = _div_tile(S, TILES["row"], POOL_HALO)
    hb = tr // POOL_HALO

    def body(z_ref, halo_ref, w_ref, sc_ref, y_ref):
        i = pl.program_id(0)
        u = z_ref[...]
        halo = jnp.where(i > 0, halo_ref[...], 0.0)
        ds_ = _pool_delta(jnp.concatenate([halo, u], axis=0), u, i * tr, tr)
        for gi in range(4):
            y_ref[:, gi * 128:(gi + 1) * 128] = _dot(ds_[gi], w_ref[gi]) * sc_ref[:, gi * 128:(gi + 1) * 128]

    return pl.pallas_call(
        body, name="pool_fwd", grid=(S // tr,),
        in_specs=[pl.BlockSpec((tr, POOL_WIDTH), lambda i: (i, 0)),
                  pl.BlockSpec((POOL_HALO, POOL_WIDTH), lambda i: (jnp.maximum(i * hb - 1, 0), 0)),
                  pl.BlockSpec((4, 128, 128), lambda i: (0, 0, 0)), pl.BlockSpec((1, POOL_WIDTH), lambda i: (0, 0))],
        out_specs=pl.BlockSpec((tr, POOL_WIDTH), lambda i: (i, 0)),
        out_shape=jax.ShapeDtypeStruct((S, POOL_WIDTH), F32), compiler_params=_params(("parallel",)),
    )(z, z, w, scale)


def _pool_bwd(z, dy, w, scale):
    S = z.shape[0]
    tr = _div_tile(S, TILES["row"], POOL_HALO)
    hb = tr // POOL_HALO
    nt = S // tr

    def body(z_ref, halo_ref, dy_ref, dyn_ref, w_ref, sc_ref, dz_ref, dw_ref, dsc_ref):
        i = pl.program_id(0)
        u = z_ref[...]
        halo = jnp.where(i > 0, halo_ref[...], 0.0)
        ds_ = _pool_delta(jnp.concatenate([halo, u], axis=0), u, i * tr, tr)
        dy_ext = jnp.concatenate([dy_ref[...], jnp.where(i < nt - 1, dyn_ref[...], 0.0)], axis=0)
        pos = (i * tr + _rows(tr + POOL_HALO) + 1).astype(F32)

        @pl.when(i == 0)
        def _():
            dw_ref[...] = jnp.zeros_like(dw_ref)
            dsc_ref[...] = jnp.zeros_like(dsc_ref)

        for gi, win in enumerate(POOL_WINDOWS):
            cols = slice(gi * 128, (gi + 1) * 128)
            dyg = dy_ext[:, cols]
            dys = dyg * sc_ref[:, cols]
            dsc_ref[:, cols] += _colsum(dyg[:tr] * _dot(ds_[gi], w_ref[gi]))
            dw_ref[gi] += _dot(ds_[gi].T, dys[:tr])
            dd = _dot_nt(dys, w_ref[gi])
            e = dd / jnp.minimum(pos, float(win))
            d = 1
            while d < win:
                e = e + pltpu.roll(e, tr + POOL_HALO - d, 0)
                d *= 2
            dz_ref[:, cols] = (e[:tr] - dd[:tr]).astype(dz_ref.dtype)

    return pl.pallas_call(
        body, name="pool_bwd", grid=(nt,),
        in_specs=[pl.BlockSpec((tr, POOL_WIDTH), lambda i: (i, 0)),
                  pl.BlockSpec((POOL_HALO, POOL_WIDTH), lambda i: (jnp.maximum(i * hb - 1, 0), 0)),
                  pl.BlockSpec((tr, POOL_WIDTH), lambda i: (i, 0)),
                  pl.BlockSpec((POOL_HALO, POOL_WIDTH), lambda i: (jnp.minimum((i + 1) * hb, nt * hb - 1), 0)),
                  pl.BlockSpec((4, 128, 128), lambda i: (0, 0, 0)), pl.BlockSpec((1, POOL_WIDTH), lambda i: (0, 0))],
        out_specs=(pl.BlockSpec((tr, POOL_WIDTH), lambda i: (i, 0)), pl.BlockSpec((4, 128, 128), lambda i: (0, 0, 0)),
                   pl.BlockSpec((1, POOL_WIDTH), lambda i: (0, 0))),
        out_shape=(jax.ShapeDtypeStruct((S, POOL_WIDTH), BF16), jax.ShapeDtypeStruct((4, 128, 128), F32),
                   jax.ShapeDtypeStruct((1, POOL_WIDTH), F32)),
        compiler_params=_params(("arbitrary",)),
    )(z, z, dy, dy, w, scale)


def _log_sigmoid(x):
    return jnp.minimum(x, 0.0) - jnp.log(1.0 + jnp.exp(-jnp.abs(x)))


def _forget_cumsum(z, b_f):
    S = z.shape[0]
    tr = _div_tile(S, TILES["cum"], 8)
    zf_block = ZF // 128

    def body(z_ref, b_ref, f_ref, carry):
        @pl.when(pl.program_id(0) == 0)
        def _():
            carry[...] = jnp.zeros_like(carry)

        lf = _log_sigmoid(z_ref[...] + b_ref[...])
        tri = lax.broadcasted_iota(jnp.int32, (tr, tr), 1) <= lax.broadcasted_iota(jnp.int32, (tr, tr), 0)
        f_ref[...] = _dot3(tri, lf) + carry[...]
        carry[...] += _colsum(lf)

    return pl.pallas_call(
        body, name="forget_cumsum", grid=(S // tr,),
        in_specs=[pl.BlockSpec((tr, 128), lambda i: (i, zf_block)), pl.BlockSpec((1, 128), lambda i: (0, 0))],
        out_specs=pl.BlockSpec((tr, 128), lambda i: (i, 0)), out_shape=jax.ShapeDtypeStruct((S, 128), F32),
        scratch_shapes=[pltpu.VMEM((1, 128), F32)], compiler_params=_params(("arbitrary",)),
    )(z, b_f)


def _forget_cumsum_bwd(z, b_f, dF):
    S = z.shape[0]
    tr = _div_tile(S, TILES["cum"], 8)
    nt = S // tr
    zf_block = ZF // 128

    def body(z_ref, b_ref, df_ref, dz_ref, db_ref, carry):
        @pl.when(pl.program_id(0) == 0)
        def _():
            carry[...] = jnp.zeros_like(carry)
            db_ref[...] = jnp.zeros_like(db_ref)

        dF_ = df_ref[...]
        tri = lax.broadcasted_iota(jnp.int32, (tr, tr), 1) >= lax.broadcasted_iota(jnp.int32, (tr, tr), 0)
        dlf = _dot3(tri, dF_) + carry[...]
        carry[...] += _colsum(dF_)
        lane = lax.broadcasted_iota(jnp.int32, (tr, 128), 1)
        dzf = jnp.where(lane < N_HEADS, dlf * _sigmoid(-(z_ref[...] + b_ref[...])), 0.0)
        dz_ref[...] = dzf.astype(dz_ref.dtype)
        db_ref[...] += _colsum(dzf)

    return pl.pallas_call(
        body, name="forget_cumsum_bwd", grid=(nt,),
        in_specs=[pl.BlockSpec((tr, 128), lambda i: (nt - 1 - i, zf_block)), pl.BlockSpec((1, 128), lambda i: (0, 0)),
                  pl.BlockSpec((tr, 128), lambda i: (nt - 1 - i, 0))],
        out_specs=(pl.BlockSpec((tr, 128), lambda i: (nt - 1 - i, 0)), pl.BlockSpec((1, 128), lambda i: (0, 0))),
        out_shape=(jax.ShapeDtypeStruct((S, 128), BF16), jax.ShapeDtypeStruct((1, 128), F32)),
        scratch_shapes=[pltpu.VMEM((1, 128), F32)], compiler_params=_params(("arbitrary",)),
    )(z, b_f, dF)


NEG = -1e30
ATTN_SCALE = HEAD_DIM ** -0.5


def _on_block_kind(qi, kj, fn):
    @pl.when(qi == kj)
    def _():
        fn(True)

    @pl.when(qi != kj)
    def _():
        fn(False)


FIRST, LAST, HEAD_FIRST, HEAD_LAST, KEY_ZERO = 1, 2, 4, 8, 16


def _tri_schedule(n, by_key=False):
    outer, inner, flags = [], [], []
    for a in range(n):
        partners = list(range(a, n)) if by_key else list(range(a + 1))
        for idx, b in enumerate(partners):
            f = FIRST if idx == 0 else 0
            f |= LAST if idx == len(partners) - 1 else 0
            f |= KEY_ZERO if (a if by_key else b) == 0 else 0
            outer.append(a)
            inner.append(b)
            flags.append(f)
    flags[0] |= HEAD_FIRST
    flags[-1] |= HEAD_LAST
    return [jnp.asarray(np.array(v, np.int32)) for v in (outer, inner, flags)]


def _flash_call(body, name, sched, in_specs, out_specs, out_shape, scratch):
    grid_spec = pltpu.PrefetchScalarGridSpec(
        num_scalar_prefetch=3, grid=(N_HEADS, int(sched[0].shape[0])), in_specs=in_specs, out_specs=out_specs,
        scratch_shapes=scratch)
    return pl.pallas_call(body, name=name, grid_spec=grid_spec, out_shape=out_shape,
                          compiler_params=_params(("parallel", "arbitrary")))


def _scores_t(q, k, fq_row, fk_col, diagonal, floor):
    st = _dot_nt(k, q) * ATTN_SCALE + fq_row - fk_col
    if not diagonal:
        return st
    t = st.shape[0]
    return jnp.where(lax.broadcasted_iota(jnp.int32, (t, t), 0) <= lax.broadcasted_iota(jnp.int32, (t, t), 1), st, floor)


def _flash_fwd(z, v_t, f_col, f_row):
    S = z.shape[0]
    t = _div_tile(S, TILES["attn"], 128)
    sched = _tri_schedule(S // t)
    qb, kb = ZQ // 128, ZK // 128

    def body(qt, kt, ft, q_ref, k_ref, vt_ref, fq_ref, fk_ref, o_ref, lse_ref, m_sc, l_sc, acc_sc):
        step = pl.program_id(1)
        qi, kj, fl = qt[step], kt[step], ft[step]

        @pl.when((fl & FIRST) != 0)
        def _():
            m_sc[...] = jnp.full_like(m_sc, NEG)
            l_sc[...] = jnp.zeros_like(l_sc)
            acc_sc[...] = jnp.zeros_like(acc_sc)

        def update(diagonal):
            st = _scores_t(q_ref[...], k_ref[...], fq_ref[...], fk_ref[...], diagonal, NEG)
            m_new = jnp.maximum(m_sc[...], jnp.max(st, axis=0, keepdims=True))
            alpha = jnp.exp(m_sc[...] - m_new)
            pt = jnp.exp(st - m_new)
            l_sc[...] = alpha * l_sc[...] + jnp.sum(pt, axis=0, keepdims=True)
            acc_sc[...] = alpha * acc_sc[...] + _dot(vt_ref[...], pt)
            m_sc[...] = m_new

        _on_block_kind(qi, kj, update)

        @pl.when((fl & LAST) != 0)
        def _():
            o_ref[...] = acc_sc[...] / l_sc[...]
            lse_ref[...] = m_sc[...] + jnp.log(l_sc[...])

    row = pl.BlockSpec((None, 1, t), lambda h, s, qt, kt, ft: (h, 0, qt[s]))
    return _flash_call(
        body, "flash_fwd", sched,
        [pl.BlockSpec((t, 128), lambda h, s, qt, kt, ft: (qt[s], qb + h)),
         pl.BlockSpec((t, 128), lambda h, s, qt, kt, ft: (kt[s], kb + h)),
         pl.BlockSpec((128, t), lambda h, s, qt, kt, ft: (h, kt[s])), row,
         pl.BlockSpec((None, t, 1), lambda h, s, qt, kt, ft: (h, kt[s], 0))],
        (pl.BlockSpec((128, t), lambda h, s, qt, kt, ft: (h, qt[s])), row),
        (jax.ShapeDtypeStruct((ATTN_WIDTH, S), F32), jax.ShapeDtypeStruct((N_HEADS, 1, S), F32)),
        [pltpu.VMEM((1, t), F32), pltpu.VMEM((1, t), F32), pltpu.VMEM((128, t), F32)],
    )(*sched, z, z, v_t, f_row, f_col)


def _flash_bwd(z, k_t, dy, do_t, o_t, lse_row, f_col, f_row):
    S = z.shape[0]
    t = _div_tile(S, TILES["attn"], 128)
    n = S // t
    sched = _tri_schedule(n, by_key=True)
    qb, kb, vb = ZQ // 128, ZK // 128, ZV // 128
    dob = POOL_WIDTH // 128

    def body(kt, qt, ft, q_ref, k_ref, v_ref, kt_ref, do_ref, dot_ref, ot_ref, lse_ref, fq_ref, fk_ref,
             dq_ref, dk_ref, dv_ref, dfk_ref, dfq_ref, dk_sc, dv_sc, dfk_sc, dq_sc, dfq_sc, delta_sc):
        step = pl.program_id(1)
        kj, qi, fl = kt[step], qt[step], ft[step]

        @pl.when((fl & HEAD_FIRST) != 0)
        def _():
            dq_sc[...] = jnp.zeros_like(dq_sc)
            dfq_sc[...] = jnp.zeros_like(dfq_sc)

        @pl.when((fl & FIRST) != 0)
        def _():
            dk_sc[...] = jnp.zeros_like(dk_sc)
            dv_sc[...] = jnp.zeros_like(dv_sc)
            dfk_sc[...] = jnp.zeros_like(dfk_sc)

        @pl.when((fl & KEY_ZERO) != 0)
        def _():
            delta_sc[qi] = jnp.sum(dot_ref[...] * ot_ref[...], axis=0, keepdims=True)

        def update(diagonal):
            pt = jnp.exp(_scores_t(q_ref[...], k_ref[...], fq_ref[...], fk_ref[...], diagonal, NEG) - lse_ref[...])
            dv_sc[...] += _dot(pt, do_ref[...])
            dst = pt * (_dot_nt(v_ref[...], do_ref[...]) - delta_sc[qi])
            dk_sc[...] += _dot(dst, q_ref[...])
            dfk_sc[...] += jnp.sum(dst, axis=1, keepdims=True)
            dfq_sc[qi] = dfq_sc[qi] + jnp.sum(dst, axis=0, keepdims=True)
            dq_sc[qi] = dq_sc[qi] + _dot(kt_ref[...], dst)

        _on_block_kind(qi, kj, update)

        @pl.when((fl & LAST) != 0)
        def _():
            dk_ref[...] = (dk_sc[...] * ATTN_SCALE).astype(dk_ref.dtype)
            dv_ref[...] = dv_sc[...].astype(dv_ref.dtype)
            dfk_ref[...] = -dfk_sc[...]

        @pl.when((fl & HEAD_LAST) != 0)
        def _():
            for i in range(n):
                dq_ref[:, i * t:(i + 1) * t] = (dq_sc[i] * ATTN_SCALE).astype(dq_ref.dtype)
                dfq_ref[:, i * t:(i + 1) * t] = dfq_sc[i]

    def qs(block):
        return pl.BlockSpec((t, 128), lambda h, s, kt, qt, ft: (qt[s], block + h))

    def kv(block):
        return pl.BlockSpec((t, 128), lambda h, s, kt, qt, ft: (kt[s], block + h))

    first_sweep = pl.BlockSpec((128, t), lambda h, s, kt, qt, ft: (h, jnp.where(kt[s] == 0, qt[s], n - 1)))
    qrow = pl.BlockSpec((None, 1, t), lambda h, s, kt, qt, ft: (h, 0, qt[s]))
    kcol = pl.BlockSpec((None, t, 1), lambda h, s, kt, qt, ft: (h, kt[s], 0))
    out = pl.BlockSpec((t, 128), lambda h, s, kt, qt, ft: (kt[s], h))
    return _flash_call(
        body, "flash_bwd", sched,
        [qs(qb), kv(kb), kv(vb), pl.BlockSpec((128, t), lambda h, s, kt, qt, ft: (h, kt[s])), qs(dob), first_sweep,
         first_sweep, qrow, qrow, kcol],
        (pl.BlockSpec((128, S), lambda h, s, kt, qt, ft: (h, 0)), out, out, kcol,
         pl.BlockSpec((None, 1, S), lambda h, s, kt, qt, ft: (h, 0, 0))),
        (jax.ShapeDtypeStruct((ATTN_WIDTH, S), BF16), jax.ShapeDtypeStruct((S, ATTN_WIDTH), BF16),
         jax.ShapeDtypeStruct((S, ATTN_WIDTH), BF16), jax.ShapeDtypeStruct((N_HEADS, S, 1), F32),
         jax.ShapeDtypeStruct((N_HEADS, 1, S), F32)),
        [pltpu.VMEM((t, 128), F32), pltpu.VMEM((t, 128), F32), pltpu.VMEM((t, 1), F32), pltpu.VMEM((n, 128, t), F32),
         pltpu.VMEM((n, 1, t), F32), pltpu.VMEM((n, 1, t), F32)],
    )(*sched, z, z, z, k_t, dy, do_t, o_t, lse_row, f_row, f_col)


LRU_HALO = 8


def _lru_gates(ext, cw_ref, cb_ref, wa_ref, ba_ref, wi_ref, bi_ref, lam_ref, tr):
    taps = [pltpu.roll(ext, 3 - k, 0)[LRU_HALO:] if k < 3 else ext[LRU_HALO:] for k in range(4)]
    xc = cb_ref[...] + taps[0] * cw_ref[0:1, :]
    for k in range(1, 4):
        xc = xc + taps[k] * cw_ref[k:k + 1, :]
    ga = jnp.concatenate([_dot(xc[:, g * 128:(g + 1) * 128], wa_ref[g]) for g in range(4)], axis=1) + ba_ref[...]
    gi = jnp.concatenate([_dot(xc[:, g * 128:(g + 1) * 128], wi_ref[g]) for g in range(4)], axis=1) + bi_ref[...]
    r, ig = _sigmoid(ga), _sigmoid(gi)
    nl = -lam_ref[...]
    sp = jnp.maximum(nl, 0.0) + jnp.log(1.0 + jnp.exp(-jnp.abs(nl)))
    la = -LRU_C * r * sp
    a = jnp.exp(la)
    mult = jnp.sqrt(_neg_expm1(2.0 * la))
    return xc, r, ig, sp, a, mult, taps


def _lru_specs(tr, nt, rev):
    hb = tr // LRU_HALO
    ti = (lambda i: nt - 1 - i) if rev else (lambda i: i)
    zx_b, zy_b = ZX // LRU_WIDTH, ZY // LRU_WIDTH
    cur = lambda b: pl.BlockSpec((tr, LRU_WIDTH), lambda i: (ti(i), b))
    prev = lambda b: pl.BlockSpec((LRU_HALO, LRU_WIDTH), lambda i: (jnp.maximum(ti(i) * hb - 1, 0), b))
    vec = pl.BlockSpec((1, LRU_WIDTH), lambda i: (0, 0))
    cw = pl.BlockSpec((4, LRU_WIDTH), lambda i: (0, 0))
    blk = pl.BlockSpec((4, 128, 128), lambda i: (0, 0, 0))
    return ti, cur, prev, vec, cw, blk, zx_b, zy_b


def _lru_fwd(z, cw, cb, wa, ba, wi, bi, lam):
    S = z.shape[0]
    tr = _div_tile(S, TILES["lru"], 8)
    nt = S // tr
    ti, cur, prev, vec, cwspec, blk, zx_b, zy_b = _lru_specs(tr, nt, False)

    def body(zx_ref, halo_ref, zy_ref, cw_ref, cb_ref, wa_ref, ba_ref, wi_ref, bi_ref, lam_ref, y_ref, h_ref, carry):
        i = pl.program_id(0)

        @pl.when(i == 0)
        def _():
            carry[...] = jnp.zeros_like(carry)

        ext = jnp.concatenate([jnp.where(i > 0, halo_ref[...], 0.0), zx_ref[...]], axis=0)
        xc, r, ig, sp, a, mult, _ = _lru_gates(ext, cw_ref, cb_ref, wa_ref, ba_ref, wi_ref, bi_ref, lam_ref, tr)
        A, B = a, mult * (ig * xc)
        row = _rows(tr, LRU_WIDTH)
        d = 1
        while d < tr:
            a_sh = jnp.where(row >= d, pltpu.roll(A, d, 0), 1.0)
            b_sh = jnp.where(row >= d, pltpu.roll(B, d, 0), 0.0)
            B = A * b_sh + B
            A = A * a_sh
            d *= 2
        h = B + A * carry[...]
        h_ref[...] = h
        carry[...] = h_ref[pl.ds(tr - 1, 1), :]
        y_ref[...] = h * _gelu_parts(zy_ref[...])[0]

    out = pl.BlockSpec((tr, LRU_WIDTH), lambda i: (i, 0))
    shape = jax.ShapeDtypeStruct((S, LRU_WIDTH), F32)
    return pl.pallas_call(
        body, name="lru_fwd", grid=(nt,),
        in_specs=[cur(zx_b), prev(zx_b), cur(zy_b), cwspec, vec, blk, vec, blk, vec, vec],
        out_specs=(out, out), out_shape=(shape, shape), scratch_shapes=[pltpu.VMEM((1, LRU_WIDTH), F32)],
        compiler_params=_params(("arbitrary",)),
    )(z, z, z, cw, cb, wa, ba, wi, bi, lam)


def _lru_bwd(z, dy, hs, cw, cb, wa, ba, wi, bi, lam):
    S = z.shape[0]
    tr = _div_tile(S, TILES["lru"], 8)
    nt = S // tr
    ti, cur, prev, vec, cwspec, blk, zx_b, zy_b = _lru_specs(tr, nt, True)
    dy_b = (POOL_WIDTH + ATTN_WIDTH) // LRU_WIDTH

    def body(zx_ref, halo_ref, zy_ref, dy_ref, h_ref, hprev_ref, cw_ref, cb_ref, wa_ref, ba_ref, wi_ref, bi_ref, lam_ref,
             dzx_ref, dzy_ref, dcw_ref, dcb_ref, dwa_ref, dba_ref, dwi_ref, dbi_ref, dlam_ref, gcarry, dxc_next, tmp):
        i = pl.program_id(0)
        t_idx = nt - 1 - i

        @pl.when(i == 0)
        def _():
            gcarry[...] = jnp.zeros_like(gcarry)
            dxc_next[...] = jnp.zeros_like(dxc_next)
            for ref in (dcw_ref, dcb_ref, dwa_ref, dba_ref, dwi_ref, dbi_ref, dlam_ref):
                ref[...] = jnp.zeros_like(ref)

        ext = jnp.concatenate([jnp.where(t_idx > 0, halo_ref[...], 0.0), zx_ref[...]], axis=0)
        xc, r, ig, sp, a, mult, taps = _lru_gates(ext, cw_ref, cb_ref, wa_ref, ba_ref, wi_ref, bi_ref, lam_ref, tr)
        h = h_ref[...]
        gel, dgel = _gelu_parts(zy_ref[...])
        dy_ = dy_ref[...]
        dzy_ref[...] = (dy_ * h * dgel).astype(dzy_ref.dtype)
        row = _rows(tr, LRU_WIDTH)
        B = dy_ * gel + jnp.where(row == tr - 1, gcarry[...], 0.0)
        A = jnp.where(row < tr - 1, pltpu.roll(a, tr - 1, 0), 0.0)
        d = 1
        while d < tr:
            keep = row < tr - d
            b_sh = jnp.where(keep, pltpu.roll(B, tr - d, 0), 0.0)
            a_sh = jnp.where(keep, pltpu.roll(A, tr - d, 0), 0.0)
            B = B + A * b_sh
            A = A * a_sh
            d *= 2
        g = B
        tmp[...] = a * g
        gcarry[...] = tmp[pl.ds(0, 1), :]
        h_ext = jnp.concatenate([jnp.where(t_idx > 0, hprev_ref[...], 0.0), h], axis=0)
        hprev = pltpu.roll(h_ext, 1, 0)[LRU_HALO:]
        t1 = g * mult
        dig = t1 * xc
        dxc = t1 * ig
        dla = (g * hprev) * a - (g * (ig * xc)) * (a * a) / mult
        dr = dla * (-LRU_C * sp)
        dga = dr * r * (1.0 - r)
        dgi = dig * ig * (1.0 - ig)
        dlam_ref[...] += _colsum(dla * (-LRU_C * r)) * (-_sigmoid(-lam_ref[...]))
        dba_ref[...] += _colsum(dga)
        dbi_ref[...] += _colsum(dgi)
        parts = []
        for gidx in range(4):
            cols = slice(gidx * 128, (gidx + 1) * 128)
            xct = xc[:, cols].T
            dwa_ref[gidx] += _dot(xct, dga[:, cols])
            dwi_ref[gidx] += _dot(xct, dgi[:, cols])
            parts.append(_dot_nt(dga[:, cols], wa_ref[gidx]) + _dot_nt(dgi[:, cols], wi_ref[gidx]))
        dxc = dxc + jnp.concatenate(parts, axis=1)
        dcb_ref[...] += _colsum(dxc)
        for k in range(4):
            dcw_ref[k:k + 1, :] += _colsum(dxc * taps[k])
        ext_d = jnp.concatenate([dxc, dxc_next[...]], axis=0)
        dzx = dxc * cw_ref[3:4, :]
        for k in range(3):
            dzx = dzx + pltpu.roll(ext_d, tr + LRU_HALO - (3 - k), 0)[:tr] * cw_ref[k:k + 1, :]
        dzx_ref[...] = dzx.astype(dzx_ref.dtype)
        dxc_next[...] = dxc[:LRU_HALO]

    rev = pl.BlockSpec((tr, LRU_WIDTH), lambda i: (nt - 1 - i, 0))
    hb = tr // LRU_HALO
    hprev_spec = pl.BlockSpec((LRU_HALO, LRU_WIDTH), lambda i: (jnp.maximum((nt - 1 - i) * hb - 1, 0), 0))
    dy_spec = pl.BlockSpec((tr, LRU_WIDTH), lambda i: (nt - 1 - i, dy_b))
    vshape = jax.ShapeDtypeStruct((1, LRU_WIDTH), F32)
    bshape = jax.ShapeDtypeStruct((4, 128, 128), F32)
    return pl.pallas_call(
        body, name="lru_bwd", grid=(nt,),
        in_specs=[cur(zx_b), prev(zx_b), cur(zy_b), dy_spec, rev, hprev_spec, cwspec, vec, blk, vec, blk, vec, vec],
        out_specs=(rev, rev, cwspec, vec, blk, vec, blk, vec, vec),
        out_shape=(jax.ShapeDtypeStruct((S, LRU_WIDTH), BF16), jax.ShapeDtypeStruct((S, LRU_WIDTH), BF16),
                   jax.ShapeDtypeStruct((4, LRU_WIDTH), F32), vshape, bshape, vshape, bshape, vshape, vshape),
        scratch_shapes=[pltpu.VMEM((1, LRU_WIDTH), F32), pltpu.VMEM((LRU_HALO, LRU_WIDTH), F32),
                        pltpu.VMEM((tr, LRU_WIDTH), F32)],
        compiler_params=_params(("arbitrary",)),
    )(z, z, z, dy, hs, hs, cw, cb, wa, ba, wi, bi, lam)


FFN_HALO = 8


def _ffn_act(au, cw, cb):
    S, F2 = au.shape
    F = F2 // 2
    tc = FFN_CHUNK
    tr = _div_tile(S, TILES["row"], 8)
    hb = tr // FFN_HALO

    def body(au_ref, halo_ref, cw_ref, cb_ref, p_ref):
        i = pl.program_id(0)
        a_ = au_ref[:, :tc]
        ext = jnp.concatenate([jnp.where(i > 0, halo_ref[:, :tc], 0.0), a_], axis=0)
        gc = cb_ref[...] + a_ * cw_ref[2:3, :]
        for k in range(2):
            gc = gc + pltpu.roll(ext, 2 - k, 0)[FFN_HALO:] * cw_ref[k:k + 1, :]
        p_ref[...] = (gc * _sigmoid(gc) * au_ref[:, tc:]).astype(p_ref.dtype)

    return pl.pallas_call(
        body, name="ffn_act", grid=(S // tr, F // tc),
        in_specs=[pl.BlockSpec((tr, 2 * tc), lambda i, j: (i, j)),
                  pl.BlockSpec((FFN_HALO, 2 * tc), lambda i, j: (jnp.maximum(i * hb - 1, 0), j)),
                  pl.BlockSpec((3, tc), lambda i, j: (0, j)), pl.BlockSpec((1, tc), lambda i, j: (0, j))],
        out_specs=pl.BlockSpec((tr, tc), lambda i, j: (i, j)), out_shape=jax.ShapeDtypeStruct((S, F), BF16),
        compiler_params=_params(("parallel", "parallel")),
    )(au, au, cw, cb)


def _ffn_act_bwd(au, dp, cw, cb):
    S, F2 = au.shape
    F = F2 // 2
    tc = FFN_CHUNK
    tr = _div_tile(S, TILES["ew"], 8)
    hb = tr // FFN_HALO
    nt = S // tr
    H = FFN_HALO

    def body(au_ref, prev_ref, next_ref, dp_ref, dpn_ref, cw_ref, cb_ref, dau_ref, dcw_ref, dcb_ref):
        i = pl.program_id(1)
        last = i == nt - 1
        a_ext = jnp.concatenate([jnp.where(i > 0, prev_ref[:, :tc], 0.0), au_ref[:, :tc], next_ref[:, :tc]], axis=0)
        u_ext = jnp.concatenate([au_ref[:, tc:], next_ref[:, tc:]], axis=0)
        dp_ext = jnp.concatenate([dp_ref[...], jnp.where(last, 0.0, dpn_ref[...])], axis=0)
        taps = [pltpu.roll(a_ext, 2 - k, 0)[H:] if k < 2 else a_ext[H:] for k in range(3)]
        gc = cb_ref[...] + taps[0] * cw_ref[0:1, :] + taps[1] * cw_ref[1:2, :] + taps[2] * cw_ref[2:3, :]
        sig = _sigmoid(gc)
        dgc = dp_ext * u_ext * (sig * (1.0 + gc * (1.0 - sig)))
        da = dgc[:tr] * cw_ref[2:3, :]
        for k in range(2):
            da = da + pltpu.roll(dgc, tr + H - (2 - k), 0)[:tr] * cw_ref[k:k + 1, :]
        dau_ref[:, :tc] = da.astype(dau_ref.dtype)
        dau_ref[:, tc:] = (dp_ref[...] * (gc[:tr] * sig[:tr])).astype(dau_ref.dtype)

        @pl.when(i == 0)
        def _():
            dcw_ref[...] = jnp.zeros_like(dcw_ref)
            dcb_ref[...] = jnp.zeros_like(dcb_ref)

        dcb_ref[...] += _colsum(dgc[:tr])
        for k in range(3):
            dcw_ref[k:k + 1, :] += _colsum(dgc[:tr] * taps[k][:tr])

    return pl.pallas_call(
        body, name="ffn_act_bwd", grid=(F // tc, nt),
        in_specs=[pl.BlockSpec((tr, 2 * tc), lambda j, i: (i, j)),
                  pl.BlockSpec((H, 2 * tc), lambda j, i: (jnp.maximum(i * hb - 1, 0), j)),
                  pl.BlockSpec((H, 2 * tc), lambda j, i: (jnp.minimum((i + 1) * hb, nt * hb - 1), j)),
                  pl.BlockSpec((tr, tc), lambda j, i: (i, j)),
                  pl.BlockSpec((H, tc), lambda j, i: (jnp.minimum((i + 1) * hb, nt * hb - 1), j)),
                  pl.BlockSpec((3, tc), lambda j, i: (0, j)), pl.BlockSpec((1, tc), lambda j, i: (0, j))],
        out_specs=(pl.BlockSpec((tr, 2 * tc), lambda j, i: (i, j)), pl.BlockSpec((3, tc), lambda j, i: (0, j)),
                   pl.BlockSpec((1, tc), lambda j, i: (0, j))),
        out_shape=(jax.ShapeDtypeStruct((S, F2), BF16), jax.ShapeDtypeStruct((3, F), F32), jax.ShapeDtypeStruct((1, F), F32)),
        compiler_params=_params(("parallel", "arbitrary")),
    )(au, au, au, dp, dp, cw, cb)


def _row_tile(rows, cols):
    return _div_tile(rows, max(16, (2**18 // cols) // 16 * 16), 16)


def _sum_parts(parts, sel, rows, cols, name):
    tr = _row_tile(rows, cols)

    def body(sel_ref, *refs):
        acc = refs[0][...].astype(F32)
        for r in refs[1:-1]:
            acc = acc + r[...].astype(F32)
        refs[-1][...] = acc

    def spec(index):
        if isinstance(index, int):
            return pl.BlockSpec((None, tr, cols), lambda i, s: (index, i, 0))
        k, mul, off = index
        return pl.BlockSpec((None, tr, cols), lambda i, s: (s[k] * mul + off, i, 0))

    grid_spec = pltpu.PrefetchScalarGridSpec(
        num_scalar_prefetch=1, grid=(rows // tr,), in_specs=[spec(ix) for _, ix in parts],
        out_specs=pl.BlockSpec((tr, cols), lambda i, s: (i, 0)))
    return pl.pallas_call(
        body, name=name, grid_spec=grid_spec, out_shape=jax.ShapeDtypeStruct((rows, cols), F32),
        compiler_params=_params(("parallel",)),
    )(sel, *[a for a, _ in parts])


def _pair_sum(g, got, sel):
    n, rows, cols = got.shape
    tr = _row_tile(rows, cols)
    nb = rows // tr

    def body(sel_ref, a_ref, b_ref, o_ref):
        o_ref[...] = (a_ref[...] + b_ref[...]).astype(o_ref.dtype)

    grid_spec = pltpu.PrefetchScalarGridSpec(
        num_scalar_prefetch=1, grid=(n, nb),
        in_specs=[pl.BlockSpec((None, tr, cols), lambda q, i, s: (q, s[0] * nb + i, 0)),
                  pl.BlockSpec((None, tr, cols), lambda q, i, s: (q, i, 0))],
        out_specs=pl.BlockSpec((None, tr, cols), lambda q, i, s: (q, i, 0)))
    return pl.pallas_call(
        body, name="pair_sum", grid_spec=grid_spec, out_shape=jax.ShapeDtypeStruct((n, rows, cols), BF16),
        compiler_params=_params(("parallel", "parallel")),
    )(sel, g, got)


def _chip_sum_cols(pair, arrived, sel):
    n, rows, cg = arrived.shape[1:]
    tr = _row_tile(rows, cg)

    def body(sel_ref, p_ref, a0, a1, a2, o_ref):
        o_ref[...] = ((p_ref[...].astype(F32) + a0[...].astype(F32)) + a1[...].astype(F32)) + a2[...].astype(F32)

    def arr(j):
        return pl.BlockSpec((None, None, tr, cg), lambda k, i, s: (j, k, i, 0))

    grid_spec = pltpu.PrefetchScalarGridSpec(
        num_scalar_prefetch=1, grid=(n, rows // tr),
        in_specs=[pl.BlockSpec((None, tr, cg), lambda k, i, s: (k, i, s[1])), arr(0), arr(1), arr(2)],
        out_specs=pl.BlockSpec((None, tr, cg), lambda k, i, s: (k, i, 0)))
    return pl.pallas_call(
        body, name="chip_sum_cols", grid_spec=grid_spec, out_shape=jax.ShapeDtypeStruct((n, rows, cg), F32),
        compiler_params=_params(("parallel", "parallel")),
    )(sel, pair, arrived, arrived, arrived)


def _adamw_math(w, g, m, v):
    m2 = ADAM_B1 * m + (1.0 - ADAM_B1) * g
    v2 = ADAM_B2 * v + (1.0 - ADAM_B2) * (g * g)
    m_hat = m2 / (1.0 - ADAM_B1 ** ADAM_STEP)
    v_hat = v2 / (1.0 - ADAM_B2 ** ADAM_STEP)
    return -ADAM_LR * (m_hat / (jnp.sqrt(v_hat) + ADAM_EPS) + ADAM_WD * w), m2, v2


def _adamw(w, g, m, v):
    R, C = w.shape
    tr = _div_tile(R, TILES["ew"], 8)

    def body(w_ref, g_ref, m_ref, v_ref, d_ref, m2_ref, v2_ref):
        d_ref[...], m2_ref[...], v2_ref[...] = _adamw_math(w_ref[...], g_ref[...], m_ref[...], v_ref[...])

    spec = pl.BlockSpec((tr, C), lambda i: (i, 0))
    shape = jax.ShapeDtypeStruct((R, C), F32)
    return pl.pallas_call(
        body, name="adamw", grid=(R // tr,), in_specs=[spec] * 4, out_specs=(spec,) * 3, out_shape=(shape,) * 3,
        compiler_params=_params(("parallel",)),
    )(w, g, m, v)


def _ada_grad_adamw(cact_t, dmod, w, m, v):
    L, D, N = w.shape
    tm, tn = _div_tile(D, 256, 8), _div_tile(N, 1024, 128)

    def body(c_ref, d_ref, w_ref, m_ref, v_ref, g_ref, dl_ref, m2_ref, v2_ref):
        g = c_ref[:, 0:1] * d_ref[0:1, :]
        for b in range(1, N_DEV):
            g = g + c_ref[:, b:b + 1] * d_ref[b:b + 1, :]
        g_ref[...] = g
        dl_ref[...], m2_ref[...], v2_ref[...] = _adamw_math(w_ref[...], g, m_ref[...], v_ref[...])

    big = pl.BlockSpec((None, tm, tn), lambda l, i, j: (l, i, j))
    shape = jax.ShapeDtypeStruct((L, D, N), F32)
    return pl.pallas_call(
        body, name="ada_grad_adamw", grid=(L, D // tm, N // tn),
        in_specs=[pl.BlockSpec((tm, N_DEV), lambda l, i, j: (i, 0)),
                  pl.BlockSpec((None, N_DEV, tn), lambda l, i, j: (l, 0, j)), big, big, big],
        out_specs=(big,) * 4, out_shape=(shape,) * 4, compiler_params=_params(("parallel", "parallel", "parallel")),
    )(cact_t, dmod, w, m, v)


def _silu_rows(c):
    def body(c_ref, o_ref):
        x = c_ref[...]
        o_ref[...] = x * _sigmoid(x)

    return pl.pallas_call(body, name="silu_rows", out_shape=jax.ShapeDtypeStruct(c.shape, F32))(c)


ANY = pl.BlockSpec(memory_space=pl.ANY)


def _position():
    return lax.axis_index("x"), lax.axis_index("y"), lax.axis_index("c")


def _other_chips(x, y):
    return [(1 - x, y), (x, 1 - y), (1 - x, 1 - y)]


def _allgather8(v):
    R, C = v.shape

    def body(v_ref, out_ref, send_sems, recv_sems, local_sem):
        x, y, c = _position()
        me = 4 * x + 2 * y + c
        mine = pltpu.make_async_copy(v_ref, out_ref.at[me], local_sem)
        mine.start()
        sends, recvs = [], []
        for k in range(1, N_DEV):
            px, py, pc = (x + (k >> 2)) % 2, (y + ((k >> 1) & 1)) % 2, (c + (k & 1)) % 2
            sends.append(pltpu.make_async_remote_copy(
                src_ref=v_ref, dst_ref=out_ref.at[me], send_sem=send_sems.at[k - 1], recv_sem=recv_sems.at[k - 1],
                device_id=(px, py, pc), device_id_type=MESH))
            recvs.append(pltpu.make_async_remote_copy(
                src_ref=v_ref, dst_ref=out_ref.at[4 * px + 2 * py + pc], send_sem=send_sems.at[k - 1],
                recv_sem=recv_sems.at[k - 1], device_id=(px, py, pc), device_id_type=MESH))
        for cp in sends:
            cp.start()
        for cp in recvs:
            cp.wait_recv()
        for cp in sends:
            cp.wait_send()
        mine.wait()

    return pl.pallas_call(
        body, name="comm_allgather8", out_shape=jax.ShapeDtypeStruct((N_DEV, R, C), v.dtype), in_specs=[ANY],
        out_specs=ANY,
        scratch_shapes=[pltpu.SemaphoreType.DMA((N_DEV - 1,)), pltpu.SemaphoreType.DMA((N_DEV - 1,)),
                        pltpu.SemaphoreType.DMA],
    )(v)


def _remote(src, dst, send_sems, recv_sems, k, to):
    return pltpu.make_async_remote_copy(src_ref=src, dst_ref=dst, send_sem=send_sems.at[k], recv_sem=recv_sems.at[k],
                                        device_id=to, device_id_type=MESH)


def _half_rows(ref, h):
    n = ref.shape[0] // 2
    return ref.at[pl.ds(h * n, n)]


def _gather_weights(l, win_s, wout_s, gate_s, up_s, down_s):
    D, CI = win_s.shape[1:]
    CG = gate_s.shape[2]
    n_t = 5

    def body(win, wout, gate, up, down, win4, wout4, gu, down4, send_sems, recv_sems, local_sems):
        x, y, c = _position()
        q = 2 * x + y
        chips = _other_chips(x, y)
        sibling = (x, y, 1 - c)
        tensors = [(win.at[l], lambda p: win4.at[p]), (wout.at[l], lambda p: wout4.at[p]),
                   (gate.at[l], lambda p: gu.at[0, :, pl.ds(p * CG, CG)]),
                   (up.at[l], lambda p: gu.at[1, :, pl.ds(p * CG, CG)]), (down.at[l], lambda p: down4.at[p])]
        own, first, passed = [], [], []
        for t, (src, dst) in enumerate(tensors):
            cp = pltpu.make_async_copy(src, dst(q), local_sems.at[t])
            cp.start()
            own.append(cp)
            for j, (px, py) in enumerate(chips):
                cp = _remote(_half_rows(src, c), _half_rows(dst(q), c), send_sems, recv_sems, 6 * t + j, (px, py, c))
                cp.start()
                first.append(cp)
        for t, (src, dst) in enumerate(tensors):
            for j, (px, py) in enumerate(chips):
                landed = _half_rows(dst(2 * px + py), c)
                _remote(landed, landed, send_sems, recv_sems, 6 * t + j, (px, py, c)).wait_recv()
                cp = _remote(landed, landed, send_sems, recv_sems, 6 * t + 3 + j, sibling)
                cp.start()
                passed.append(cp)
        for t, (src, dst) in enumerate(tensors):
            for j, (px, py) in enumerate(chips):
                landed = _half_rows(dst(2 * px + py), 1 - c)
                _remote(landed, landed, send_sems, recv_sems, 6 * t + 3 + j, sibling).wait_recv()
        for cp in first + passed:
            cp.wait_send()
        for cp in own:
            cp.wait()

    shapes = (jax.ShapeDtypeStruct((N_CHIPS, D, CI), BF16), jax.ShapeDtypeStruct((N_CHIPS,) + wout_s.shape[1:], BF16),
              jax.ShapeDtypeStruct((2, D, N_CHIPS * CG), BF16), jax.ShapeDtypeStruct((N_CHIPS,) + down_s.shape[1:], BF16))
    return pl.pallas_call(
        body, name="comm_gather_weights", out_shape=shapes, in_specs=[ANY] * n_t, out_specs=(ANY,) * 4,
        scratch_shapes=[pltpu.SemaphoreType.DMA((6 * n_t,)), pltpu.SemaphoreType.DMA((6 * n_t,)),
                        pltpu.SemaphoreType.DMA((n_t,))],
    )(win_s, wout_s, gate_s, up_s, down_s)


def _sibling_swap_halves(gs):
    n_t = len(gs)

    def body(*refs):
        ins, outs, (send_sems, recv_sems) = refs[:n_t], refs[n_t:2 * n_t], refs[2 * n_t:]
        x, y, c = _position()
        cps = []
        for t in range(n_t):
            half = ins[t].shape[1] // 2
            cps.append(_remote(ins[t].at[:, pl.ds((1 - c) * half, half), :], outs[t], send_sems, recv_sems, t,
                               (x, y, 1 - c)))
        for cp in cps:
            cp.start()
        for cp in cps:
            cp.wait()

    shapes = tuple(jax.ShapeDtypeStruct((g.shape[0], g.shape[1] // 2, g.shape[2]), g.dtype) for g in gs)
    return pl.pallas_call(
        body, name="comm_sibling_swap", out_shape=shapes, in_specs=[ANY] * n_t, out_specs=(ANY,) * n_t,
        scratch_shapes=[pltpu.SemaphoreType.DMA((n_t,)), pltpu.SemaphoreType.DMA((n_t,))],
    )(*gs)


def _chip_scatter(leads, gu):
    n_l = len(leads)
    CG = gu.shape[2] // N_CHIPS
    per = n_l + 2

    def body(*refs):
        ins, gu_ref, outs, gu_out = refs[:n_l], refs[n_l], refs[n_l + 1:2 * n_l + 1], refs[2 * n_l + 1]
        send_sems, recv_sems = refs[2 * n_l + 2:]
        x, y, c = _position()
        cps = []
        for j, (px, py) in enumerate(_other_chips(x, y)):
            p = 2 * px + py
            for t in range(n_l):
                cps.append(_remote(ins[t].at[p], outs[t].at[j], send_sems, recv_sems, per * j + t, (px, py, c)))
            for k in range(2):
                cps.append(_remote(gu_ref.at[k, :, pl.ds(p * CG, CG)], gu_out.at[j, k], send_sems, recv_sems,
                                   per * j + n_l + k, (px, py, c)))
        for cp in cps:
            cp.start()
        for cp in cps:
            cp.wait()

    shapes = tuple(jax.ShapeDtypeStruct((3,) + g.shape[1:], g.dtype) for g in leads)
    shapes += (jax.ShapeDtypeStruct((3, 2, gu.shape[1], CG), gu.dtype),)
    return pl.pallas_call(
        body, name="comm_chip_scatter", out_shape=shapes, in_specs=[ANY] * (n_l + 1), out_specs=(ANY,) * (n_l + 1),
        scratch_shapes=[pltpu.SemaphoreType.DMA((3 * per,)), pltpu.SemaphoreType.DMA((3 * per,))],
    )(*leads, gu)


def _sibling_allgather(vs):
    n_t = len(vs)

    def body(*refs):
        ins, outs, (send_sems, recv_sems, local_sems) = refs[:n_t], refs[n_t:2 * n_t], refs[2 * n_t:]
        x, y, c = _position()
        cps, own, recvs = [], [], []
        for t in range(n_t):
            R = ins[t].shape[-2]
            rows = lambda h: outs[t].at[:, pl.ds(h * R, R), :] if len(ins[t].shape) == 3 else outs[t].at[pl.ds(h * R, R)]
            own.append(pltpu.make_async_copy(ins[t], rows(c), local_sems.at[t]))
            cps.append(_remote(ins[t], rows(c), send_sems, recv_sems, t, (x, y, 1 - c)))
            recvs.append(_remote(ins[t], rows(1 - c), send_sems, recv_sems, t, (x, y, 1 - c)))
        for cp in own + cps:
            cp.start()
        for cp in recvs:
            cp.wait_recv()
        for cp in cps:
            cp.wait_send()
        for cp in own:
            cp.wait()

    shapes = tuple(jax.ShapeDtypeStruct(v.shape[:-2] + (2 * v.shape[-2], v.shape[-1]), v.dtype) for v in vs)
    return pl.pallas_call(
        body, name="comm_sibling_allgather", out_shape=shapes, in_specs=[ANY] * n_t, out_specs=(ANY,) * n_t,
        scratch_shapes=[pltpu.SemaphoreType.DMA((n_t,)), pltpu.SemaphoreType.DMA((n_t,)), pltpu.SemaphoreType.DMA((n_t,))],
    )(*vs)


def _reduce_scatter(leads, gu, c_idx, q_idx):
    sel = jnp.stack([c_idx, q_idx]).astype(jnp.int32)
    got = _sibling_swap_halves(list(leads) + [gu])
    pairs = [_pair_sum(g, r, sel) for g, r in zip(list(leads) + [gu], got)]
    arrived = _chip_scatter(pairs[:-1], pairs[-1])
    mine = [_sum_parts([(p, (1, 1, 0)), (a, 0), (a, 1), (a, 2)], sel, p.shape[1], p.shape[2], "chip_sum")
            for p, a in zip(pairs[:-1], arrived[:-1])]
    mine.append(_chip_sum_cols(pairs[-1], arrived[-1], sel))
    return _sibling_allgather(mine)


def _layer_fwd(x, mod, p):
    sh1, sc1, gt1, sh2, sc2, gt2 = mod
    h1 = _norm_mod(x, p["g_mix"], sc1, sh1)
    z = _matmul(h1, p["w_in"], name="mm_in")
    y_pool = _pool_fwd(z, p["pool_w"], p["pool_scale"])
    F = _forget_cumsum(z, p["b_f"])
    Fh = F[:, :N_HEADS].T
    f_col, f_row = Fh[:, :, None], Fh[:, None, :]
    o_t, lse = _flash_fwd(z, z[:, ZV:ZV + ATTN_WIDTH].T.astype(BF16), f_col, f_row)
    y_lru, hs = _lru_fwd(z, p["lru_conv_w"], p["lru_conv_b"], p["lru_wa"], p["lru_ba"], p["lru_wi"], p["lru_bi"],
                         p["lru_lambda"])
    y = jnp.concatenate([y_pool.astype(BF16), o_t.T.astype(BF16), y_lru.astype(BF16)], axis=1)
    m1, x_mid = _matmul(y, p["w_out"], res=x, gate=gt1, name="mm_out")
    h2 = _norm_mod(x_mid, p["g_ffn"], sc2, sh2)
    au = _matmul(h2, p["w_gu"], gu="b", name="mm_gu")
    pa = _ffn_act(au, p["ffn_conv_w"], p["ffn_conv_b"])
    m2, x_out = _matmul(pa, p["w_down"], res=x_mid, gate=gt2, name="mm_down")
    saved = dict(x=x, h1=h1, z=z, f_col=f_col, f_row=f_row, o_t=o_t, lse=lse, hs=hs, y=y, m1=m1, x_mid=x_mid, h2=h2, au=au,
                 pa=pa, m2=m2)
    return x_out, saved


def _layer_bwd(dx_out, mod, p, s):
    sh1, sc1, gt1, sh2, sc2, gt2 = mod
    g = {}
    dm2, dgt2 = _gate_bwd(dx_out, s["m2"], gt2)
    dpa = _matmul(dm2, p["w_down"], nt=True, name="mm_down_dx")
    g["w_down"] = _matmul(s["pa"].T, dm2, name="mm_down_dw")
    dau, g["ffn_conv_w"], g["ffn_conv_b"] = _ffn_act_bwd(s["au"], dpa, p["ffn_conv_w"], p["ffn_conv_b"])
    dh2 = _matmul(dau, p["w_gu"], nt=True, gu="b", name="mm_gu_dx")
    g["w_gu"] = _matmul(s["h2"].T, dau, gu="out", name="mm_gu_dw")
    dx_mid, g["g_ffn"], dsc2, dsh2 = _norm_mod_bwd(s["x_mid"], dh2, dx_out, p["g_ffn"], sc2)
    dm1, dgt1 = _gate_bwd(dx_mid, s["m1"], gt1)
    dy = _matmul(dm1, p["w_out"], nt=True, name="mm_out_dx")
    g["w_out"] = _matmul(s["y"].T, dm1, name="mm_out_dw")
    z = s["z"]
    (dzx, dzy, g["lru_conv_w"], g["lru_conv_b"], g["lru_wa"], g["lru_ba"], g["lru_wi"], g["lru_bi"],
     g["lru_lambda"]) = _lru_bwd(z, dy, s["hs"], p["lru_conv_w"], p["lru_conv_b"], p["lru_wa"], p["lru_ba"], p["lru_wi"],
                                 p["lru_bi"], p["lru_lambda"])
    k_t = z[:, ZK:ZK + ATTN_WIDTH].T.astype(BF16)
    do_t = dy[:, POOL_WIDTH:POOL_WIDTH + ATTN_WIDTH].T
    dq_t, dk, dv, dfk, dfq = _flash_bwd(z, k_t, dy, do_t, s["o_t"], s["lse"], s["f_col"], s["f_row"])
    dq = dq_t.T
    dF_pad = jnp.pad((dfq[:, 0, :] + dfk[:, :, 0]).T, ((0, 0), (0, 128 - N_HEADS)))
    dzf, db_f = _forget_cumsum_bwd(z, p["b_f"], dF_pad)
    g["b_f"] = db_f[:, :N_HEADS]
    dzp, g["pool_w"], g["pool_scale"] = _pool_bwd(z, dy, p["pool_w"], p["pool_scale"])
    S = z.shape[0]
    dz = jnp.concatenate([dzp, dq, dk, dv, dzx, dzy, dzf, jnp.zeros((S, ZW - ZF - 128), BF16)], axis=1)
    dh1 = _matmul(dz, p["w_in"], nt=True, name="mm_in_dx")
    g["w_in"] = _matmul(s["h1"].T, dz, name="mm_in_dw")
    dx_in, g["g_mix"], dsc1, dsh1 = _norm_mod_bwd(s["x"], dh1, dx_mid, p["g_mix"], sc1)
    return dx_in, g, (dsh1, dsc1, dgt1, dsh2, dsc2, dgt2)


def _local_step(x, target, mods, layers, final_g):
    saved = []
    for mod, p in zip(mods, layers):
        x, s = _layer_fwd(x, mod, p)
        saved.append(s)
    dx, dfinal_g, loss = _loss_head(x, target, final_g)
    grads, dmods = [None] * len(layers), [None] * len(layers)
    for l in reversed(range(len(layers))):
        dx, grads[l], dmods[l] = _layer_bwd(dx, mods[l], layers[l], saved[l])
    return loss, dx, grads, dmods, dfinal_g


def _pad_in_cols(w):
    D = w.shape[0]
    return jnp.concatenate([w[:, :3584], w[:, 3592:N_IN], w[:, 3584:3592], jnp.zeros((D, ZW - N_IN), w.dtype)], axis=1)


def _unpad_in_cols(w):
    return jnp.concatenate([w[:, :3584], w[:, ZF:ZF + N_HEADS], w[:, 3584:ZF]], axis=1)


BIG = ("w_in", "w_out", "w_ffn_gate", "w_ffn_up", "w_ffn_down")
SMALL = ("b_ada", "g_mix", "b_f", "pool_w", "pool_scale", "lru_conv_w", "lru_conv_b", "lru_wa", "lru_ba", "lru_wi",
         "lru_bi", "lru_lambda", "g_ffn", "ffn_conv_w", "ffn_conv_b", "final_g")
SHARDED_SMALL = ("lru_conv_w", "ffn_conv_w")
WEIGHTS = ("w_ada", "b_ada", "g_mix", "w_in", "b_f", "pool_w", "pool_scale", "lru_conv_w", "lru_conv_b", "lru_wa",
           "lru_ba", "lru_wi", "lru_bi", "lru_lambda", "w_out", "g_ffn", "w_ffn_gate", "w_ffn_up", "ffn_conv_w",
           "ffn_conv_b", "w_ffn_down", "final_g")


PACK_QUANTUM = 512 * 128


def _pack(arrays):
    flat = jnp.concatenate([a.reshape(-1).astype(F32) for a in arrays])
    n = -(-flat.shape[0] // PACK_QUANTUM) * PACK_QUANTUM
    return jnp.pad(flat, (0, n - flat.shape[0])).reshape(n // 128, 128)


def _unpack(packed, shapes):
    flat = packed.reshape(-1)
    out, off = [], 0
    for shp in shapes:
        n = int(np.prod(shp))
        out.append(flat[off:off + n].reshape(shp))
        off += n
    return out


def kernel(x, c, w_ada, b_ada, g_mix, w_in, b_f, pool_w, pool_scale, lru_conv_w, lru_conv_b, lru_wa, lru_ba, lru_wi, lru_bi, lru_lambda, w_out, g_ffn, w_ffn_gate, w_ffn_up, ffn_conv_w, ffn_conv_b, w_ffn_down, final_g, loss_target, m_w_ada, m_b_ada, m_g_mix, m_w_in, m_b_f, m_pool_w, m_pool_scale, m_lru_conv_w, m_lru_conv_b, m_lru_wa, m_lru_ba, m_lru_wi, m_lru_bi, m_lru_lambda, m_w_out, m_g_ffn, m_w_ffn_gate, m_w_ffn_up, m_ffn_conv_w, m_ffn_conv_b, m_w_ffn_down, m_final_g, v_w_ada, v_b_ada, v_g_mix, v_w_in, v_b_f, v_pool_w, v_pool_scale, v_lru_conv_w, v_lru_conv_b, v_lru_wa, v_lru_ba, v_lru_wi, v_lru_bi, v_lru_lambda, v_w_out, v_g_ffn, v_w_ffn_gate, v_w_ffn_up, v_ffn_conv_w, v_ffn_conv_b, v_w_ffn_down, v_final_g):
    env = dict(locals())
    W = {n: env[n] for n in WEIGHTS}
    M = {n: env["m_" + n] for n in WEIGHTS}
    V = {n: env["v_" + n] for n in WEIGHTS}
    L, D = g_mix.shape
    S = x.shape[1]
    F = 4 * w_ffn_gate.shape[2]
    ix, iy, ic = lax.axis_index("x"), lax.axis_index("y"), lax.axis_index("c")
    q = 2 * ix + iy
    me = 2 * q + ic

    head = _allgather8(_pack([c, lru_conv_w, ffn_conv_w]))
    nlc, nfc = lru_conv_w.size, ffn_conv_w.size
    c_all = head.reshape(N_DEV, -1)[:, :D]
    lru_cw = jnp.concatenate([head[2 * k].reshape(-1)[D:D + nlc].reshape(L, 4, -1) for k in range(N_CHIPS)], axis=2)
    ffn_cw = jnp.concatenate([head[2 * k].reshape(-1)[D + nlc:D + nlc + nfc].reshape(L, 3, -1) for k in range(N_CHIPS)],
                             axis=2)
    cact = _silu_rows(c_all)

    NA = w_ada.shape[2]
    mod_part = jnp.stack([_matmul(cact, w_ada[l], name="mm_ada") for l in range(L)])
    mod_all = _allgather8(mod_part.reshape(L * N_DEV, NA)).reshape(N_DEV, L, N_DEV, NA)
    mod_full = jnp.concatenate([mod_all[2 * k] for k in range(N_CHIPS)], axis=2)
    mod_mine = lax.dynamic_index_in_dim(mod_full, me, axis=1, keepdims=False) + b_ada
    mods = [[mod_mine[l, k * D:(k + 1) * D].reshape(1, D) for k in range(6)] for l in range(L)]

    shards = [W[n].astype(BF16) for n in BIG]
    layers = []
    for l in range(L):
        win4, wout4, gu, down4 = _gather_weights(l, *shards)
        layers.append(dict(
            w_in=_pad_in_cols(jnp.transpose(win4, (1, 0, 2)).reshape(D, N_IN)), w_out=wout4.reshape(D, D),
            w_gu=gu, w_down=down4.reshape(F, D),
            g_mix=g_mix[l][None], g_ffn=g_ffn[l][None], b_f=jnp.pad(b_f[l], (0, 128 - N_HEADS))[None],
            pool_w=pool_w[l], pool_scale=pool_scale[l][None], lru_conv_w=lru_cw[l], lru_conv_b=lru_conv_b[l][None],
            lru_wa=lru_wa[l], lru_ba=lru_ba[l][None], lru_wi=lru_wi[l], lru_bi=lru_bi[l][None],
            lru_lambda=lru_lambda[l][None], ffn_conv_w=ffn_cw[l], ffn_conv_b=ffn_conv_b[l][None]))

    loss, dx, grads, dmods, dfinal_g = _local_step(x[0], loss_target[0], mods, layers, final_g[None])

    G = {n: [] for n in BIG}
    for l in range(L):
        gl = grads[l]
        dwin4 = jnp.transpose(_unpad_in_cols(gl["w_in"]).reshape(D, N_CHIPS, N_IN // N_CHIPS), (1, 0, 2))
        leads = [dwin4, gl["w_out"].reshape(N_CHIPS, D // N_CHIPS, D), gl["w_down"].reshape(N_CHIPS, F // N_CHIPS, D)]
        g_in, g_out, g_down, g_gu = _reduce_scatter(leads, gl["w_gu"], ic, q)
        for n, a in zip(BIG, (g_in, g_out, g_gu[0], g_gu[1], g_down)):
            G[n].append(a)
    G = {n: jnp.stack(G[n]) for n in BIG}

    stack = lambda name: jnp.stack([grads[l][name] for l in range(L)])
    dmod = jnp.stack([jnp.concatenate(dmods[l], axis=1)[0] for l in range(L)])
    small = dict(b_ada=dmod, g_mix=stack("g_mix"), b_f=stack("b_f"), pool_w=stack("pool_w"),
                 pool_scale=stack("pool_scale"), lru_conv_w=stack("lru_conv_w"), lru_conv_b=stack("lru_conv_b"),
                 lru_wa=stack("lru_wa"), lru_ba=stack("lru_ba"), lru_wi=stack("lru_wi"), lru_bi=stack("lru_bi"),
                 lru_lambda=stack("lru_lambda"), g_ffn=stack("g_ffn"), ffn_conv_w=stack("ffn_conv_w"),
                 ffn_conv_b=stack("ffn_conv_b"), final_g=dfinal_g)
    packed = _pack([small[n] for n in SMALL] + [loss[0, :1]])
    everyone = _allgather8(packed)
    zero_sel = jnp.zeros((2,), jnp.int32)
    total = _sum_parts([(everyone, k) for k in range(N_DEV)], zero_sel, packed.shape[0], 128, "device_sum")
    sums = _unpack(total, [small[n].shape for n in SMALL] + [(1,)])
    loss_total = sums[-1][0]
    for n, a in zip(SMALL, sums[:-1]):
        a = a.reshape((L, -1, a.shape[-1])) if n in SHARDED_SMALL else a.reshape(W[n].shape)
        if n in SHARDED_SMALL:
            a = lax.dynamic_slice_in_dim(a, q * W[n].shape[2], W[n].shape[2], axis=2)
        G[n] = a

    dmod_all = everyone.reshape(N_DEV, -1)[:, :L * 6 * D].reshape(N_DEV, L, 6 * D)
    dmod_cols = jnp.transpose(lax.dynamic_slice_in_dim(dmod_all, q * NA, NA, axis=2), (1, 0, 2))
    G["w_ada"], d_ada, m_ada, v_ada = _ada_grad_adamw(cact.T, dmod_cols, w_ada, m_w_ada, v_w_ada)
    delta, new_m, new_v = {"w_ada": d_ada}, {"w_ada": m_ada}, {"w_ada": v_ada}

    for n in BIG:
        cols = W[n].shape[-1]
        outs = _adamw(*[a.reshape(-1, cols) for a in (W[n], G[n], M[n], V[n])])
        delta[n], new_m[n], new_v[n] = [o.reshape(W[n].shape) for o in outs]
    outs = _adamw(*[_pack([t[n] for n in SMALL]) for t in (W, G, M, V)])
    shapes = [W[n].shape for n in SMALL]
    for tgt, o in zip((delta, new_m, new_v), outs):
        for n, a in zip(SMALL, _unpack(o, shapes)):
            tgt[n] = a

    return (loss_total, dx[None], *[G[n] for n in WEIGHTS], *[delta[n] for n in WEIGHTS],
            *[new_m[n] for n in WEIGHTS], *[new_v[n] for n in WEIGHTS])
```

```python
import functools
import math

import jax
import jax.numpy as jnp
import numpy as np
from jax import lax
from jax.experimental import pallas as pl
from jax.experimental.pallas import tpu as pltpu

F32 = jnp.float32
BF16 = jnp.bfloat16
MESH = pl.DeviceIdType.MESH

EPS = 1e-6
HEAD_DIM = 128
POOL_WIDTH = 512
POOL_WINDOWS = (2, 4, 8, 16)
ATTN_WIDTH = 1024
N_HEADS = 8
LRU_WIDTH = 512
LRU_C = 8.0
N_IN = 4616
ZP, ZQ, ZK, ZV, ZX, ZY, ZF, ZW = 0, 512, 1536, 2560, 3584, 4096, 4608, 5120
FFN_CHUNK = 512
N_CHIPS = 4
N_DEV = 8

ADAM_LR, ADAM_B1, ADAM_B2, ADAM_EPS, ADAM_WD, ADAM_STEP = 0.001, 0.9, 0.999, 1e-08, 0.01, 10

TILES = dict(mm_m=512, mm_n=1024, mm_k=2048, row=512, attn=512, lru=256, cum=512, ew=256)
VMEM_LIMIT = 48 * 2**20


def _params(sem):
    return pltpu.CompilerParams(dimension_semantics=sem, vmem_limit_bytes=VMEM_LIMIT)


def _div_tile(n, pref, align):
    if n <= pref:
        return n
    t = (pref // align) * align
    while t >= align:
        if n % t == 0:
            return t
        t -= align
    raise ValueError(f"no tile for {n}")


def _sigmoid(x):
    return 0.5 * jnp.tanh(0.5 * x) + 0.5


def _gelu_parts(x):
    k = math.sqrt(2.0 / math.pi)
    u = k * (x + 0.044715 * x * x * x)
    t = jnp.tanh(u)
    gel = 0.5 * x * (1.0 + t)
    dgel = 0.5 * (1.0 + t) + 0.5 * x * (1.0 - t * t) * k * (1.0 + 3 * 0.044715 * x * x)
    return gel, dgel


def _neg_expm1(y):
    series = -y * (1.0 + y * (0.5 + y * (1.0 / 6 + y * (1.0 / 24 + y * (1.0 / 120)))))
    return jnp.where(y > -0.1, series, 1.0 - jnp.exp(y))


def _dot(a, b):
    return jnp.dot(a.astype(BF16), b.astype(BF16), preferred_element_type=F32)


def _dot_nt(a, b):
    return lax.dot_general(a.astype(BF16), b.astype(BF16), (((1,), (1,)), ((), ())), preferred_element_type=F32)


def _dot3(tri, v):
    hi = v.astype(BF16)
    r1 = v - hi.astype(F32)
    mid = r1.astype(BF16)
    lo = (r1 - mid.astype(F32)).astype(BF16)
    t = tri.astype(BF16)
    return (jnp.dot(t, hi, preferred_element_type=F32) + jnp.dot(t, mid, preferred_element_type=F32)
            + jnp.dot(t, lo, preferred_element_type=F32))


def _colsum(v):
    return jnp.sum(v, axis=0, keepdims=True)


def _rows(n, cols=128):
    return lax.broadcasted_iota(jnp.int32, (n, cols), 0)


def _matmul(a, b, *, nt=False, out_dtype=F32, res=None, gate=None, gu=None, name="matmul"):
    M, K = a.shape
    pair = 2 * FFN_CHUNK
    if gu == "b":
        N = b.shape[1] if nt else 2 * b.shape[2]
    else:
        N = b.shape[0] if nt else b.shape[1]
    tm = _div_tile(M, TILES["mm_m"] * (1 if gu is None else 2), 8)
    tn = _div_tile(N, TILES["mm_n"], 128)
    tk = _div_tile(K, TILES["mm_k"], 128)
    if gu == "b" and nt:
        tk = pair
    elif gu == "b":
        tn = pair
    elif gu == "out":
        tn = FFN_CHUNK
    nk = K // tk
    epi = res is not None
    n_b = 2 if gu == "b" else 1

    def body(*refs):
        a_ref, b_refs, rest = refs[0], refs[1:1 + n_b], refs[1 + n_b:]
        if epi:
            res_ref, gate_ref, o_ref, x_ref = rest[:4]
        else:
            o_ref = rest[0]
        if gu == "b" and nt:
            part = (_dot_nt(a_ref[:, :FFN_CHUNK], b_refs[0][...]) + _dot_nt(a_ref[:, FFN_CHUNK:], b_refs[1][...]))
        elif gu == "b":
            part = jnp.concatenate([_dot(a_ref[...], b_refs[0][...]), _dot(a_ref[...], b_refs[1][...])], axis=1)
        else:
            part = _dot_nt(a_ref[...], b_refs[0][...]) if nt else _dot(a_ref[...], b_refs[0][...])

        def finish(acc):
            o_ref[...] = acc.astype(o_ref.dtype)
            if epi:
                x_ref[...] = res_ref[...] + gate_ref[...] * acc

        if nk == 1:
            finish(part)
        else:
            acc_ref = refs[-1]
            k = pl.program_id(2)

            @pl.when(k == 0)
            def _():
                acc_ref[...] = part

            @pl.when(k > 0)
            def _():
                acc_ref[...] += part

            @pl.when(k == nk - 1)
            def _():
                finish(acc_ref[...])

    a_spec = pl.BlockSpec((tm, tk), lambda i, j, k: (i, k))
    o_spec = pl.BlockSpec((tm, tn), lambda i, j, k: (i, j))
    out_shape = jax.ShapeDtypeStruct((M, N), out_dtype)
    if gu == "b" and nt:
        b_specs = [pl.BlockSpec((None, tn, FFN_CHUNK), lambda i, j, k, p=p: (p, j, k)) for p in range(2)]
    elif gu == "b":
        b_specs = [pl.BlockSpec((None, tk, FFN_CHUNK), lambda i, j, k, p=p: (p, k, j)) for p in range(2)]
    elif nt:
        b_specs = [pl.BlockSpec((tn, tk), lambda i, j, k: (j, k))]
    else:
        b_specs = [pl.BlockSpec((tk, tn), lambda i, j, k: (k, j))]
    if gu == "out":
        o_spec = pl.BlockSpec((None, tm, tn), lambda i, j, k: (j % 2, i, j // 2))
        out_shape = jax.ShapeDtypeStruct((2, M, N // 2), out_dtype)
    in_specs, args = [a_spec] + b_specs, [a] + [b] * n_b
    out_specs = o_spec
    if epi:
        in_specs += [o_spec, pl.BlockSpec((1, tn), lambda i, j, k: (0, j))]
        args += [res, gate]
        out_specs = (o_spec, o_spec)
        out_shape = (out_shape, jax.ShapeDtypeStruct((M, N), F32))
    return pl.pallas_call(
        body, name=name, grid=(M // tm, N // tn, nk), in_specs=in_specs, out_specs=out_specs, out_shape=out_shape,
        scratch_shapes=[pltpu.VMEM((tm, tn), F32)] if nk > 1 else [],
        compiler_params=_params(("parallel", "parallel", "arbitrary")),
    )(*args)


def _norm_mod(x, g, sc, sh):
    S, D = x.shape
    tr = _div_tile(S, TILES["row"], 8)

    def body(x_ref, g_ref, sc_ref, sh_ref, h_ref):
        xf = x_ref[...]
        r = lax.rsqrt(jnp.mean(xf * xf, axis=-1, keepdims=True) + EPS)
        h_ref[...] = (((xf * r) * g_ref[...]) * (1.0 + sc_ref[...]) + sh_ref[...]).astype(h_ref.dtype)

    row = pl.BlockSpec((tr, D), lambda i: (i, 0))
    vec = pl.BlockSpec((1, D), lambda i: (0, 0))
    return pl.pallas_call(
        body, name="norm_mod", grid=(S // tr,), in_specs=[row, vec, vec, vec], out_specs=row,
        out_shape=jax.ShapeDtypeStruct((S, D), BF16), compiler_params=_params(("parallel",)),
    )(x, g, sc, sh)


def _norm_mod_bwd(x, dh, dres, g, sc):
    S, D = x.shape
    tr = _div_tile(S, TILES["ew"], 8)

    def body(x_ref, dh_ref, dres_ref, g_ref, sc_ref, dx_ref, dg_ref, dsc_ref, dsh_ref):
        xf, dh_ = x_ref[...], dh_ref[...]
        r = lax.rsqrt(jnp.mean(xf * xf, axis=-1, keepdims=True) + EPS)
        xhat = xf * r
        dxhat = dh_ * (g_ref[...] * (1.0 + sc_ref[...]))
        dx_ref[...] = dres_ref[...] + r * (dxhat - xhat * jnp.mean(dxhat * xhat, axis=-1, keepdims=True))
        t = _colsum(dh_ * xhat)

        @pl.when(pl.program_id(0) == 0)
        def _():
            dg_ref[...] = jnp.zeros_like(dg_ref)
            dsc_ref[...] = jnp.zeros_like(dsc_ref)
            dsh_ref[...] = jnp.zeros_like(dsh_ref)

        dg_ref[...] += t * (1.0 + sc_ref[...])
        dsc_ref[...] += t * g_ref[...]
        dsh_ref[...] += _colsum(dh_)

    row = pl.BlockSpec((tr, D), lambda i: (i, 0))
    vec = pl.BlockSpec((1, D), lambda i: (0, 0))
    vshape = jax.ShapeDtypeStruct((1, D), F32)
    return pl.pallas_call(
        body, name="norm_mod_bwd", grid=(S // tr,), in_specs=[row, row, row, vec, vec], out_specs=(row, vec, vec, vec),
        out_shape=(jax.ShapeDtypeStruct((S, D), F32), vshape, vshape, vshape), compiler_params=_params(("arbitrary",)),
    )(x, dh, dres, g, sc)


def _gate_bwd(dx, m, gt):
    S, D = dx.shape
    tr = _div_tile(S, TILES["row"], 8)

    def body(dx_ref, m_ref, gt_ref, dm_ref, dgt_ref):
        d = dx_ref[...]
        dm_ref[...] = (d * gt_ref[...]).astype(dm_ref.dtype)

        @pl.when(pl.program_id(0) == 0)
        def _():
            dgt_ref[...] = jnp.zeros_like(dgt_ref)

        dgt_ref[...] += _colsum(d * m_ref[...])

    row = pl.BlockSpec((tr, D), lambda i: (i, 0))
    vec = pl.BlockSpec((1, D), lambda i: (0, 0))
    return pl.pallas_call(
        body, name="gate_bwd", grid=(S // tr,), in_specs=[row, row, vec], out_specs=(row, vec),
        out_shape=(jax.ShapeDtypeStruct((S, D), BF16), jax.ShapeDtypeStruct((1, D), F32)),
        compiler_params=_params(("arbitrary",)),
    )(dx, m, gt)


def _loss_head(x, target, g):
    S, D = x.shape
    tr = _div_tile(S, TILES["ew"], 8)

    def body(x_ref, t_ref, g_ref, dx_ref, dg_ref, loss_ref):
        xf = x_ref[...]
        r = lax.rsqrt(jnp.mean(xf * xf, axis=-1, keepdims=True) + EPS)
        xhat = xf * r
        err = xhat * g_ref[...] - t_ref[...]
        dy = err * (1.0 / D)
        dxhat = dy * g_ref[...]
        dx_ref[...] = r * (dxhat - xhat * jnp.mean(dxhat * xhat, axis=-1, keepdims=True))

        @pl.when(pl.program_id(0) == 0)
        def _():
            dg_ref[...] = jnp.zeros_like(dg_ref)
            loss_ref[...] = jnp.zeros_like(loss_ref)

        dg_ref[...] += _colsum(dy * xhat)
        loss_ref[...] += 0.5 * jnp.sum(jnp.mean(err * err, axis=-1, keepdims=True))

    row = pl.BlockSpec((tr, D), lambda i: (i, 0))
    vec = pl.BlockSpec((1, D), lambda i: (0, 0))
    one = pl.BlockSpec((1, 128), lambda i: (0, 0))
    return pl.pallas_call(
        body, name="loss_head", grid=(S // tr,), in_specs=[row, row, vec], out_specs=(row, vec, one),
        out_shape=(jax.ShapeDtypeStruct((S, D), F32), jax.ShapeDtypeStruct((1, D), F32),
                   jax.ShapeDtypeStruct((1, 128), F32)),
        compiler_params=_params(("arbitrary",)),
    )(x, target, g)


POOL_HALO = 16


def _pool_delta(ext, u, first_pos, tr):
    pos = (first_pos + _rows(tr) + 1).astype(F32)
    outs = []
    for gi, win in enumerate(POOL_WINDOWS):
        s = ext[:, gi * 128:(gi + 1) * 128]
        d = 1
        while d < win:
            s = s + pltpu.roll(s, d, 0)
            d *= 2
        outs.append(s[POOL_HALO:] / jnp.minimum(pos, float(win)) - u[:, gi * 128:(gi + 1) * 128])
    return outs


def _pool_fwd(z, w, scale):
    S = z.shape[0]
    tr = _div_tile(S, TILES["row"], POOL_HALO)
    hb = tr // POOL_HALO

    def body(z_ref, halo_ref, w_ref, sc_ref, y_ref):
        i = pl.program_id(0)
        u = z_ref[...]
        halo = jnp.where(i > 0, halo_ref[...], 0.0)
        ds_ = _pool_delta(jnp.concatenate([halo, u], axis=0), u, i * tr, tr)
        for gi in range(4):
            y_ref[:, gi * 128:(gi + 1) * 128] = _dot(ds_[gi], w_ref[gi]) * sc_ref[:, gi * 128:(gi + 1) * 128]

    return pl.pallas_call(
        body, name="pool_fwd", grid=(S // tr,),
        in_specs=[pl.BlockSpec((tr, POOL_WIDTH), lambda i: (i, 0)),
                  pl.BlockSpec((POOL_HALO, POOL_WIDTH), lambda i: (jnp.maximum(i * hb - 1, 0), 0)),
                  pl.BlockSpec((4, 128, 128), lambda i: (0, 0, 0)), pl.BlockSpec((1, POOL_WIDTH), lambda i: (0, 0))],
        out_specs=pl.BlockSpec((tr, POOL_WIDTH), lambda i: (i, 0)),
        out_shape=jax.ShapeDtypeStruct((S, POOL_WIDTH), F32), compiler_params=_params(("parallel",)),
    )(z, z, w, scale)


def _pool_bwd(z, dy, w, scale):
    S = z.shape[0]
    tr = _div_tile(S, TILES["row"], POOL_HALO)
    hb = tr // POOL_HALO
    nt = S // tr

    def body(z_ref, halo_ref, dy_ref, dyn_ref, w_ref, sc_ref, dz_ref, dw_ref, dsc_ref):
        i = pl.program_id(0)
        u = z_ref[...]
        halo = jnp.where(i > 0, halo_ref[...], 0.0)
        ds_ = _pool_delta(jnp.concatenate([halo, u], axis=0), u, i * tr, tr)
        dy_ext = jnp.concatenate([dy_ref[...], jnp.where(i < nt - 1, dyn_ref[...], 0.0)], axis=0)
        pos = (i * tr + _rows(tr + POOL_HALO) + 1).astype(F32)

        @pl.when(i == 0)
        def _():
            dw_ref[...] = jnp.zeros_like(dw_ref)
            dsc_ref[...] = jnp.zeros_like(dsc_ref)

        for gi, win in enumerate(POOL_WINDOWS):
            cols = slice(gi * 128, (gi + 1) * 128)
            dyg = dy_ext[:, cols]
            dys = dyg * sc_ref[:, cols]
            dsc_ref[:, cols] += _colsum(dyg[:tr] * _dot(ds_[gi], w_ref[gi]))
            dw_ref[gi] += _dot(ds_[gi].T, dys[:tr])
            dd = _dot_nt(dys, w_ref[gi])
            e = dd / jnp.minimum(pos, float(win))
            d = 1
            while d < win:
                e = e + pltpu.roll(e, tr + POOL_HALO - d, 0)
                d *= 2
            dz_ref[:, cols] = (e[:tr] - dd[:tr]).astype(dz_ref.dtype)

    return pl.pallas_call(
        body, name="pool_bwd", grid=(nt,),
        in_specs=[pl.BlockSpec((tr, POOL_WIDTH), lambda i: (i, 0)),
                  pl.BlockSpec((POOL_HALO, POOL_WIDTH), lambda i: (jnp.maximum(i * hb - 1, 0), 0)),
                  pl.BlockSpec((tr, POOL_WIDTH), lambda i: (i, 0)),
                  pl.BlockSpec((POOL_HALO, POOL_WIDTH), lambda i: (jnp.minimum((i + 1) * hb, nt * hb - 1), 0)),
                  pl.BlockSpec((4, 128, 128), lambda i: (0, 0, 0)), pl.BlockSpec((1, POOL_WIDTH), lambda i: (0, 0))],
        out_specs=(pl.BlockSpec((tr, POOL_WIDTH), lambda i: (i, 0)), pl.BlockSpec((4, 128, 128), lambda i: (0, 0, 0)),
                   pl.BlockSpec((1, POOL_WIDTH), lambda i: (0, 0))),
        out_shape=(jax.ShapeDtypeStruct((S, POOL_WIDTH), BF16), jax.ShapeDtypeStruct((4, 128, 128), F32),
                   jax.ShapeDtypeStruct((1, POOL_WIDTH), F32)),
        compiler_params=_params(("arbitrary",)),
    )(z, z, dy, dy, w, scale)


def _log_sigmoid(x):
    return jnp.minimum(x, 0.0) - jnp.log(1.0 + jnp.exp(-jnp.abs(x)))


def _forget_cumsum(z, b_f):
    S = z.shape[0]
    tr = _div_tile(S, TILES["cum"], 8)
    zf_block = ZF // 128

    def body(z_ref, b_ref, f_ref, carry):
        @pl.when(pl.program_id(0) == 0)
        def _():
            carry[...] = jnp.zeros_like(carry)

        lf = _log_sigmoid(z_ref[...] + b_ref[...])
        tri = lax.broadcasted_iota(jnp.int32, (tr, tr), 1) <= lax.broadcasted_iota(jnp.int32, (tr, tr), 0)
        f_ref[...] = _dot3(tri, lf) + carry[...]
        carry[...] += _colsum(lf)

    return pl.pallas_call(
        body, name="forget_cumsum", grid=(S // tr,),
        in_specs=[pl.BlockSpec((tr, 128), lambda i: (i, zf_block)), pl.BlockSpec((1, 128), lambda i: (0, 0))],
        out_specs=pl.BlockSpec((tr, 128), lambda i: (i, 0)), out_shape=jax.ShapeDtypeStruct((S, 128), F32),
        scratch_shapes=[pltpu.VMEM((1, 128), F32)], compiler_params=_params(("arbitrary",)),
    )(z, b_f)


def _forget_cumsum_bwd(z, b_f, dF):
    S = z.shape[0]
    tr = _div_tile(S, TILES["cum"], 8)
    nt = S // tr
    zf_block = ZF // 128

    def body(z_ref, b_ref, df_ref, dz_ref, db_ref, carry):
        @pl.when(pl.program_id(0) == 0)
        def _():
            carry[...] = jnp.zeros_like(carry)
            db_ref[...] = jnp.zeros_like(db_ref)

        dF_ = df_ref[...]
        tri = lax.broadcasted_iota(jnp.int32, (tr, tr), 1) >= lax.broadcasted_iota(jnp.int32, (tr, tr), 0)
        dlf = _dot3(tri, dF_) + carry[...]
        carry[...] += _colsum(dF_)
        lane = lax.broadcasted_iota(jnp.int32, (tr, 128), 1)
        dzf = jnp.where(lane < N_HEADS, dlf * _sigmoid(-(z_ref[...] + b_ref[...])), 0.0)
        dz_ref[...] = dzf.astype(dz_ref.dtype)
        db_ref[...] += _colsum(dzf)

    return pl.pallas_call(
        body, name="forget_cumsum_bwd", grid=(nt,),
        in_specs=[pl.BlockSpec((tr, 128), lambda i: (nt - 1 - i, zf_block)), pl.BlockSpec((1, 128), lambda i: (0, 0)),
                  pl.BlockSpec((tr, 128), lambda i: (nt - 1 - i, 0))],
        out_specs=(pl.BlockSpec((tr, 128), lambda i: (nt - 1 - i, 0)), pl.BlockSpec((1, 128), lambda i: (0, 0))),
        out_shape=(jax.ShapeDtypeStruct((S, 128), BF16), jax.ShapeDtypeStruct((1, 128), F32)),
        scratch_shapes=[pltpu.VMEM((1, 128), F32)], compiler_params=_params(("arbitrary",)),
    )(z, b_f, dF)


NEG = -1e30
ATTN_SCALE = HEAD_DIM ** -0.5


def _on_block_kind(qi, kj, fn):
    @pl.when(qi == kj)
    def _():
        fn(True)

    @pl.when(qi != kj)
    def _():
        fn(False)


FIRST, LAST, HEAD_FIRST, HEAD_LAST, KEY_ZERO = 1, 2, 4, 8, 16


def _tri_schedule(n, by_key=False):
    outer, inner, flags = [], [], []
    for a in range(n):
        partners = list(range(a, n)) if by_key else list(range(a + 1))
        for idx, b in enumerate(partners):
            f = FIRST if idx == 0 else 0
            f |= LAST if idx == len(partners) - 1 else 0
            f |= KEY_ZERO if (a if by_key else b) == 0 else 0
            outer.append(a)
            inner.append(b)
            flags.append(f)
    flags[0] |= HEAD_FIRST
    flags[-1] |= HEAD_LAST
    return [jnp.asarray(np.array(v, np.int32)) for v in (outer, inner, flags)]


def _flash_call(body, name, sched, in_specs, out_specs, out_shape, scratch, comm_ins=0, comm_shapes=(), comm_sems=()):
    grid_spec = pltpu.PrefetchScalarGridSpec(
        num_scalar_prefetch=3, grid=(N_HEADS, int(sched[0].shape[0])), in_specs=list(in_specs) + [ANY] * comm_ins,
        out_specs=tuple(out_specs) + (ANY,) * len(comm_shapes), scratch_shapes=list(scratch) + list(comm_sems))
    heads = "arbitrary" if comm_sems else "parallel"
    return pl.pallas_call(body, name=name, grid_spec=grid_spec, out_shape=tuple(out_shape) + tuple(comm_shapes),
                          compiler_params=_params((heads, "arbitrary")))


def _scores_t(q, k, fq_row, fk_col, diagonal, floor):
    st = _dot_nt(k, q) * ATTN_SCALE + fq_row - fk_col
    if not diagonal:
        return st
    t = st.shape[0]
    return jnp.where(lax.broadcasted_iota(jnp.int32, (t, t), 0) <= lax.broadcasted_iota(jnp.int32, (t, t), 1), st, floor)


def _flash_fwd(z, v_t, f_col, f_row, next_shards=None):
    S = z.shape[0]
    t = _div_tile(S, TILES["attn"], 128)
    sched = _tri_schedule(S // t)
    n_steps = int(sched[0].shape[0])
    qb, kb = ZQ // 128, ZK // 128
    n_comm = N_SHARDS if next_shards is not None else 0

    def body(qt, kt, ft, q_ref, k_ref, vt_ref, fq_ref, fk_ref, *rest):
        shard_refs, rest = rest[:n_comm], rest[n_comm:]
        (o_ref, lse_ref), rest = rest[:2], rest[2:]
        gathered, rest = (rest[:4], rest[4:]) if n_comm else ((), rest)
        (m_sc, l_sc, acc_sc), sems = rest[:3], rest[3:]
        head, step = pl.program_id(0), pl.program_id(1)
        qi, kj, fl = qt[step], kt[step], ft[step]

        if n_comm:
            @pl.when((head == 0) & (step == 0))
            def _():
                _GatherPlan(shard_refs, gathered, sems).start()

            @pl.when((head == N_HEADS - 2) & (step == 0))
            def _():
                _GatherPlan(shard_refs, gathered, sems).forward()

            @pl.when((head == N_HEADS - 1) & (step == n_steps - 1))
            def _():
                _GatherPlan(shard_refs, gathered, sems).finish()

        @pl.when((fl & FIRST) != 0)
        def _():
            m_sc[...] = jnp.full_like(m_sc, NEG)
            l_sc[...] = jnp.zeros_like(l_sc)
            acc_sc[...] = jnp.zeros_like(acc_sc)

        def update(diagonal):
            st = _scores_t(q_ref[...], k_ref[...], fq_ref[...], fk_ref[...], diagonal, NEG)
            m_new = jnp.maximum(m_sc[...], jnp.max(st, axis=0, keepdims=True))
            alpha = jnp.exp(m_sc[...] - m_new)
            pt = jnp.exp(st - m_new)
            l_sc[...] = alpha * l_sc[...] + jnp.sum(pt, axis=0, keepdims=True)
            acc_sc[...] = alpha * acc_sc[...] + _dot(vt_ref[...], pt)
            m_sc[...] = m_new

        _on_block_kind(qi, kj, update)

        @pl.when((fl & LAST) != 0)
        def _():
            o_ref[...] = acc_sc[...] / l_sc[...]
            lse_ref[...] = m_sc[...] + jnp.log(l_sc[...])

    row = pl.BlockSpec((None, 1, t), lambda h, s, qt, kt, ft: (h, 0, qt[s]))
    return _flash_call(
        body, "flash_fwd", sched,
        [pl.BlockSpec((t, 128), lambda h, s, qt, kt, ft: (qt[s], qb + h)),
         pl.BlockSpec((t, 128), lambda h, s, qt, kt, ft: (kt[s], kb + h)),
         pl.BlockSpec((128, t), lambda h, s, qt, kt, ft: (h, kt[s])), row,
         pl.BlockSpec((None, t, 1), lambda h, s, qt, kt, ft: (h, kt[s], 0))],
        (pl.BlockSpec((128, t), lambda h, s, qt, kt, ft: (h, qt[s])), row),
        (jax.ShapeDtypeStruct((ATTN_WIDTH, S), F32), jax.ShapeDtypeStruct((N_HEADS, 1, S), F32)),
        [pltpu.VMEM((1, t), F32), pltpu.VMEM((1, t), F32), pltpu.VMEM((128, t), F32)],
        comm_ins=n_comm, comm_shapes=_gather_shapes(next_shards) if n_comm else (),
        comm_sems=GATHER_SEMS if n_comm else (),
    )(*sched, z, z, v_t, f_row, f_col, *(next_shards or ()))


def _flash_bwd(z, k_t, dy, do_t, o_t, lse_row, f_col, f_row, pairs=None):
    S = z.shape[0]
    t = _div_tile(S, TILES["attn"], 128)
    n = S // t
    sched = _tri_schedule(n, by_key=True)
    n_steps = int(sched[0].shape[0])
    qb, kb, vb = ZQ // 128, ZK // 128, ZV // 128
    dob = POOL_WIDTH // 128
    n_comm = N_PAIRS if pairs is not None else 0

    def body(kt, qt, ft, q_ref, k_ref, v_ref, kt_ref, do_ref, dot_ref, ot_ref, lse_ref, fq_ref, fk_ref, *rest):
        pair_refs, rest = rest[:n_comm], rest[n_comm:]
        (dq_ref, dk_ref, dv_ref, dfk_ref, dfq_ref), rest = rest[:5], rest[5:]
        arrived, rest = rest[:n_comm], rest[n_comm:]
        (dk_sc, dv_sc, dfk_sc, dq_sc, dfq_sc, delta_sc), sems = rest[:6], rest[6:]
        head, step = pl.program_id(0), pl.program_id(1)
        kj, qi, fl = kt[step], qt[step], ft[step]

        if n_comm:
            @pl.when((head == 0) & (step == 0))
            def _():
                for cp in _scatter_copies(pair_refs, arrived, sems):
                    cp.start()

            @pl.when((head == N_HEADS - 1) & (step == n_steps - 1))
            def _():
                for cp in _scatter_copies(pair_refs, arrived, sems):
                    cp.wait()

        @pl.when((fl & HEAD_FIRST) != 0)
        def _():
            dq_sc[...] = jnp.zeros_like(dq_sc)
            dfq_sc[...] = jnp.zeros_like(dfq_sc)

        @pl.when((fl & FIRST) != 0)
        def _():
            dk_sc[...] = jnp.zeros_like(dk_sc)
            dv_sc[...] = jnp.zeros_like(dv_sc)
            dfk_sc[...] = jnp.zeros_like(dfk_sc)

        @pl.when((fl & KEY_ZERO) != 0)
        def _():
            delta_sc[qi] = jnp.sum(dot_ref[...] * ot_ref[...], axis=0, keepdims=True)

        def update(diagonal):
            pt = jnp.exp(_scores_t(q_ref[...], k_ref[...], fq_ref[...], fk_ref[...], diagonal, NEG) - lse_ref[...])
            dv_sc[...] += _dot(pt, do_ref[...])
            dst = pt * (_dot_nt(v_ref[...], do_ref[...]) - delta_sc[qi])
            dk_sc[...] += _dot(dst, q_ref[...])
            dfk_sc[...] += jnp.sum(dst, axis=1, keepdims=True)
            dfq_sc[qi] = dfq_sc[qi] + jnp.sum(dst, axis=0, keepdims=True)
            dq_sc[qi] = dq_sc[qi] + _dot(kt_ref[...], dst)

        _on_block_kind(qi, kj, update)

        @pl.when((fl & LAST) != 0)
        def _():
            dk_ref[...] = (dk_sc[...] * ATTN_SCALE).astype(dk_ref.dtype)
            dv_ref[...] = dv_sc[...].astype(dv_ref.dtype)
            dfk_ref[...] = -dfk_sc[...]

        @pl.when((fl & HEAD_LAST) != 0)
        def _():
            for i in range(n):
                dq_ref[:, i * t:(i + 1) * t] = (dq_sc[i] * ATTN_SCALE).astype(dq_ref.dtype)
                dfq_ref[:, i * t:(i + 1) * t] = dfq_sc[i]

    def qs(block):
        return pl.BlockSpec((t, 128), lambda h, s, kt, qt, ft: (qt[s], block + h))

    def kv(block):
        return pl.BlockSpec((t, 128), lambda h, s, kt, qt, ft: (kt[s], block + h))

    first_sweep = pl.BlockSpec((128, t), lambda h, s, kt, qt, ft: (h, jnp.where(kt[s] == 0, qt[s], n - 1)))
    qrow = pl.BlockSpec((None, 1, t), lambda h, s, kt, qt, ft: (h, 0, qt[s]))
    kcol = pl.BlockSpec((None, t, 1), lambda h, s, kt, qt, ft: (h, kt[s], 0))
    out = pl.BlockSpec((t, 128), lambda h, s, kt, qt, ft: (kt[s], h))
    return _flash_call(
        body, "flash_bwd", sched,
        [qs(qb), kv(kb), kv(vb), pl.BlockSpec((128, t), lambda h, s, kt, qt, ft: (h, kt[s])), qs(dob), first_sweep,
         first_sweep, qrow, qrow, kcol],
        (pl.BlockSpec((128, S), lambda h, s, kt, qt, ft: (h, 0)), out, out, kcol,
         pl.BlockSpec((None, 1, S), lambda h, s, kt, qt, ft: (h, 0, 0))),
        (jax.ShapeDtypeStruct((ATTN_WIDTH, S), BF16), jax.ShapeDtypeStruct((S, ATTN_WIDTH), BF16),
         jax.ShapeDtypeStruct((S, ATTN_WIDTH), BF16), jax.ShapeDtypeStruct((N_HEADS, S, 1), F32),
         jax.ShapeDtypeStruct((N_HEADS, 1, S), F32)),
        [pltpu.VMEM((t, 128), F32), pltpu.VMEM((t, 128), F32), pltpu.VMEM((t, 1), F32), pltpu.VMEM((n, 128, t), F32),
         pltpu.VMEM((n, 1, t), F32), pltpu.VMEM((n, 1, t), F32)],
        comm_ins=n_comm, comm_shapes=_scatter_shapes(pairs) if n_comm else (), comm_sems=SCATTER_SEMS if n_comm else (),
    )(*sched, z, z, z, k_t, dy, do_t, o_t, lse_row, f_row, f_col, *(pairs or ()))


LRU_HALO = 8


def _lru_gates(ext, cw_ref, cb_ref, wa_ref, ba_ref, wi_ref, bi_ref, lam_ref, tr):
    taps = [pltpu.roll(ext, 3 - k, 0)[LRU_HALO:] if k < 3 else ext[LRU_HALO:] for k in range(4)]
    xc = cb_ref[...] + taps[0] * cw_ref[0:1, :]
    for k in range(1, 4):
        xc = xc + taps[k] * cw_ref[k:k + 1, :]
    ga = jnp.concatenate([_dot(xc[:, g * 128:(g + 1) * 128], wa_ref[g]) for g in range(4)], axis=1) + ba_ref[...]
    gi = jnp.concatenate([_dot(xc[:, g * 128:(g + 1) * 128], wi_ref[g]) for g in range(4)], axis=1) + bi_ref[...]
    r, ig = _sigmoid(ga), _sigmoid(gi)
    nl = -lam_ref[...]
    sp = jnp.maximum(nl, 0.0) + jnp.log(1.0 + jnp.exp(-jnp.abs(nl)))
    la = -LRU_C * r * sp
    a = jnp.exp(la)
    mult = jnp.sqrt(_neg_expm1(2.0 * la))
    return xc, r, ig, sp, a, mult, taps


def _lru_specs(tr, nt, rev):
    hb = tr // LRU_HALO
    ti = (lambda i: nt - 1 - i) if rev else (lambda i: i)
    zx_b, zy_b = ZX // LRU_WIDTH, ZY // LRU_WIDTH
    cur = lambda b: pl.BlockSpec((tr, LRU_WIDTH), lambda i: (ti(i), b))
    prev = lambda b: pl.BlockSpec((LRU_HALO, LRU_WIDTH), lambda i: (jnp.maximum(ti(i) * hb - 1, 0), b))
    vec = pl.BlockSpec((1, LRU_WIDTH), lambda i: (0, 0))
    cw = pl.BlockSpec((4, LRU_WIDTH), lambda i: (0, 0))
    blk = pl.BlockSpec((4, 128, 128), lambda i: (0, 0, 0))
    return ti, cur, prev, vec, cw, blk, zx_b, zy_b


def _lru_fwd(z, cw, cb, wa, ba, wi, bi, lam):
    S = z.shape[0]
    tr = _div_tile(S, TILES["lru"], 8)
    nt = S // tr
    ti, cur, prev, vec, cwspec, blk, zx_b, zy_b = _lru_specs(tr, nt, False)

    def body(zx_ref, halo_ref, zy_ref, cw_ref, cb_ref, wa_ref, ba_ref, wi_ref, bi_ref, lam_ref, y_ref, h_ref, carry):
        i = pl.program_id(0)

        @pl.when(i == 0)
        def _():
            carry[...] = jnp.zeros_like(carry)

        ext = jnp.concatenate([jnp.where(i > 0, halo_ref[...], 0.0), zx_ref[...]], axis=0)
        xc, r, ig, sp, a, mult, _ = _lru_gates(ext, cw_ref, cb_ref, wa_ref, ba_ref, wi_ref, bi_ref, lam_ref, tr)
        A, B = a, mult * (ig * xc)
        row = _rows(tr, LRU_WIDTH)
        d = 1
        while d < tr:
            a_sh = jnp.where(row >= d, pltpu.roll(A, d, 0), 1.0)
            b_sh = jnp.where(row >= d, pltpu.roll(B, d, 0), 0.0)
            B = A * b_sh + B
            A = A * a_sh
            d *= 2
        h = B + A * carry[...]
        h_ref[...] = h
        carry[...] = h_ref[pl.ds(tr - 1, 1), :]
        y_ref[...] = h * _gelu_parts(zy_ref[...])[0]

    out = pl.BlockSpec((tr, LRU_WIDTH), lambda i: (i, 0))
    shape = jax.ShapeDtypeStruct((S, LRU_WIDTH), F32)
    return pl.pallas_call(
        body, name="lru_fwd", grid=(nt,),
        in_specs=[cur(zx_b), prev(zx_b), cur(zy_b), cwspec, vec, blk, vec, blk, vec, vec],
        out_specs=(out, out), out_shape=(shape, shape), scratch_shapes=[pltpu.VMEM((1, LRU_WIDTH), F32)],
        compiler_params=_params(("arbitrary",)),
    )(z, z, z, cw, cb, wa, ba, wi, bi, lam)


def _lru_bwd(z, dy, hs, cw, cb, wa, ba, wi, bi, lam):
    S = z.shape[0]
    tr = _div_tile(S, TILES["lru"], 8)
    nt = S // tr
    ti, cur, prev, vec, cwspec, blk, zx_b, zy_b = _lru_specs(tr, nt, True)
    dy_b = (POOL_WIDTH + ATTN_WIDTH) // LRU_WIDTH

    def body(zx_ref, halo_ref, zy_ref, dy_ref, h_ref, hprev_ref, cw_ref, cb_ref, wa_ref, ba_ref, wi_ref, bi_ref, lam_ref,
             dzx_ref, dzy_ref, dcw_ref, dcb_ref, dwa_ref, dba_ref, dwi_ref, dbi_ref, dlam_ref, gcarry, dxc_next, tmp):
        i = pl.program_id(0)
        t_idx = nt - 1 - i

        @pl.when(i == 0)
        def _():
            gcarry[...] = jnp.zeros_like(gcarry)
            dxc_next[...] = jnp.zeros_like(dxc_next)
            for ref in (dcw_ref, dcb_ref, dwa_ref, dba_ref, dwi_ref, dbi_ref, dlam_ref):
                ref[...] = jnp.zeros_like(ref)

        ext = jnp.concatenate([jnp.where(t_idx > 0, halo_ref[...], 0.0), zx_ref[...]], axis=0)
        xc, r, ig, sp, a, mult, taps = _lru_gates(ext, cw_ref, cb_ref, wa_ref, ba_ref, wi_ref, bi_ref, lam_ref, tr)
        h = h_ref[...]
        gel, dgel = _gelu_parts(zy_ref[...])
        dy_ = dy_ref[...]
        dzy_ref[...] = (dy_ * h * dgel).astype(dzy_ref.dtype)
        row = _rows(tr, LRU_WIDTH)
        B = dy_ * gel + jnp.where(row == tr - 1, gcarry[...], 0.0)
        A = jnp.where(row < tr - 1, pltpu.roll(a, tr - 1, 0), 0.0)
        d = 1
        while d < tr:
            keep = row < tr - d
            b_sh = jnp.where(keep, pltpu.roll(B, tr - d, 0), 0.0)
            a_sh = jnp.where(keep, pltpu.roll(A, tr - d, 0), 0.0)
            B = B + A * b_sh
            A = A * a_sh
            d *= 2
        g = B
        tmp[...] = a * g
        gcarry[...] = tmp[pl.ds(0, 1), :]
        h_ext = jnp.concatenate([jnp.where(t_idx > 0, hprev_ref[...], 0.0), h], axis=0)
        hprev = pltpu.roll(h_ext, 1, 0)[LRU_HALO:]
        t1 = g * mult
        dig = t1 * xc
        dxc = t1 * ig
        dla = (g * hprev) * a - (g * (ig * xc)) * (a * a) / mult
        dr = dla * (-LRU_C * sp)
        dga = dr * r * (1.0 - r)
        dgi = dig * ig * (1.0 - ig)
        dlam_ref[...] += _colsum(dla * (-LRU_C * r)) * (-_sigmoid(-lam_ref[...]))
        dba_ref[...] += _colsum(dga)
        dbi_ref[...] += _colsum(dgi)
        parts = []
        for gidx in range(4):
            cols = slice(gidx * 128, (gidx + 1) * 128)
            xct = xc[:, cols].T
            dwa_ref[gidx] += _dot(xct, dga[:, cols])
            dwi_ref[gidx] += _dot(xct, dgi[:, cols])
            parts.append(_dot_nt(dga[:, cols], wa_ref[gidx]) + _dot_nt(dgi[:, cols], wi_ref[gidx]))
        dxc = dxc + jnp.concatenate(parts, axis=1)
        dcb_ref[...] += _colsum(dxc)
        for k in range(4):
            dcw_ref[k:k + 1, :] += _colsum(dxc * taps[k])
        ext_d = jnp.concatenate([dxc, dxc_next[...]], axis=0)
        dzx = dxc * cw_ref[3:4, :]
        for k in range(3):
            dzx = dzx + pltpu.roll(ext_d, tr + LRU_HALO - (3 - k), 0)[:tr] * cw_ref[k:k + 1, :]
        dzx_ref[...] = dzx.astype(dzx_ref.dtype)
        dxc_next[...] = dxc[:LRU_HALO]

    rev = pl.BlockSpec((tr, LRU_WIDTH), lambda i: (nt - 1 - i, 0))
    hb = tr // LRU_HALO
    hprev_spec = pl.BlockSpec((LRU_HALO, LRU_WIDTH), lambda i: (jnp.maximum((nt - 1 - i) * hb - 1, 0), 0))
    dy_spec = pl.BlockSpec((tr, LRU_WIDTH), lambda i: (nt - 1 - i, dy_b))
    vshape = jax.ShapeDtypeStruct((1, LRU_WIDTH), F32)
    bshape = jax.ShapeDtypeStruct((4, 128, 128), F32)
    return pl.pallas_call(
        body, name="lru_bwd", grid=(nt,),
        in_specs=[cur(zx_b), prev(zx_b), cur(zy_b), dy_spec, rev, hprev_spec, cwspec, vec, blk, vec, blk, vec, vec],
        out_specs=(rev, rev, cwspec, vec, blk, vec, blk, vec, vec),
        out_shape=(jax.ShapeDtypeStruct((S, LRU_WIDTH), BF16), jax.ShapeDtypeStruct((S, LRU_WIDTH), BF16),
                   jax.ShapeDtypeStruct((4, LRU_WIDTH), F32), vshape, bshape, vshape, bshape, vshape, vshape),
        scratch_shapes=[pltpu.VMEM((1, LRU_WIDTH), F32), pltpu.VMEM((LRU_HALO, LRU_WIDTH), F32),
                        pltpu.VMEM((tr, LRU_WIDTH), F32)],
        compiler_params=_params(("arbitrary",)),
    )(z, z, z, dy, hs, hs, cw, cb, wa, ba, wi, bi, lam)


FFN_HALO = 8


def _ffn_act(au, cw, cb):
    S, F2 = au.shape
    F = F2 // 2
    tc = FFN_CHUNK
    tr = _div_tile(S, TILES["row"], 8)
    hb = tr // FFN_HALO

    def body(au_ref, halo_ref, cw_ref, cb_ref, p_ref):
        i = pl.program_id(0)
        a_ = au_ref[:, :tc]
        ext = jnp.concatenate([jnp.where(i > 0, halo_ref[:, :tc], 0.0), a_], axis=0)
        gc = cb_ref[...] + a_ * cw_ref[2:3, :]
        for k in range(2):
            gc = gc + pltpu.roll(ext, 2 - k, 0)[FFN_HALO:] * cw_ref[k:k + 1, :]
        p_ref[...] = (gc * _sigmoid(gc) * au_ref[:, tc:]).astype(p_ref.dtype)

    return pl.pallas_call(
        body, name="ffn_act", grid=(S // tr, F // tc),
        in_specs=[pl.BlockSpec((tr, 2 * tc), lambda i, j: (i, j)),
                  pl.BlockSpec((FFN_HALO, 2 * tc), lambda i, j: (jnp.maximum(i * hb - 1, 0), j)),
                  pl.BlockSpec((3, tc), lambda i, j: (0, j)), pl.BlockSpec((1, tc), lambda i, j: (0, j))],
        out_specs=pl.BlockSpec((tr, tc), lambda i, j: (i, j)), out_shape=jax.ShapeDtypeStruct((S, F), BF16),
        compiler_params=_params(("parallel", "parallel")),
    )(au, au, cw, cb)


def _ffn_act_bwd(au, dp, cw, cb):
    S, F2 = au.shape
    F = F2 // 2
    tc = FFN_CHUNK
    tr = _div_tile(S, TILES["ew"], 8)
    hb = tr // FFN_HALO
    nt = S // tr
    H = FFN_HALO

    def body(au_ref, prev_ref, next_ref, dp_ref, dpn_ref, cw_ref, cb_ref, dau_ref, dcw_ref, dcb_ref):
        i = pl.program_id(1)
        last = i == nt - 1
        a_ext = jnp.concatenate([jnp.where(i > 0, prev_ref[:, :tc], 0.0), au_ref[:, :tc], next_ref[:, :tc]], axis=0)
        u_ext = jnp.concatenate([au_ref[:, tc:], next_ref[:, tc:]], axis=0)
        dp_ext = jnp.concatenate([dp_ref[...], jnp.where(last, 0.0, dpn_ref[...])], axis=0)
        taps = [pltpu.roll(a_ext, 2 - k, 0)[H:] if k < 2 else a_ext[H:] for k in range(3)]
        gc = cb_ref[...] + taps[0] * cw_ref[0:1, :] + taps[1] * cw_ref[1:2, :] + taps[2] * cw_ref[2:3, :]
        sig = _sigmoid(gc)
        dgc = dp_ext * u_ext * (sig * (1.0 + gc * (1.0 - sig)))
        da = dgc[:tr] * cw_ref[2:3, :]
        for k in range(2):
            da = da + pltpu.roll(dgc, tr + H - (2 - k), 0)[:tr] * cw_ref[k:k + 1, :]
        dau_ref[:, :tc] = da.astype(dau_ref.dtype)
        dau_ref[:, tc:] = (dp_ref[...] * (gc[:tr] * sig[:tr])).astype(dau_ref.dtype)

        @pl.when(i == 0)
        def _():
            dcw_ref[...] = jnp.zeros_like(dcw_ref)
            dcb_ref[...] = jnp.zeros_like(dcb_ref)

        dcb_ref[...] += _colsum(dgc[:tr])
        for k in range(3):
            dcw_ref[k:k + 1, :] += _colsum(dgc[:tr] * taps[k][:tr])

    return pl.pallas_call(
        body, name="ffn_act_bwd", grid=(F // tc, nt),
        in_specs=[pl.BlockSpec((tr, 2 * tc), lambda j, i: (i, j)),
                  pl.BlockSpec((H, 2 * tc), lambda j, i: (jnp.maximum(i * hb - 1, 0), j)),
                  pl.BlockSpec((H, 2 * tc), lambda j, i: (jnp.minimum((i + 1) * hb, nt * hb - 1), j)),
                  pl.BlockSpec((tr, tc), lambda j, i: (i, j)),
                  pl.BlockSpec((H, tc), lambda j, i: (jnp.minimum((i + 1) * hb, nt * hb - 1), j)),
                  pl.BlockSpec((3, tc), lambda j, i: (0, j)), pl.BlockSpec((1, tc), lambda j, i: (0, j))],
        out_specs=(pl.BlockSpec((tr, 2 * tc), lambda j, i: (i, j)), pl.BlockSpec((3, tc), lambda j, i: (0, j)),
                   pl.BlockSpec((1, tc), lambda j, i: (0, j))),
        out_shape=(jax.ShapeDtypeStruct((S, F2), BF16), jax.ShapeDtypeStruct((3, F), F32), jax.ShapeDtypeStruct((1, F), F32)),
        compiler_params=_params(("parallel", "arbitrary")),
    )(au, au, au, dp, dp, cw, cb)


def _row_tile(rows, cols):
    return _div_tile(rows, max(16, (2**18 // cols) // 16 * 16), 16)


def _sum_parts(parts, sel, rows, cols, name):
    tr = _row_tile(rows, cols)

    def body(sel_ref, *refs):
        acc = refs[0][...].astype(F32)
        for r in refs[1:-1]:
            acc = acc + r[...].astype(F32)
        refs[-1][...] = acc

    def spec(index):
        if isinstance(index, int):
            return pl.BlockSpec((None, tr, cols), lambda i, s: (index, i, 0))
        k, mul, off = index
        return pl.BlockSpec((None, tr, cols), lambda i, s: (s[k] * mul + off, i, 0))

    grid_spec = pltpu.PrefetchScalarGridSpec(
        num_scalar_prefetch=1, grid=(rows // tr,), in_specs=[spec(ix) for _, ix in parts],
        out_specs=pl.BlockSpec((tr, cols), lambda i, s: (i, 0)))
    return pl.pallas_call(
        body, name=name, grid_spec=grid_spec, out_shape=jax.ShapeDtypeStruct((rows, cols), F32),
        compiler_params=_params(("parallel",)),
    )(sel, *[a for a, _ in parts])


def _pair_sum(g, got, sel):
    n, rows, cols = got.shape
    tr = _row_tile(rows, cols)
    nb = rows // tr

    def body(sel_ref, a_ref, b_ref, o_ref):
        o_ref[...] = (a_ref[...] + b_ref[...]).astype(o_ref.dtype)

    grid_spec = pltpu.PrefetchScalarGridSpec(
        num_scalar_prefetch=1, grid=(n, nb),
        in_specs=[pl.BlockSpec((None, tr, cols), lambda q, i, s: (q, s[0] * nb + i, 0)),
                  pl.BlockSpec((None, tr, cols), lambda q, i, s: (q, i, 0))],
        out_specs=pl.BlockSpec((None, tr, cols), lambda q, i, s: (q, i, 0)))
    return pl.pallas_call(
        body, name="pair_sum", grid_spec=grid_spec, out_shape=jax.ShapeDtypeStruct((n, rows, cols), BF16),
        compiler_params=_params(("parallel", "parallel")),
    )(sel, g, got)


def _chip_sum_cols(pair, arrived, sel):
    n, rows, cg = arrived.shape[1:]
    tr = _row_tile(rows, cg)

    def body(sel_ref, p_ref, a0, a1, a2, o_ref):
        o_ref[...] = ((p_ref[...].astype(F32) + a0[...].astype(F32)) + a1[...].astype(F32)) + a2[...].astype(F32)

    def arr(j):
        return pl.BlockSpec((None, None, tr, cg), lambda k, i, s: (j, k, i, 0))

    grid_spec = pltpu.PrefetchScalarGridSpec(
        num_scalar_prefetch=1, grid=(n, rows // tr),
        in_specs=[pl.BlockSpec((None, tr, cg), lambda k, i, s: (k, i, s[1])), arr(0), arr(1), arr(2)],
        out_specs=pl.BlockSpec((None, tr, cg), lambda k, i, s: (k, i, 0)))
    return pl.pallas_call(
        body, name="chip_sum_cols", grid_spec=grid_spec, out_shape=jax.ShapeDtypeStruct((n, rows, cg), F32),
        compiler_params=_params(("parallel", "parallel")),
    )(sel, pair, arrived, arrived, arrived)


def _adamw_math(w, g, m, v):
    m2 = ADAM_B1 * m + (1.0 - ADAM_B1) * g
    v2 = ADAM_B2 * v + (1.0 - ADAM_B2) * (g * g)
    m_hat = m2 / (1.0 - ADAM_B1 ** ADAM_STEP)
    v_hat = v2 / (1.0 - ADAM_B2 ** ADAM_STEP)
    return -ADAM_LR * (m_hat / (jnp.sqrt(v_hat) + ADAM_EPS) + ADAM_WD * w), m2, v2


def _adamw(w, g, m, v):
    R, C = w.shape
    tr = _div_tile(R, TILES["ew"], 8)

    def body(w_ref, g_ref, m_ref, v_ref, d_ref, m2_ref, v2_ref):
        d_ref[...], m2_ref[...], v2_ref[...] = _adamw_math(w_ref[...], g_ref[...], m_ref[...], v_ref[...])

    spec = pl.BlockSpec((tr, C), lambda i: (i, 0))
    shape = jax.ShapeDtypeStruct((R, C), F32)
    return pl.pallas_call(
        body, name="adamw", grid=(R // tr,), in_specs=[spec] * 4, out_specs=(spec,) * 3, out_shape=(shape,) * 3,
        compiler_params=_params(("parallel",)),
    )(w, g, m, v)


def _ada_grad_adamw(cact_t, dmod, w, m, v):
    L, D, N = w.shape
    tm, tn = _div_tile(D, 256, 8), _div_tile(N, 1024, 128)

    def body(c_ref, d_ref, w_ref, m_ref, v_ref, g_ref, dl_ref, m2_ref, v2_ref):
        g = c_ref[:, 0:1] * d_ref[0:1, :]
        for b in range(1, N_DEV):
            g = g + c_ref[:, b:b + 1] * d_ref[b:b + 1, :]
        g_ref[...] = g
        dl_ref[...], m2_ref[...], v2_ref[...] = _adamw_math(w_ref[...], g, m_ref[...], v_ref[...])

    big = pl.BlockSpec((None, tm, tn), lambda l, i, j: (l, i, j))
    shape = jax.ShapeDtypeStruct((L, D, N), F32)
    return pl.pallas_call(
        body, name="ada_grad_adamw", grid=(L, D // tm, N // tn),
        in_specs=[pl.BlockSpec((tm, N_DEV), lambda l, i, j: (i, 0)),
                  pl.BlockSpec((None, N_DEV, tn), lambda l, i, j: (l, 0, j)), big, big, big],
        out_specs=(big,) * 4, out_shape=(shape,) * 4, compiler_params=_params(("parallel", "parallel", "parallel")),
    )(cact_t, dmod, w, m, v)


def _silu_rows(c):
    def body(c_ref, o_ref):
        x = c_ref[...]
        o_ref[...] = x * _sigmoid(x)

    return pl.pallas_call(body, name="silu_rows", out_shape=jax.ShapeDtypeStruct(c.shape, F32))(c)


ANY = pl.BlockSpec(memory_space=pl.ANY)


def _position():
    return lax.axis_index("x"), lax.axis_index("y"), lax.axis_index("c")


def _other_chips(x, y):
    return [(1 - x, y), (x, 1 - y), (1 - x, 1 - y)]


def _allgather8(v):
    R, C = v.shape

    def body(v_ref, out_ref, send_sems, recv_sems, local_sem):
        x, y, c = _position()
        me = 4 * x + 2 * y + c
        mine = pltpu.make_async_copy(v_ref, out_ref.at[me], local_sem)
        mine.start()
        sends, recvs = [], []
        for k in range(1, N_DEV):
            px, py, pc = (x + (k >> 2)) % 2, (y + ((k >> 1) & 1)) % 2, (c + (k & 1)) % 2
            sends.append(pltpu.make_async_remote_copy(
                src_ref=v_ref, dst_ref=out_ref.at[me], send_sem=send_sems.at[k - 1], recv_sem=recv_sems.at[k - 1],
                device_id=(px, py, pc), device_id_type=MESH))
            recvs.append(pltpu.make_async_remote_copy(
                src_ref=v_ref, dst_ref=out_ref.at[4 * px + 2 * py + pc], send_sem=send_sems.at[k - 1],
                recv_sem=recv_sems.at[k - 1], device_id=(px, py, pc), device_id_type=MESH))
        for cp in sends:
            cp.start()
        for cp in recvs:
            cp.wait_recv()
        for cp in sends:
            cp.wait_send()
        mine.wait()

    return pl.pallas_call(
        body, name="comm_allgather8", out_shape=jax.ShapeDtypeStruct((N_DEV, R, C), v.dtype), in_specs=[ANY],
        out_specs=ANY,
        scratch_shapes=[pltpu.SemaphoreType.DMA((N_DEV - 1,)), pltpu.SemaphoreType.DMA((N_DEV - 1,)),
                        pltpu.SemaphoreType.DMA],
    )(v)


def _remote(src, dst, send_sems, recv_sems, k, to):
    return pltpu.make_async_remote_copy(src_ref=src, dst_ref=dst, send_sem=send_sems.at[k], recv_sem=recv_sems.at[k],
                                        device_id=to, device_id_type=MESH)


def _half_rows(ref, h):
    n = ref.shape[0] // 2
    return ref.at[pl.ds(h * n, n)]


N_SHARDS = 5


class _GatherPlan:
    def __init__(self, ins, outs, sems):
        win, wout, gate, up, down = ins
        win4, wout4, gu, down4 = outs
        send_sems, recv_sems, local_sems = sems
        x, y, c = _position()
        q = 2 * x + y
        sibling = (x, y, 1 - c)
        CG = gate.shape[1]
        tensors = [(win, lambda p: win4.at[p]), (wout, lambda p: wout4.at[p]),
                   (gate, lambda p: gu.at[0, :, pl.ds(p * CG, CG)]), (up, lambda p: gu.at[1, :, pl.ds(p * CG, CG)]),
                   (down, lambda p: down4.at[p])]
        self.own, self.first, self.landed, self.passed, self.passed_landed = [], [], [], [], []
        for t, (src, dst) in enumerate(tensors):
            self.own.append(pltpu.make_async_copy(src, dst(q), local_sems.at[t]))
            for j, (px, py) in enumerate(_other_chips(x, y)):
                self.first.append(_remote(_half_rows(src, c), _half_rows(dst(q), c), send_sems, recv_sems, 6 * t + j,
                                          (px, py, c)))
                mine = _half_rows(dst(2 * px + py), c)
                self.landed.append(_remote(mine, mine, send_sems, recv_sems, 6 * t + j, (px, py, c)))
                self.passed.append(_remote(mine, mine, send_sems, recv_sems, 6 * t + 3 + j, sibling))
                other = _half_rows(dst(2 * px + py), 1 - c)
                self.passed_landed.append(_remote(other, other, send_sems, recv_sems, 6 * t + 3 + j, sibling))

    def start(self):
        for cp in self.own + self.first:
            cp.start()

    def forward(self):
        for landed, onward in zip(self.landed, self.passed):
            landed.wait_recv()
            onward.start()

    def finish(self):
        for cp in self.passed_landed:
            cp.wait_recv()
        for cp in self.first + self.passed:
            cp.wait_send()
        for cp in self.own:
            cp.wait()


def _gather_shapes(shards):
    win_s, wout_s, gate_s, up_s, down_s = shards
    return (jax.ShapeDtypeStruct((N_CHIPS,) + win_s.shape, BF16), jax.ShapeDtypeStruct((N_CHIPS,) + wout_s.shape, BF16),
            jax.ShapeDtypeStruct((2, gate_s.shape[0], N_CHIPS * gate_s.shape[1]), BF16),
            jax.ShapeDtypeStruct((N_CHIPS,) + down_s.shape, BF16))


GATHER_SEMS = [pltpu.SemaphoreType.DMA((6 * N_SHARDS,)), pltpu.SemaphoreType.DMA((6 * N_SHARDS,)),
               pltpu.SemaphoreType.DMA((N_SHARDS,))]


def _gather_weights(shards):
    def body(*refs):
        plan = _GatherPlan(refs[:N_SHARDS], refs[N_SHARDS:N_SHARDS + 4], refs[N_SHARDS + 4:])
        plan.start()
        plan.forward()
        plan.finish()

    return pl.pallas_call(
        body, name="comm_gather_weights", out_shape=_gather_shapes(shards), in_specs=[ANY] * N_SHARDS,
        out_specs=(ANY,) * 4, scratch_shapes=GATHER_SEMS,
    )(*shards)


def _sibling_swap_halves(gs):
    n_t = len(gs)

    def body(*refs):
        ins, outs, (send_sems, recv_sems) = refs[:n_t], refs[n_t:2 * n_t], refs[2 * n_t:]
        x, y, c = _position()
        cps = []
        for t in range(n_t):
            half = ins[t].shape[1] // 2
            cps.append(_remote(ins[t].at[:, pl.ds((1 - c) * half, half), :], outs[t], send_sems, recv_sems, t,
                               (x, y, 1 - c)))
        for cp in cps:
            cp.start()
        for cp in cps:
            cp.wait()

    shapes = tuple(jax.ShapeDtypeStruct((g.shape[0], g.shape[1] // 2, g.shape[2]), g.dtype) for g in gs)
    return pl.pallas_call(
        body, name="comm_sibling_swap", out_shape=shapes, in_specs=[ANY] * n_t, out_specs=(ANY,) * n_t,
        scratch_shapes=[pltpu.SemaphoreType.DMA((n_t,)), pltpu.SemaphoreType.DMA((n_t,))],
    )(*gs)


N_PAIRS = 4
SCATTER_COPIES = 3 * (N_PAIRS + 1)
SCATTER_SEMS = [pltpu.SemaphoreType.DMA((SCATTER_COPIES,)), pltpu.SemaphoreType.DMA((SCATTER_COPIES,))]


def _scatter_copies(ins, outs, sems):
    send_sems, recv_sems = sems
    n_l = N_PAIRS - 1
    CG = ins[n_l].shape[2] // N_CHIPS
    x, y, c = _position()
    cps = []
    for j, (px, py) in enumerate(_other_chips(x, y)):
        p = 2 * px + py
        for t in range(n_l):
            cps.append(_remote(ins[t].at[p], outs[t].at[j], send_sems, recv_sems, len(cps), (px, py, c)))
        for k in range(2):
            cps.append(_remote(ins[n_l].at[k, :, pl.ds(p * CG, CG)], outs[n_l].at[j, k], send_sems, recv_sems, len(cps),
                               (px, py, c)))
    return cps


def _scatter_shapes(pairs):
    gu = pairs[-1]
    return tuple(jax.ShapeDtypeStruct((3,) + g.shape[1:], g.dtype) for g in pairs[:-1]) + (
        jax.ShapeDtypeStruct((3, 2, gu.shape[1], gu.shape[2] // N_CHIPS), gu.dtype),)


def _chip_scatter(pairs):
    def body(*refs):
        cps = _scatter_copies(refs[:N_PAIRS], refs[N_PAIRS:2 * N_PAIRS], refs[2 * N_PAIRS:])
        for cp in cps:
            cp.start()
        for cp in cps:
            cp.wait()

    return pl.pallas_call(
        body, name="comm_chip_scatter", out_shape=_scatter_shapes(pairs), in_specs=[ANY] * N_PAIRS,
        out_specs=(ANY,) * N_PAIRS, scratch_shapes=SCATTER_SEMS,
    )(*pairs)


def _sibling_allgather(vs):
    n_t = len(vs)

    def body(*refs):
        ins, outs, (send_sems, recv_sems, local_sems) = refs[:n_t], refs[n_t:2 * n_t], refs[2 * n_t:]
        x, y, c = _position()
        cps, own, recvs = [], [], []
        for t in range(n_t):
            R = ins[t].shape[-2]
            rows = lambda h: outs[t].at[:, pl.ds(h * R, R), :] if len(ins[t].shape) == 3 else outs[t].at[pl.ds(h * R, R)]
            own.append(pltpu.make_async_copy(ins[t], rows(c), local_sems.at[t]))
            cps.append(_remote(ins[t], rows(c), send_sems, recv_sems, t, (x, y, 1 - c)))
            recvs.append(_remote(ins[t], rows(1 - c), send_sems, recv_sems, t, (x, y, 1 - c)))
        for cp in own + cps:
            cp.start()
        for cp in recvs:
            cp.wait_recv()
        for cp in cps:
            cp.wait_send()
        for cp in own:
            cp.wait()

    shapes = tuple(jax.ShapeDtypeStruct(v.shape[:-2] + (2 * v.shape[-2], v.shape[-1]), v.dtype) for v in vs)
    return pl.pallas_call(
        body, name="comm_sibling_allgather", out_shape=shapes, in_specs=[ANY] * n_t, out_specs=(ANY,) * n_t,
        scratch_shapes=[pltpu.SemaphoreType.DMA((n_t,)), pltpu.SemaphoreType.DMA((n_t,)), pltpu.SemaphoreType.DMA((n_t,))],
    )(*vs)


def _reduce_start(grads, sel):
    got = _sibling_swap_halves(grads)
    return [_pair_sum(g, r, sel) for g, r in zip(grads, got)]


def _reduce_finish(pairs, arrived, sel):
    mine = [_sum_parts([(p, (1, 1, 0)), (a, 0), (a, 1), (a, 2)], sel, p.shape[1], p.shape[2], "chip_sum")
            for p, a in zip(pairs[:-1], arrived[:-1])]
    mine.append(_chip_sum_cols(pairs[-1], arrived[-1], sel))
    return _sibling_allgather(mine)


def _layer_fwd(x, mod, p, next_shards=None):
    sh1, sc1, gt1, sh2, sc2, gt2 = mod
    h1 = _norm_mod(x, p["g_mix"], sc1, sh1)
    z = _matmul(h1, p["w_in"], name="mm_in")
    y_pool = _pool_fwd(z, p["pool_w"], p["pool_scale"])
    F = _forget_cumsum(z, p["b_f"])
    Fh = F[:, :N_HEADS].T
    f_col, f_row = Fh[:, :, None], Fh[:, None, :]
    o_t, lse, *gathered = _flash_fwd(z, z[:, ZV:ZV + ATTN_WIDTH].T.astype(BF16), f_col, f_row, next_shards)
    y_lru, hs = _lru_fwd(z, p["lru_conv_w"], p["lru_conv_b"], p["lru_wa"], p["lru_ba"], p["lru_wi"], p["lru_bi"],
                         p["lru_lambda"])
    y = jnp.concatenate([y_pool.astype(BF16), o_t.T.astype(BF16), y_lru.astype(BF16)], axis=1)
    m1, x_mid = _matmul(y, p["w_out"], res=x, gate=gt1, name="mm_out")
    h2 = _norm_mod(x_mid, p["g_ffn"], sc2, sh2)
    au = _matmul(h2, p["w_gu"], gu="b", name="mm_gu")
    pa = _ffn_act(au, p["ffn_conv_w"], p["ffn_conv_b"])
    m2, x_out = _matmul(pa, p["w_down"], res=x_mid, gate=gt2, name="mm_down")
    saved = dict(x=x, h1=h1, z=z, f_col=f_col, f_row=f_row, o_t=o_t, lse=lse, hs=hs, y=y, m1=m1, x_mid=x_mid, h2=h2, au=au,
                 pa=pa, m2=m2)
    return x_out, saved, gathered


def _layer_bwd(dx_out, mod, p, s, pairs=None):
    sh1, sc1, gt1, sh2, sc2, gt2 = mod
    g = {}
    dm2, dgt2 = _gate_bwd(dx_out, s["m2"], gt2)
    dpa = _matmul(dm2, p["w_down"], nt=True, name="mm_down_dx")
    g["w_down"] = _matmul(s["pa"].T, dm2, name="mm_down_dw")
    dau, g["ffn_conv_w"], g["ffn_conv_b"] = _ffn_act_bwd(s["au"], dpa, p["ffn_conv_w"], p["ffn_conv_b"])
    dh2 = _matmul(dau, p["w_gu"], nt=True, gu="b", name="mm_gu_dx")
    g["w_gu"] = _matmul(s["h2"].T, dau, gu="out", name="mm_gu_dw")
    dx_mid, g["g_ffn"], dsc2, dsh2 = _norm_mod_bwd(s["x_mid"], dh2, dx_out, p["g_ffn"], sc2)
    dm1, dgt1 = _gate_bwd(dx_mid, s["m1"], gt1)
    dy = _matmul(dm1, p["w_out"], nt=True, name="mm_out_dx")
    g["w_out"] = _matmul(s["y"].T, dm1, name="mm_out_dw")
    z = s["z"]
    (dzx, dzy, g["lru_conv_w"], g["lru_conv_b"], g["lru_wa"], g["lru_ba"], g["lru_wi"], g["lru_bi"],
     g["lru_lambda"]) = _lru_bwd(z, dy, s["hs"], p["lru_conv_w"], p["lru_conv_b"], p["lru_wa"], p["lru_ba"], p["lru_wi"],
                                 p["lru_bi"], p["lru_lambda"])
    k_t = z[:, ZK:ZK + ATTN_WIDTH].T.astype(BF16)
    do_t = dy[:, POOL_WIDTH:POOL_WIDTH + ATTN_WIDTH].T
    dq_t, dk, dv, dfk, dfq, *arrived = _flash_bwd(z, k_t, dy, do_t, s["o_t"], s["lse"], s["f_col"], s["f_row"], pairs)
    dq = dq_t.T
    dF_pad = jnp.pad((dfq[:, 0, :] + dfk[:, :, 0]).T, ((0, 0), (0, 128 - N_HEADS)))
    dzf, db_f = _forget_cumsum_bwd(z, p["b_f"], dF_pad)
    g["b_f"] = db_f[:, :N_HEADS]
    dzp, g["pool_w"], g["pool_scale"] = _pool_bwd(z, dy, p["pool_w"], p["pool_scale"])
    S = z.shape[0]
    dz = jnp.concatenate([dzp, dq, dk, dv, dzx, dzy, dzf, jnp.zeros((S, ZW - ZF - 128), BF16)], axis=1)
    dh1 = _matmul(dz, p["w_in"], nt=True, name="mm_in_dx")
    g["w_in"] = _matmul(s["h1"].T, dz, name="mm_in_dw")
    dx_in, g["g_mix"], dsc1, dsh1 = _norm_mod_bwd(s["x"], dh1, dx_mid, p["g_mix"], sc1)
    return dx_in, g, (dsh1, dsc1, dgt1, dsh2, dsc2, dgt2), arrived


def _big_weights(gathered):
    win4, wout4, gu, down4 = gathered
    D, F = win4.shape[1], gu.shape[2]
    return dict(w_in=_pad_in_cols(jnp.transpose(win4, (1, 0, 2)).reshape(D, N_IN)), w_out=wout4.reshape(D, D), w_gu=gu,
                w_down=down4.reshape(F, D))


def _big_grads(g):
    D, F = g["w_out"].shape[0], g["w_down"].shape[0]
    dwin4 = jnp.transpose(_unpad_in_cols(g["w_in"]).reshape(D, N_CHIPS, N_IN // N_CHIPS), (1, 0, 2))
    return [dwin4, g["w_out"].reshape(N_CHIPS, D // N_CHIPS, D), g["w_down"].reshape(N_CHIPS, F // N_CHIPS, D), g["w_gu"]]


def _local_step(x, target, mods, layers, final_g, shards=None, sel=None):
    L = len(layers)
    saved, params = [], []
    gathered = _gather_weights(shards[0]) if shards else None
    for l in range(L):
        p = {**layers[l], **_big_weights(gathered)} if shards else layers[l]
        x, s, gathered = _layer_fwd(x, mods[l], p, shards[l + 1] if shards and l + 1 < L else None)
        saved.append(s)
        params.append(p)
    dx, dfinal_g, loss = _loss_head(x, target, final_g)
    grads, dmods, reduced, pairs = [None] * L, [None] * L, [None] * L, None
    for l in reversed(range(L)):
        dx, grads[l], dmods[l], arrived = _layer_bwd(dx, mods[l], params[l], saved[l], pairs)
        if shards:
            if pairs is not None:
                reduced[l + 1] = _reduce_finish(pairs, arrived, sel)
            pairs = _reduce_start(_big_grads(grads[l]), sel)
    if shards:
        reduced[0] = _reduce_finish(pairs, _chip_scatter(pairs), sel)
    return loss, dx, grads, dmods, dfinal_g, reduced


def _pad_in_cols(w):
    D = w.shape[0]
    return jnp.concatenate([w[:, :3584], w[:, 3592:N_IN], w[:, 3584:3592], jnp.zeros((D, ZW - N_IN), w.dtype)], axis=1)


def _unpad_in_cols(w):
    return jnp.concatenate([w[:, :3584], w[:, ZF:ZF + N_HEADS], w[:, 3584:ZF]], axis=1)


BIG = ("w_in", "w_out", "w_ffn_gate", "w_ffn_up", "w_ffn_down")
SMALL = ("b_ada", "g_mix", "b_f", "pool_w", "pool_scale", "lru_conv_w", "lru_conv_b", "lru_wa", "lru_ba", "lru_wi",
         "lru_bi", "lru_lambda", "g_ffn", "ffn_conv_w", "ffn_conv_b", "final_g")
SHARDED_SMALL = ("lru_conv_w", "ffn_conv_w")
WEIGHTS = ("w_ada", "b_ada", "g_mix", "w_in", "b_f", "pool_w", "pool_scale", "lru_conv_w", "lru_conv_b", "lru_wa",
           "lru_ba", "lru_wi", "lru_bi", "lru_lambda", "w_out", "g_ffn", "w_ffn_gate", "w_ffn_up", "ffn_conv_w",
           "ffn_conv_b", "w_ffn_down", "final_g")


PACK_QUANTUM = 512 * 128


def _pack(arrays):
    flat = jnp.concatenate([a.reshape(-1).astype(F32) for a in arrays])
    n = -(-flat.shape[0] // PACK_QUANTUM) * PACK_QUANTUM
    return jnp.pad(flat, (0, n - flat.shape[0])).reshape(n // 128, 128)


def _unpack(packed, shapes):
    flat = packed.reshape(-1)
    out, off = [], 0
    for shp in shapes:
        n = int(np.prod(shp))
        out.append(flat[off:off + n].reshape(shp))
        off += n
    return out


def kernel(x, c, w_ada, b_ada, g_mix, w_in, b_f, pool_w, pool_scale, lru_conv_w, lru_conv_b, lru_wa, lru_ba, lru_wi, lru_bi, lru_lambda, w_out, g_ffn, w_ffn_gate, w_ffn_up, ffn_conv_w, ffn_conv_b, w_ffn_down, final_g, loss_target, m_w_ada, m_b_ada, m_g_mix, m_w_in, m_b_f, m_pool_w, m_pool_scale, m_lru_conv_w, m_lru_conv_b, m_lru_wa, m_lru_ba, m_lru_wi, m_lru_bi, m_lru_lambda, m_w_out, m_g_ffn, m_w_ffn_gate, m_w_ffn_up, m_ffn_conv_w, m_ffn_conv_b, m_w_ffn_down, m_final_g, v_w_ada, v_b_ada, v_g_mix, v_w_in, v_b_f, v_pool_w, v_pool_scale, v_lru_conv_w, v_lru_conv_b, v_lru_wa, v_lru_ba, v_lru_wi, v_lru_bi, v_lru_lambda, v_w_out, v_g_ffn, v_w_ffn_gate, v_w_ffn_up, v_ffn_conv_w, v_ffn_conv_b, v_w_ffn_down, v_final_g):
    env = dict(locals())
    W = {n: env[n] for n in WEIGHTS}
    M = {n: env["m_" + n] for n in WEIGHTS}
    V = {n: env["v_" + n] for n in WEIGHTS}
    L, D = g_mix.shape
    S = x.shape[1]
    F = 4 * w_ffn_gate.shape[2]
    ix, iy, ic = lax.axis_index("x"), lax.axis_index("y"), lax.axis_index("c")
    q = 2 * ix + iy
    me = 2 * q + ic

    head = _allgather8(_pack([c, lru_conv_w, ffn_conv_w]))
    nlc, nfc = lru_conv_w.size, ffn_conv_w.size
    c_all = head.reshape(N_DEV, -1)[:, :D]
    lru_cw = jnp.concatenate([head[2 * k].reshape(-1)[D:D + nlc].reshape(L, 4, -1) for k in range(N_CHIPS)], axis=2)
    ffn_cw = jnp.concatenate([head[2 * k].reshape(-1)[D + nlc:D + nlc + nfc].reshape(L, 3, -1) for k in range(N_CHIPS)],
                             axis=2)
    cact = _silu_rows(c_all)

    NA = w_ada.shape[2]
    mod_part = jnp.stack([_matmul(cact, w_ada[l], name="mm_ada") for l in range(L)])
    mod_all = _allgather8(mod_part.reshape(L * N_DEV, NA)).reshape(N_DEV, L, N_DEV, NA)
    mod_full = jnp.concatenate([mod_all[2 * k] for k in range(N_CHIPS)], axis=2)
    mod_mine = lax.dynamic_index_in_dim(mod_full, me, axis=1, keepdims=False) + b_ada
    mods = [[mod_mine[l, k * D:(k + 1) * D].reshape(1, D) for k in range(6)] for l in range(L)]

    shards_all = [W[n].astype(BF16) for n in BIG]
    shards = [[w[l] for w in shards_all] for l in range(L)]
    layers = []
    for l in range(L):
        layers.append(dict(
            g_mix=g_mix[l][None], g_ffn=g_ffn[l][None], b_f=jnp.pad(b_f[l], (0, 128 - N_HEADS))[None],
            pool_w=pool_w[l], pool_scale=pool_scale[l][None], lru_conv_w=lru_cw[l], lru_conv_b=lru_conv_b[l][None],
            lru_wa=lru_wa[l], lru_ba=lru_ba[l][None], lru_wi=lru_wi[l], lru_bi=lru_bi[l][None],
            lru_lambda=lru_lambda[l][None], ffn_conv_w=ffn_cw[l], ffn_conv_b=ffn_conv_b[l][None]))

    sel = jnp.stack([ic, q]).astype(jnp.int32)
    loss, dx, grads, dmods, dfinal_g, reduced = _local_step(x[0], loss_target[0], mods, layers, final_g[None], shards, sel)

    G = {n: [] for n in BIG}
    for g_in, g_out, g_down, g_gu in reduced:
        for n, a in zip(BIG, (g_in, g_out, g_gu[0], g_gu[1], g_down)):
            G[n].append(a)
    G = {n: jnp.stack(G[n]) for n in BIG}

    stack = lambda name: jnp.stack([grads[l][name] for l in range(L)])
    dmod = jnp.stack([jnp.concatenate(dmods[l], axis=1)[0] for l in range(L)])
    small = dict(b_ada=dmod, g_mix=stack("g_mix"), b_f=stack("b_f"), pool_w=stack("pool_w"),
                 pool_scale=stack("pool_scale"), lru_conv_w=stack("lru_conv_w"), lru_conv_b=stack("lru_conv_b"),
                 lru_wa=stack("lru_wa"), lru_ba=stack("lru_ba"), lru_wi=stack("lru_wi"), lru_bi=stack("lru_bi"),
                 lru_lambda=stack("lru_lambda"), g_ffn=stack("g_ffn"), ffn_conv_w=stack("ffn_conv_w"),
                 ffn_conv_b=stack("ffn_conv_b"), final_g=dfinal_g)
    packed = _pack([small[n] for n in SMALL] + [loss[0, :1]])
    everyone = _allgather8(packed)
    zero_sel = jnp.zeros((2,), jnp.int32)
    total = _sum_parts([(everyone, k) for k in range(N_DEV)], zero_sel, packed.shape[0], 128, "device_sum")
    sums = _unpack(total, [small[n].shape for n in SMALL] + [(1,)])
    loss_total = sums[-1][0]
    for n, a in zip(SMALL, sums[:-1]):
        a = a.reshape((L, -1, a.shape[-1])) if n in SHARDED_SMALL else a.reshape(W[n].shape)
        if n in SHARDED_SMALL:
            a = lax.dynamic_slice_in_dim(a, q * W[n].shape[2], W[n].shape[2], axis=2)
        G[n] = a

    dmod_all = everyone.reshape(N_DEV, -1)[:, :L * 6 * D].reshape(N_DEV, L, 6 * D)
    dmod_cols = jnp.transpose(lax.dynamic_slice_in_dim(dmod_all, q * NA, NA, axis=2), (1, 0, 2))
    G["w_ada"], d_ada, m_ada, v_ada = _ada_grad_adamw(cact.T, dmod_cols, w_ada, m_w_ada, v_w_ada)
    delta, new_m, new_v = {"w_ada": d_ada}, {"w_ada": m_ada}, {"w_ada": v_ada}

    for n in BIG:
        cols = W[n].shape[-1]
        outs = _adamw(*[a.reshape(-1, cols) for a in (W[n], G[n], M[n], V[n])])
        delta[n], new_m[n], new_v[n] = [o.reshape(W[n].shape) for o in outs]
    outs = _adamw(*[_pack([t[n] for n in SMALL]) for t in (W, G, M, V)])
    shapes = [W[n].shape for n in SMALL]
    for tgt, o in zip((delta, new_m, new_v), outs):
        for n, a in zip(SMALL, _unpack(o, shapes)):
            tgt[n] = a

    return (loss_total, dx[None], *[G[n] for n in WEIGHTS], *[delta[n] for n in WEIGHTS],
            *[new_m[n] for n in WEIGHTS], *[new_v[n] for n in WEIGHTS])
```

```python
import functools
import math

import jax
import jax.numpy as jnp
import numpy as np
from jax import lax
from jax.experimental import pallas as pl
from jax.experimental.pallas import tpu as pltpu

F32 = jnp.float32
BF16 = jnp.bfloat16
MESH = pl.DeviceIdType.MESH

EPS = 1e-6
HEAD_DIM = 128
POOL_WIDTH = 512
POOL_WINDOWS = (2, 4, 8, 16)
ATTN_WIDTH = 1024
N_HEADS = 8
LRU_WIDTH = 512
LRU_C = 8.0
N_IN = 4616
ZP, ZQ, ZK, ZV, ZX, ZY, ZF, ZW = 0, 512, 1536, 2560, 3584, 4096, 4608, 5120
FFN_CHUNK = 512
N_CHIPS = 4
N_DEV = 8

ADAM_LR, ADAM_B1, ADAM_B2, ADAM_EPS, ADAM_WD, ADAM_STEP = 0.001, 0.9, 0.999, 1e-08, 0.01, 10

TILES = dict(mm_m=512, mm_n=1024, mm_k=2048, row=512, attn=512, lru=256, cum=512, ew=256)
VMEM_LIMIT = 48 * 2**20


def _params(sem):
    return pltpu.CompilerParams(dimension_semantics=sem, vmem_limit_bytes=VMEM_LIMIT)


def _div_tile(n, pref, align):
    if n <= pref:
        return n
    t = (pref // align) * align
    while t >= align:
        if n % t == 0:
            return t
        t -= align
    raise ValueError(f"no tile for {n}")


def _sigmoid(x):
    return 0.5 * jnp.tanh(0.5 * x) + 0.5


def _gelu_parts(x):
    k = math.sqrt(2.0 / math.pi)
    u = k * (x + 0.044715 * x * x * x)
    t = jnp.tanh(u)
    gel = 0.5 * x * (1.0 + t)
    dgel = 0.5 * (1.0 + t) + 0.5 * x * (1.0 - t * t) * k * (1.0 + 3 * 0.044715 * x * x)
    return gel, dgel


def _neg_expm1(y):
    series = -y * (1.0 + y * (0.5 + y * (1.0 / 6 + y * (1.0 / 24 + y * (1.0 / 120)))))
    return jnp.where(y > -0.1, series, 1.0 - jnp.exp(y))


def _dot(a, b):
    return jnp.dot(a.astype(BF16), b.astype(BF16), preferred_element_type=F32)


def _dot_nt(a, b):
    return lax.dot_general(a.astype(BF16), b.astype(BF16), (((1,), (1,)), ((), ())), preferred_element_type=F32)


def _dot3(tri, v):
    hi = v.astype(BF16)
    r1 = v - hi.astype(F32)
    mid = r1.astype(BF16)
    lo = (r1 - mid.astype(F32)).astype(BF16)
    t = tri.astype(BF16)
    return (jnp.dot(t, hi, preferred_element_type=F32) + jnp.dot(t, mid, preferred_element_type=F32)
            + jnp.dot(t, lo, preferred_element_type=F32))


def _colsum(v):
    return jnp.sum(v, axis=0, keepdims=True)


def _rows(n, cols=128):
    return lax.broadcasted_iota(jnp.int32, (n, cols), 0)


def _matmul(a, b, *, nt=False, ta=False, out_dtype=F32, res=None, gate=None, gu=None, name="matmul"):
    M, K = a.shape[::-1] if ta else a.shape
    pair = 2 * FFN_CHUNK
    if gu == "b":
        N = b.shape[1] if nt else 2 * b.shape[2]
    else:
        N = b.shape[0] if nt else b.shape[1]
    tm = _div_tile(M, TILES["mm_m"] * (1 if gu is None else 2), 8)
    tn = _div_tile(N, TILES["mm_n"], 128)
    tk = _div_tile(K, TILES["mm_k"], 128)
    if gu == "b" and nt:
        tk = pair
    elif gu == "b":
        tn = pair
    elif gu == "out":
        tn = FFN_CHUNK
    nk = K // tk
    epi = res is not None
    n_b = 2 if gu == "b" else 1

    def body(*refs):
        a_ref, b_refs, rest = refs[0], refs[1:1 + n_b], refs[1 + n_b:]
        if epi:
            res_ref, gate_ref, o_ref, x_ref = rest[:4]
        else:
            o_ref = rest[0]
        if gu == "b" and nt:
            part = (_dot_nt(a_ref[:, :FFN_CHUNK], b_refs[0][...]) + _dot_nt(a_ref[:, FFN_CHUNK:], b_refs[1][...]))
        elif gu == "b":
            part = jnp.concatenate([_dot(a_ref[...], b_refs[0][...]), _dot(a_ref[...], b_refs[1][...])], axis=1)
        else:
            if ta:
                part = lax.dot_general(a_ref[...].astype(BF16), b_refs[0][...].astype(BF16), (((0,), (0,)), ((), ())),
                                       preferred_element_type=F32)
            else:
                part = _dot_nt(a_ref[...], b_refs[0][...]) if nt else _dot(a_ref[...], b_refs[0][...])

        def finish(acc):
            o_ref[...] = acc.astype(o_ref.dtype)
            if epi:
                x_ref[...] = res_ref[...] + gate_ref[...] * acc

        if nk == 1:
            finish(part)
        else:
            acc_ref = refs[-1]
            k = pl.program_id(2)

            @pl.when(k == 0)
            def _():
                acc_ref[...] = part

            @pl.when(k > 0)
            def _():
                acc_ref[...] += part

            @pl.when(k == nk - 1)
            def _():
                finish(acc_ref[...])

    a_spec = pl.BlockSpec((tk, tm), lambda i, j, k: (k, i)) if ta else pl.BlockSpec((tm, tk), lambda i, j, k: (i, k))
    o_spec = pl.BlockSpec((tm, tn), lambda i, j, k: (i, j))
    out_shape = jax.ShapeDtypeStruct((M, N), out_dtype)
    if gu == "b" and nt:
        b_specs = [pl.BlockSpec((None, tn, FFN_CHUNK), lambda i, j, k, p=p: (p, j, k)) for p in range(2)]
    elif gu == "b":
        b_specs = [pl.BlockSpec((None, tk, FFN_CHUNK), lambda i, j, k, p=p: (p, k, j)) for p in range(2)]
    elif nt:
        b_specs = [pl.BlockSpec((tn, tk), lambda i, j, k: (j, k))]
    else:
        b_specs = [pl.BlockSpec((tk, tn), lambda i, j, k: (k, j))]
    if gu == "out":
        o_spec = pl.BlockSpec((None, tm, tn), lambda i, j, k: (j % 2, i, j // 2))
        out_shape = jax.ShapeDtypeStruct((2, M, N // 2), out_dtype)
    in_specs, args = [a_spec] + b_specs, [a] + [b] * n_b
    out_specs = o_spec
    if epi:
        in_specs += [o_spec, pl.BlockSpec((1, tn), lambda i, j, k: (0, j))]
        args += [res, gate]
        out_specs = (o_spec, o_spec)
        out_shape = (out_shape, jax.ShapeDtypeStruct((M, N), F32))
    return pl.pallas_call(
        body, name=name, grid=(M // tm, N // tn, nk), in_specs=in_specs, out_specs=out_specs, out_shape=out_shape,
        scratch_shapes=[pltpu.VMEM((tm, tn), F32)] if nk > 1 else [],
        compiler_params=_params(("parallel", "parallel", "arbitrary")),
    )(*args)


def _norm_mod(x, g, sc, sh):
    S, D = x.shape
    tr = _div_tile(S, TILES["row"], 8)

    def body(x_ref, g_ref, sc_ref, sh_ref, h_ref):
        xf = x_ref[...]
        r = lax.rsqrt(jnp.mean(xf * xf, axis=-1, keepdims=True) + EPS)
        h_ref[...] = (((xf * r) * g_ref[...]) * (1.0 + sc_ref[...]) + sh_ref[...]).astype(h_ref.dtype)

    row = pl.BlockSpec((tr, D), lambda i: (i, 0))
    vec = pl.BlockSpec((1, D), lambda i: (0, 0))
    return pl.pallas_call(
        body, name="norm_mod", grid=(S // tr,), in_specs=[row, vec, vec, vec], out_specs=row,
        out_shape=jax.ShapeDtypeStruct((S, D), BF16), compiler_params=_params(("parallel",)),
    )(x, g, sc, sh)


def _norm_mod_bwd(x, dh, dres, g, sc):
    S, D = x.shape
    tr = _div_tile(S, TILES["ew"], 8)

    def body(x_ref, dh_ref, dres_ref, g_ref, sc_ref, dx_ref, dg_ref, dsc_ref, dsh_ref):
        xf, dh_ = x_ref[...], dh_ref[...]
        r = lax.rsqrt(jnp.mean(xf * xf, axis=-1, keepdims=True) + EPS)
        xhat = xf * r
        dxhat = dh_ * (g_ref[...] * (1.0 + sc_ref[...]))
        dx_ref[...] = dres_ref[...] + r * (dxhat - xhat * jnp.mean(dxhat * xhat, axis=-1, keepdims=True))
        t = _colsum(dh_ * xhat)

        @pl.when(pl.program_id(0) == 0)
        def _():
            dg_ref[...] = jnp.zeros_like(dg_ref)
            dsc_ref[...] = jnp.zeros_like(dsc_ref)
            dsh_ref[...] = jnp.zeros_like(dsh_ref)

        dg_ref[...] += t * (1.0 + sc_ref[...])
        dsc_ref[...] += t * g_ref[...]
        dsh_ref[...] += _colsum(dh_)

    row = pl.BlockSpec((tr, D), lambda i: (i, 0))
    vec = pl.BlockSpec((1, D), lambda i: (0, 0))
    vshape = jax.ShapeDtypeStruct((1, D), F32)
    return pl.pallas_call(
        body, name="norm_mod_bwd", grid=(S // tr,), in_specs=[row, row, row, vec, vec], out_specs=(row, vec, vec, vec),
        out_shape=(jax.ShapeDtypeStruct((S, D), F32), vshape, vshape, vshape), compiler_params=_params(("arbitrary",)),
    )(x, dh, dres, g, sc)


def _gate_bwd(dx, m, gt):
    S, D = dx.shape
    tr = _div_tile(S, TILES["row"], 8)

    def body(dx_ref, m_ref, gt_ref, dm_ref, dgt_ref):
        d = dx_ref[...]
        dm_ref[...] = (d * gt_ref[...]).astype(dm_ref.dtype)

        @pl.when(pl.program_id(0) == 0)
        def _():
            dgt_ref[...] = jnp.zeros_like(dgt_ref)

        dgt_ref[...] += _colsum(d * m_ref[...])

    row = pl.BlockSpec((tr, D), lambda i: (i, 0))
    vec = pl.BlockSpec((1, D), lambda i: (0, 0))
    return pl.pallas_call(
        body, name="gate_bwd", grid=(S // tr,), in_specs=[row, row, vec], out_specs=(row, vec),
        out_shape=(jax.ShapeDtypeStruct((S, D), BF16), jax.ShapeDtypeStruct((1, D), F32)),
        compiler_params=_params(("arbitrary",)),
    )(dx, m, gt)


def _loss_head(x, target, g):
    S, D = x.shape
    tr = _div_tile(S, TILES["ew"], 8)

    def body(x_ref, t_ref, g_ref, dx_ref, dg_ref, loss_ref):
        xf = x_ref[...]
        r = lax.rsqrt(jnp.mean(xf * xf, axis=-1, keepdims=True) + EPS)
        xhat = xf * r
        err = xhat * g_ref[...] - t_ref[...]
        dy = err * (1.0 / D)
        dxhat = dy * g_ref[...]
        dx_ref[...] = r * (dxhat - xhat * jnp.mean(dxhat * xhat, axis=-1, keepdims=True))

        @pl.when(pl.program_id(0) == 0)
        def _():
            dg_ref[...] = jnp.zeros_like(dg_ref)
            loss_ref[...] = jnp.zeros_like(loss_ref)

        dg_ref[...] += _colsum(dy * xhat)
        loss_ref[...] += 0.5 * jnp.sum(jnp.mean(err * err, axis=-1, keepdims=True))

    row = pl.BlockSpec((tr, D), lambda i: (i, 0))
    vec = pl.BlockSpec((1, D), lambda i: (0, 0))
    one = pl.BlockSpec((1, 128), lambda i: (0, 0))
    return pl.pallas_call(
        body, name="loss_head", grid=(S // tr,), in_specs=[row, row, vec], out_specs=(row, vec, one),
        out_shape=(jax.ShapeDtypeStruct((S, D), F32), jax.ShapeDtypeStruct((1, D), F32),
                   jax.ShapeDtypeStruct((1, 128), F32)),
        compiler_params=_params(("arbitrary",)),
    )(x, target, g)


POOL_HALO = 16


def _pool_delta(ext, u, first_pos, tr):
    pos = (first_pos + _rows(tr) + 1).astype(F32)
    outs = []
    for gi, win in enumerate(POOL_WINDOWS):
        s = ext[:, gi * 128:(gi + 1) * 128]
        d = 1
        while d < win:
            s = s + pltpu.roll(s, d, 0)
            d *= 2
        outs.append(s[POOL_HALO:] / jnp.minimum(pos, float(win)) - u[:, gi * 128:(gi + 1) * 128])
    return outs


def _pool_fwd(z, w, scale):
    S = z.shape[0]
    tr = _div_tile(S, TILES["row"], POOL_HALO)
    hb = tr // POOL_HALO

    def body(z_ref, halo_ref, w_ref, sc_ref, y_ref):
        i = pl.program_id(0)
        u = z_ref[...]
        halo = jnp.where(i > 0, halo_ref[...], 0.0)
        ds_ = _pool_delta(jnp.concatenate([halo, u], axis=0), u, i * tr, tr)
        for gi in range(4):
            y_ref[:, gi * 128:(gi + 1) * 128] = _dot(ds_[gi], w_ref[gi]) * sc_ref[:, gi * 128:(gi + 1) * 128]

    return pl.pallas_call(
        body, name="pool_fwd", grid=(S // tr,),
        in_specs=[pl.BlockSpec((tr, POOL_WIDTH), lambda i: (i, 0)),
                  pl.BlockSpec((POOL_HALO, POOL_WIDTH), lambda i: (jnp.maximum(i * hb - 1, 0), 0)),
                  pl.BlockSpec((4, 128, 128), lambda i: (0, 0, 0)), pl.BlockSpec((1, POOL_WIDTH), lambda i: (0, 0))],
        out_specs=pl.BlockSpec((tr, POOL_WIDTH), lambda i: (i, 0)),
        out_shape=jax.ShapeDtypeStruct((S, POOL_WIDTH), F32), compiler_params=_params(("parallel",)),
    )(z, z, w, scale)


def _pool_bwd(z, dy, w, scale):
    S = z.shape[0]
    tr = _div_tile(S, TILES["row"], POOL_HALO)
    hb = tr // POOL_HALO
    nt = S // tr

    def body(z_ref, halo_ref, dy_ref, dyn_ref, w_ref, sc_ref, dz_ref, dw_ref, dsc_ref):
        i = pl.program_id(0)
        u = z_ref[...]
        halo = jnp.where(i > 0, halo_ref[...], 0.0)
        ds_ = _pool_delta(jnp.concatenate([halo, u], axis=0), u, i * tr, tr)
        dy_ext = jnp.concatenate([dy_ref[...], jnp.where(i < nt - 1, dyn_ref[...], 0.0)], axis=0)
        pos = (i * tr + _rows(tr + POOL_HALO) + 1).astype(F32)

        @pl.when(i == 0)
        def _():
            dw_ref[...] = jnp.zeros_like(dw_ref)
            dsc_ref[...] = jnp.zeros_like(dsc_ref)

        for gi, win in enumerate(POOL_WINDOWS):
            cols = slice(gi * 128, (gi + 1) * 128)
            dyg = dy_ext[:, cols]
            dys = dyg * sc_ref[:, cols]
            dsc_ref[:, cols] += _colsum(dyg[:tr] * _dot(ds_[gi], w_ref[gi]))
            dw_ref[gi] += _dot(ds_[gi].T, dys[:tr])
            dd = _dot_nt(dys, w_ref[gi])
            e = dd / jnp.minimum(pos, float(win))
            d = 1
            while d < win:
                e = e + pltpu.roll(e, tr + POOL_HALO - d, 0)
                d *= 2
            dz_ref[:, cols] = (e[:tr] - dd[:tr]).astype(dz_ref.dtype)

    return pl.pallas_call(
        body, name="pool_bwd", grid=(nt,),
        in_specs=[pl.BlockSpec((tr, POOL_WIDTH), lambda i: (i, 0)),
                  pl.BlockSpec((POOL_HALO, POOL_WIDTH), lambda i: (jnp.maximum(i * hb - 1, 0), 0)),
                  pl.BlockSpec((tr, POOL_WIDTH), lambda i: (i, 0)),
                  pl.BlockSpec((POOL_HALO, POOL_WIDTH), lambda i: (jnp.minimum((i + 1) * hb, nt * hb - 1), 0)),
                  pl.BlockSpec((4, 128, 128), lambda i: (0, 0, 0)), pl.BlockSpec((1, POOL_WIDTH), lambda i: (0, 0))],
        out_specs=(pl.BlockSpec((tr, POOL_WIDTH), lambda i: (i, 0)), pl.BlockSpec((4, 128, 128), lambda i: (0, 0, 0)),
                   pl.BlockSpec((1, POOL_WIDTH), lambda i: (0, 0))),
        out_shape=(jax.ShapeDtypeStruct((S, POOL_WIDTH), BF16), jax.ShapeDtypeStruct((4, 128, 128), F32),
                   jax.ShapeDtypeStruct((1, POOL_WIDTH), F32)),
        compiler_params=_params(("arbitrary",)),
    )(z, z, dy, dy, w, scale)


def _log_sigmoid(x):
    return jnp.minimum(x, 0.0) - jnp.log(1.0 + jnp.exp(-jnp.abs(x)))


def _forget_cumsum(z, b_f):
    S = z.shape[0]
    tr = _div_tile(S, TILES["cum"], 8)
    zf_block = ZF // 128

    def body(z_ref, b_ref, f_ref, carry):
        @pl.when(pl.program_id(0) == 0)
        def _():
            carry[...] = jnp.zeros_like(carry)

        lf = _log_sigmoid(z_ref[...] + b_ref[...])
        tri = lax.broadcasted_iota(jnp.int32, (tr, tr), 1) <= lax.broadcasted_iota(jnp.int32, (tr, tr), 0)
        f_ref[...] = _dot3(tri, lf) + carry[...]
        carry[...] += _colsum(lf)

    return pl.pallas_call(
        body, name="forget_cumsum", grid=(S // tr,),
        in_specs=[pl.BlockSpec((tr, 128), lambda i: (i, zf_block)), pl.BlockSpec((1, 128), lambda i: (0, 0))],
        out_specs=pl.BlockSpec((tr, 128), lambda i: (i, 0)), out_shape=jax.ShapeDtypeStruct((S, 128), F32),
        scratch_shapes=[pltpu.VMEM((1, 128), F32)], compiler_params=_params(("arbitrary",)),
    )(z, b_f)


def _forget_cumsum_bwd(z, b_f, dF):
    S = z.shape[0]
    tr = _div_tile(S, TILES["cum"], 8)
    nt = S // tr
    zf_block = ZF // 128

    def body(z_ref, b_ref, df_ref, dz_ref, db_ref, carry):
        @pl.when(pl.program_id(0) == 0)
        def _():
            carry[...] = jnp.zeros_like(carry)
            db_ref[...] = jnp.zeros_like(db_ref)

        dF_ = df_ref[...]
        tri = lax.broadcasted_iota(jnp.int32, (tr, tr), 1) >= lax.broadcasted_iota(jnp.int32, (tr, tr), 0)
        dlf = _dot3(tri, dF_) + carry[...]
        carry[...] += _colsum(dF_)
        lane = lax.broadcasted_iota(jnp.int32, (tr, 128), 1)
        dzf = jnp.where(lane < N_HEADS, dlf * _sigmoid(-(z_ref[...] + b_ref[...])), 0.0)
        dz_ref[...] = dzf.astype(dz_ref.dtype)
        db_ref[...] += _colsum(dzf)

    return pl.pallas_call(
        body, name="forget_cumsum_bwd", grid=(nt,),
        in_specs=[pl.BlockSpec((tr, 128), lambda i: (nt - 1 - i, zf_block)), pl.BlockSpec((1, 128), lambda i: (0, 0)),
                  pl.BlockSpec((tr, 128), lambda i: (nt - 1 - i, 0))],
        out_specs=(pl.BlockSpec((tr, 128), lambda i: (nt - 1 - i, 0)), pl.BlockSpec((1, 128), lambda i: (0, 0))),
        out_shape=(jax.ShapeDtypeStruct((S, 128), BF16), jax.ShapeDtypeStruct((1, 128), F32)),
        scratch_shapes=[pltpu.VMEM((1, 128), F32)], compiler_params=_params(("arbitrary",)),
    )(z, b_f, dF)


NEG = -1e30
ATTN_SCALE = HEAD_DIM ** -0.5


def _on_block_kind(qi, kj, fn):
    @pl.when(qi == kj)
    def _():
        fn(True)

    @pl.when(qi != kj)
    def _():
        fn(False)


FIRST, LAST, HEAD_FIRST, HEAD_LAST, KEY_ZERO = 1, 2, 4, 8, 16


def _tri_schedule(n, by_key=False):
    outer, inner, flags = [], [], []
    for a in range(n):
        partners = list(range(a, n)) if by_key else list(range(a + 1))
        for idx, b in enumerate(partners):
            f = FIRST if idx == 0 else 0
            f |= LAST if idx == len(partners) - 1 else 0
            f |= KEY_ZERO if (a if by_key else b) == 0 else 0
            outer.append(a)
            inner.append(b)
            flags.append(f)
    flags[0] |= HEAD_FIRST
    flags[-1] |= HEAD_LAST
    return [jnp.asarray(np.array(v, np.int32)) for v in (outer, inner, flags)]


def _flash_call(body, name, sched, in_specs, out_specs, out_shape, scratch, comm_ins=0, comm_shapes=(), comm_sems=()):
    grid_spec = pltpu.PrefetchScalarGridSpec(
        num_scalar_prefetch=3, grid=(N_HEADS, int(sched[0].shape[0])), in_specs=list(in_specs) + [ANY] * comm_ins,
        out_specs=tuple(out_specs) + (ANY,) * len(comm_shapes), scratch_shapes=list(scratch) + list(comm_sems))
    heads = "arbitrary" if comm_sems else "parallel"
    return pl.pallas_call(body, name=name, grid_spec=grid_spec, out_shape=tuple(out_shape) + tuple(comm_shapes),
                          compiler_params=_params((heads, "arbitrary")))


def _scores_t(q, k, fq_row, fk_col, diagonal, floor):
    st = _dot_nt(k, q) * ATTN_SCALE + fq_row - fk_col
    if not diagonal:
        return st
    t = st.shape[0]
    return jnp.where(lax.broadcasted_iota(jnp.int32, (t, t), 0) <= lax.broadcasted_iota(jnp.int32, (t, t), 1), st, floor)


def _flash_fwd(z, v_t, f_col, f_row, next_shards=None):
    S = z.shape[0]
    t = _div_tile(S, TILES["attn"], 128)
    sched = _tri_schedule(S // t)
    n_steps = int(sched[0].shape[0])
    qb, kb = ZQ // 128, ZK // 128
    n_comm = N_SHARDS if next_shards is not None else 0

    def body(qt, kt, ft, q_ref, k_ref, vt_ref, fq_ref, fk_ref, *rest):
        shard_refs, rest = rest[:n_comm], rest[n_comm:]
        (o_ref, lse_ref), rest = rest[:2], rest[2:]
        gathered, rest = (rest[:4], rest[4:]) if n_comm else ((), rest)
        (m_sc, l_sc, acc_sc), sems = rest[:3], rest[3:]
        head, step = pl.program_id(0), pl.program_id(1)
        qi, kj, fl = qt[step], kt[step], ft[step]

        if n_comm:
            @pl.when((head == 0) & (step == 0))
            def _():
                _GatherPlan(shard_refs, gathered, sems).start()

            @pl.when((head == N_HEADS - 2) & (step == 0))
            def _():
                _GatherPlan(shard_refs, gathered, sems).forward()

            @pl.when((head == N_HEADS - 1) & (step == n_steps - 1))
            def _():
                _GatherPlan(shard_refs, gathered, sems).finish()

        @pl.when((fl & FIRST) != 0)
        def _():
            m_sc[...] = jnp.full_like(m_sc, NEG)
            l_sc[...] = jnp.zeros_like(l_sc)
            acc_sc[...] = jnp.zeros_like(acc_sc)

        def update(diagonal):
            st = _scores_t(q_ref[...], k_ref[...], fq_ref[...], fk_ref[...], diagonal, NEG)
            m_new = jnp.maximum(m_sc[...], jnp.max(st, axis=0, keepdims=True))
            alpha = jnp.exp(m_sc[...] - m_new)
            pt = jnp.exp(st - m_new)
            l_sc[...] = alpha * l_sc[...] + jnp.sum(pt, axis=0, keepdims=True)
            acc_sc[...] = alpha * acc_sc[...] + _dot(vt_ref[...], pt)
            m_sc[...] = m_new

        _on_block_kind(qi, kj, update)

        @pl.when((fl & LAST) != 0)
        def _():
            o_ref[...] = acc_sc[...] / l_sc[...]
            lse_ref[...] = m_sc[...] + jnp.log(l_sc[...])

    row = pl.BlockSpec((None, 1, t), lambda h, s, qt, kt, ft: (h, 0, qt[s]))
    return _flash_call(
        body, "flash_fwd", sched,
        [pl.BlockSpec((t, 128), lambda h, s, qt, kt, ft: (qt[s], qb + h)),
         pl.BlockSpec((t, 128), lambda h, s, qt, kt, ft: (kt[s], kb + h)),
         pl.BlockSpec((128, t), lambda h, s, qt, kt, ft: (h, kt[s])), row,
         pl.BlockSpec((None, t, 1), lambda h, s, qt, kt, ft: (h, kt[s], 0))],
        (pl.BlockSpec((128, t), lambda h, s, qt, kt, ft: (h, qt[s])), row),
        (jax.ShapeDtypeStruct((ATTN_WIDTH, S), F32), jax.ShapeDtypeStruct((N_HEADS, 1, S), F32)),
        [pltpu.VMEM((1, t), F32), pltpu.VMEM((1, t), F32), pltpu.VMEM((128, t), F32)],
        comm_ins=n_comm, comm_shapes=_gather_shapes(next_shards) if n_comm else (),
        comm_sems=GATHER_SEMS if n_comm else (),
    )(*sched, z, z, v_t, f_row, f_col, *(next_shards or ()))


def _flash_bwd(z, k_t, dy, do_t, o_t, lse_row, f_col, f_row, pairs=None):
    S = z.shape[0]
    t = _div_tile(S, TILES["attn"], 128)
    n = S // t
    sched = _tri_schedule(n, by_key=True)
    n_steps = int(sched[0].shape[0])
    qb, kb, vb = ZQ // 128, ZK // 128, ZV // 128
    dob = POOL_WIDTH // 128
    n_comm = N_PAIRS if pairs is not None else 0

    def body(kt, qt, ft, q_ref, k_ref, v_ref, kt_ref, do_ref, dot_ref, ot_ref, lse_ref, fq_ref, fk_ref, *rest):
        pair_refs, rest = rest[:n_comm], rest[n_comm:]
        (dq_ref, dk_ref, dv_ref, dfk_ref, dfq_ref), rest = rest[:5], rest[5:]
        arrived, rest = rest[:n_comm], rest[n_comm:]
        (dk_sc, dv_sc, dfk_sc, dq_sc, dfq_sc, delta_sc), sems = rest[:6], rest[6:]
        head, step = pl.program_id(0), pl.program_id(1)
        kj, qi, fl = kt[step], qt[step], ft[step]

        if n_comm:
            @pl.when((head == 0) & (step == 0))
            def _():
                for cp in _scatter_copies(pair_refs, arrived, sems):
                    cp.start()

            @pl.when((head == N_HEADS - 1) & (step == n_steps - 1))
            def _():
                for cp in _scatter_copies(pair_refs, arrived, sems):
                    cp.wait()

        @pl.when((fl & HEAD_FIRST) != 0)
        def _():
            dq_sc[...] = jnp.zeros_like(dq_sc)
            dfq_sc[...] = jnp.zeros_like(dfq_sc)

        @pl.when((fl & FIRST) != 0)
        def _():
            dk_sc[...] = jnp.zeros_like(dk_sc)
            dv_sc[...] = jnp.zeros_like(dv_sc)
            dfk_sc[...] = jnp.zeros_like(dfk_sc)

        @pl.when((fl & KEY_ZERO) != 0)
        def _():
            delta_sc[qi] = jnp.sum(dot_ref[...] * ot_ref[...], axis=0, keepdims=True)

        def update(diagonal):
            pt = jnp.exp(_scores_t(q_ref[...], k_ref[...], fq_ref[...], fk_ref[...], diagonal, NEG) - lse_ref[...])
            dv_sc[...] += _dot(pt, do_ref[...])
            dst = pt * (_dot_nt(v_ref[...], do_ref[...]) - delta_sc[qi])
            dk_sc[...] += _dot(dst, q_ref[...])
            dfk_sc[...] += jnp.sum(dst, axis=1, keepdims=True)
            dfq_sc[qi] = dfq_sc[qi] + jnp.sum(dst, axis=0, keepdims=True)
            dq_sc[qi] = dq_sc[qi] + _dot(kt_ref[...], dst)

        _on_block_kind(qi, kj, update)

        @pl.when((fl & LAST) != 0)
        def _():
            dk_ref[...] = (dk_sc[...] * ATTN_SCALE).astype(dk_ref.dtype)
            dv_ref[...] = dv_sc[...].astype(dv_ref.dtype)
            dfk_ref[...] = -dfk_sc[...]

        @pl.when((fl & HEAD_LAST) != 0)
        def _():
            for i in range(n):
                dq_ref[:, i * t:(i + 1) * t] = (dq_sc[i] * ATTN_SCALE).astype(dq_ref.dtype)
                dfq_ref[:, i * t:(i + 1) * t] = dfq_sc[i]

    def qs(block):
        return pl.BlockSpec((t, 128), lambda h, s, kt, qt, ft: (qt[s], block + h))

    def kv(block):
        return pl.BlockSpec((t, 128), lambda h, s, kt, qt, ft: (kt[s], block + h))

    first_sweep = pl.BlockSpec((128, t), lambda h, s, kt, qt, ft: (h, jnp.where(kt[s] == 0, qt[s], n - 1)))
    qrow = pl.BlockSpec((None, 1, t), lambda h, s, kt, qt, ft: (h, 0, qt[s]))
    kcol = pl.BlockSpec((None, t, 1), lambda h, s, kt, qt, ft: (h, kt[s], 0))
    out = pl.BlockSpec((t, 128), lambda h, s, kt, qt, ft: (kt[s], h))
    return _flash_call(
        body, "flash_bwd", sched,
        [qs(qb), kv(kb), kv(vb), pl.BlockSpec((128, t), lambda h, s, kt, qt, ft: (h, kt[s])), qs(dob), first_sweep,
         first_sweep, qrow, qrow, kcol],
        (pl.BlockSpec((128, S), lambda h, s, kt, qt, ft: (h, 0)), out, out, kcol,
         pl.BlockSpec((None, 1, S), lambda h, s, kt, qt, ft: (h, 0, 0))),
        (jax.ShapeDtypeStruct((ATTN_WIDTH, S), BF16), jax.ShapeDtypeStruct((S, ATTN_WIDTH), BF16),
         jax.ShapeDtypeStruct((S, ATTN_WIDTH), BF16), jax.ShapeDtypeStruct((N_HEADS, S, 1), F32),
         jax.ShapeDtypeStruct((N_HEADS, 1, S), F32)),
        [pltpu.VMEM((t, 128), F32), pltpu.VMEM((t, 128), F32), pltpu.VMEM((t, 1), F32), pltpu.VMEM((n, 128, t), F32),
         pltpu.VMEM((n, 1, t), F32), pltpu.VMEM((n, 1, t), F32)],
        comm_ins=n_comm, comm_shapes=_scatter_shapes(pairs) if n_comm else (), comm_sems=SCATTER_SEMS if n_comm else (),
    )(*sched, z, z, z, k_t, dy, do_t, o_t, lse_row, f_row, f_col, *(pairs or ()))


LRU_HALO = 8


def _lru_gates(ext, cw_ref, cb_ref, wa_ref, ba_ref, wi_ref, bi_ref, lam_ref, tr):
    taps = [pltpu.roll(ext, 3 - k, 0)[LRU_HALO:] if k < 3 else ext[LRU_HALO:] for k in range(4)]
    xc = cb_ref[...] + taps[0] * cw_ref[0:1, :]
    for k in range(1, 4):
        xc = xc + taps[k] * cw_ref[k:k + 1, :]
    ga = jnp.concatenate([_dot(xc[:, g * 128:(g + 1) * 128], wa_ref[g]) for g in range(4)], axis=1) + ba_ref[...]
    gi = jnp.concatenate([_dot(xc[:, g * 128:(g + 1) * 128], wi_ref[g]) for g in range(4)], axis=1) + bi_ref[...]
    r, ig = _sigmoid(ga), _sigmoid(gi)
    nl = -lam_ref[...]
    sp = jnp.maximum(nl, 0.0) + jnp.log(1.0 + jnp.exp(-jnp.abs(nl)))
    la = -LRU_C * r * sp
    a = jnp.exp(la)
    mult = jnp.sqrt(_neg_expm1(2.0 * la))
    return xc, r, ig, sp, a, mult, taps


def _lru_specs(tr, nt, rev):
    hb = tr // LRU_HALO
    ti = (lambda i: nt - 1 - i) if rev else (lambda i: i)
    zx_b, zy_b = ZX // LRU_WIDTH, ZY // LRU_WIDTH
    cur = lambda b: pl.BlockSpec((tr, LRU_WIDTH), lambda i: (ti(i), b))
    prev = lambda b: pl.BlockSpec((LRU_HALO, LRU_WIDTH), lambda i: (jnp.maximum(ti(i) * hb - 1, 0), b))
    vec = pl.BlockSpec((1, LRU_WIDTH), lambda i: (0, 0))
    cw = pl.BlockSpec((4, LRU_WIDTH), lambda i: (0, 0))
    blk = pl.BlockSpec((4, 128, 128), lambda i: (0, 0, 0))
    return ti, cur, prev, vec, cw, blk, zx_b, zy_b


def _lru_fwd(z, cw, cb, wa, ba, wi, bi, lam):
    S = z.shape[0]
    tr = _div_tile(S, TILES["lru"], 8)
    nt = S // tr
    ti, cur, prev, vec, cwspec, blk, zx_b, zy_b = _lru_specs(tr, nt, False)

    def body(zx_ref, halo_ref, zy_ref, cw_ref, cb_ref, wa_ref, ba_ref, wi_ref, bi_ref, lam_ref, y_ref, h_ref, carry):
        i = pl.program_id(0)

        @pl.when(i == 0)
        def _():
            carry[...] = jnp.zeros_like(carry)

        ext = jnp.concatenate([jnp.where(i > 0, halo_ref[...], 0.0), zx_ref[...]], axis=0)
        xc, r, ig, sp, a, mult, _ = _lru_gates(ext, cw_ref, cb_ref, wa_ref, ba_ref, wi_ref, bi_ref, lam_ref, tr)
        A, B = a, mult * (ig * xc)
        row = _rows(tr, LRU_WIDTH)
        d = 1
        while d < tr:
            a_sh = jnp.where(row >= d, pltpu.roll(A, d, 0), 1.0)
            b_sh = jnp.where(row >= d, pltpu.roll(B, d, 0), 0.0)
            B = A * b_sh + B
            A = A * a_sh
            d *= 2
        h = B + A * carry[...]
        h_ref[...] = h
        carry[...] = h_ref[pl.ds(tr - 1, 1), :]
        y_ref[...] = h * _gelu_parts(zy_ref[...])[0]

    out = pl.BlockSpec((tr, LRU_WIDTH), lambda i: (i, 0))
    shape = jax.ShapeDtypeStruct((S, LRU_WIDTH), F32)
    return pl.pallas_call(
        body, name="lru_fwd", grid=(nt,),
        in_specs=[cur(zx_b), prev(zx_b), cur(zy_b), cwspec, vec, blk, vec, blk, vec, vec],
        out_specs=(out, out), out_shape=(shape, shape), scratch_shapes=[pltpu.VMEM((1, LRU_WIDTH), F32)],
        compiler_params=_params(("arbitrary",)),
    )(z, z, z, cw, cb, wa, ba, wi, bi, lam)


def _lru_bwd(z, dy, hs, cw, cb, wa, ba, wi, bi, lam):
    S = z.shape[0]
    tr = _div_tile(S, TILES["lru"], 8)
    nt = S // tr
    ti, cur, prev, vec, cwspec, blk, zx_b, zy_b = _lru_specs(tr, nt, True)
    dy_b = (POOL_WIDTH + ATTN_WIDTH) // LRU_WIDTH

    def body(zx_ref, halo_ref, zy_ref, dy_ref, h_ref, hprev_ref, cw_ref, cb_ref, wa_ref, ba_ref, wi_ref, bi_ref, lam_ref,
             dzx_ref, dzy_ref, dcw_ref, dcb_ref, dwa_ref, dba_ref, dwi_ref, dbi_ref, dlam_ref, gcarry, dxc_next, tmp):
        i = pl.program_id(0)
        t_idx = nt - 1 - i

        @pl.when(i == 0)
        def _():
            gcarry[...] = jnp.zeros_like(gcarry)
            dxc_next[...] = jnp.zeros_like(dxc_next)
            for ref in (dcw_ref, dcb_ref, dwa_ref, dba_ref, dwi_ref, dbi_ref, dlam_ref):
                ref[...] = jnp.zeros_like(ref)

        ext = jnp.concatenate([jnp.where(t_idx > 0, halo_ref[...], 0.0), zx_ref[...]], axis=0)
        xc, r, ig, sp, a, mult, taps = _lru_gates(ext, cw_ref, cb_ref, wa_ref, ba_ref, wi_ref, bi_ref, lam_ref, tr)
        h = h_ref[...]
        gel, dgel = _gelu_parts(zy_ref[...])
        dy_ = dy_ref[...]
        dzy_ref[...] = (dy_ * h * dgel).astype(dzy_ref.dtype)
        row = _rows(tr, LRU_WIDTH)
        B = dy_ * gel + jnp.where(row == tr - 1, gcarry[...], 0.0)
        A = jnp.where(row < tr - 1, pltpu.roll(a, tr - 1, 0), 0.0)
        d = 1
        while d < tr:
            keep = row < tr - d
            b_sh = jnp.where(keep, pltpu.roll(B, tr - d, 0), 0.0)
            a_sh = jnp.where(keep, pltpu.roll(A, tr - d, 0), 0.0)
            B = B + A * b_sh
            A = A * a_sh
            d *= 2
        g = B
        tmp[...] = a * g
        gcarry[...] = tmp[pl.ds(0, 1), :]
        h_ext = jnp.concatenate([jnp.where(t_idx > 0, hprev_ref[...], 0.0), h], axis=0)
        hprev = pltpu.roll(h_ext, 1, 0)[LRU_HALO:]
        t1 = g * mult
        dig = t1 * xc
        dxc = t1 * ig
        dla = (g * hprev) * a - (g * (ig * xc)) * (a * a) / mult
        dr = dla * (-LRU_C * sp)
        dga = dr * r * (1.0 - r)
        dgi = dig * ig * (1.0 - ig)
        dlam_ref[...] += _colsum(dla * (-LRU_C * r)) * (-_sigmoid(-lam_ref[...]))
        dba_ref[...] += _colsum(dga)
        dbi_ref[...] += _colsum(dgi)
        parts = []
        for gidx in range(4):
            cols = slice(gidx * 128, (gidx + 1) * 128)
            xct = xc[:, cols].T
            dwa_ref[gidx] += _dot(xct, dga[:, cols])
            dwi_ref[gidx] += _dot(xct, dgi[:, cols])
            parts.append(_dot_nt(dga[:, cols], wa_ref[gidx]) + _dot_nt(dgi[:, cols], wi_ref[gidx]))
        dxc = dxc + jnp.concatenate(parts, axis=1)
        dcb_ref[...] += _colsum(dxc)
        for k in range(4):
            dcw_ref[k:k + 1, :] += _colsum(dxc * taps[k])
        ext_d = jnp.concatenate([dxc, dxc_next[...]], axis=0)
        dzx = dxc * cw_ref[3:4, :]
        for k in range(3):
            dzx = dzx + pltpu.roll(ext_d, tr + LRU_HALO - (3 - k), 0)[:tr] * cw_ref[k:k + 1, :]
        dzx_ref[...] = dzx.astype(dzx_ref.dtype)
        dxc_next[...] = dxc[:LRU_HALO]

    rev = pl.BlockSpec((tr, LRU_WIDTH), lambda i: (nt - 1 - i, 0))
    hb = tr // LRU_HALO
    hprev_spec = pl.BlockSpec((LRU_HALO, LRU_WIDTH), lambda i: (jnp.maximum((nt - 1 - i) * hb - 1, 0), 0))
    dy_spec = pl.BlockSpec((tr, LRU_WIDTH), lambda i: (nt - 1 - i, dy_b))
    vshape = jax.ShapeDtypeStruct((1, LRU_WIDTH), F32)
    bshape = jax.ShapeDtypeStruct((4, 128, 128), F32)
    return pl.pallas_call(
        body, name="lru_bwd", grid=(nt,),
        in_specs=[cur(zx_b), prev(zx_b), cur(zy_b), dy_spec, rev, hprev_spec, cwspec, vec, blk, vec, blk, vec, vec],
        out_specs=(rev, rev, cwspec, vec, blk, vec, blk, vec, vec),
        out_shape=(jax.ShapeDtypeStruct((S, LRU_WIDTH), BF16), jax.ShapeDtypeStruct((S, LRU_WIDTH), BF16),
                   jax.ShapeDtypeStruct((4, LRU_WIDTH), F32), vshape, bshape, vshape, bshape, vshape, vshape),
        scratch_shapes=[pltpu.VMEM((1, LRU_WIDTH), F32), pltpu.VMEM((LRU_HALO, LRU_WIDTH), F32),
                        pltpu.VMEM((tr, LRU_WIDTH), F32)],
        compiler_params=_params(("arbitrary",)),
    )(z, z, z, dy, hs, hs, cw, cb, wa, ba, wi, bi, lam)


FFN_HALO = 8


def _ffn_act(au, cw, cb):
    S, F2 = au.shape
    F = F2 // 2
    tc = FFN_CHUNK
    tr = _div_tile(S, TILES["row"], 8)
    hb = tr // FFN_HALO

    def body(au_ref, halo_ref, cw_ref, cb_ref, p_ref):
        i = pl.program_id(0)
        a_ = au_ref[:, :tc]
        ext = jnp.concatenate([jnp.where(i > 0, halo_ref[:, :tc], 0.0), a_], axis=0)
        gc = cb_ref[...] + a_ * cw_ref[2:3, :]
        for k in range(2):
            gc = gc + pltpu.roll(ext, 2 - k, 0)[FFN_HALO:] * cw_ref[k:k + 1, :]
        p_ref[...] = (gc * _sigmoid(gc) * au_ref[:, tc:]).astype(p_ref.dtype)

    return pl.pallas_call(
        body, name="ffn_act", grid=(S // tr, F // tc),
        in_specs=[pl.BlockSpec((tr, 2 * tc), lambda i, j: (i, j)),
                  pl.BlockSpec((FFN_HALO, 2 * tc), lambda i, j: (jnp.maximum(i * hb - 1, 0), j)),
                  pl.BlockSpec((3, tc), lambda i, j: (0, j)), pl.BlockSpec((1, tc), lambda i, j: (0, j))],
        out_specs=pl.BlockSpec((tr, tc), lambda i, j: (i, j)), out_shape=jax.ShapeDtypeStruct((S, F), BF16),
        compiler_params=_params(("parallel", "parallel")),
    )(au, au, cw, cb)


def _ffn_act_bwd(au, dp, cw, cb):
    S, F2 = au.shape
    F = F2 // 2
    tc = FFN_CHUNK
    tr = _div_tile(S, TILES["ew"], 8)
    hb = tr // FFN_HALO
    nt = S // tr
    H = FFN_HALO

    def body(au_ref, prev_ref, next_ref, dp_ref, dpn_ref, cw_ref, cb_ref, dau_ref, dcw_ref, dcb_ref):
        i = pl.program_id(1)
        last = i == nt - 1
        a_ext = jnp.concatenate([jnp.where(i > 0, prev_ref[:, :tc], 0.0), au_ref[:, :tc], next_ref[:, :tc]], axis=0)
        u_ext = jnp.concatenate([au_ref[:, tc:], next_ref[:, tc:]], axis=0)
        dp_ext = jnp.concatenate([dp_ref[...], jnp.where(last, 0.0, dpn_ref[...])], axis=0)
        taps = [pltpu.roll(a_ext, 2 - k, 0)[H:] if k < 2 else a_ext[H:] for k in range(3)]
        gc = cb_ref[...] + taps[0] * cw_ref[0:1, :] + taps[1] * cw_ref[1:2, :] + taps[2] * cw_ref[2:3, :]
        sig = _sigmoid(gc)
        dgc = dp_ext * u_ext * (sig * (1.0 + gc * (1.0 - sig)))
        da = dgc[:tr] * cw_ref[2:3, :]
        for k in range(2):
            da = da + pltpu.roll(dgc, tr + H - (2 - k), 0)[:tr] * cw_ref[k:k + 1, :]
        dau_ref[:, :tc] = da.astype(dau_ref.dtype)
        dau_ref[:, tc:] = (dp_ref[...] * (gc[:tr] * sig[:tr])).astype(dau_ref.dtype)

        @pl.when(i == 0)
        def _():
            dcw_ref[...] = jnp.zeros_like(dcw_ref)
            dcb_ref[...] = jnp.zeros_like(dcb_ref)

        dcb_ref[...] += _colsum(dgc[:tr])
        for k in range(3):
            dcw_ref[k:k + 1, :] += _colsum(dgc[:tr] * taps[k][:tr])

    return pl.pallas_call(
        body, name="ffn_act_bwd", grid=(F // tc, nt),
        in_specs=[pl.BlockSpec((tr, 2 * tc), lambda j, i: (i, j)),
                  pl.BlockSpec((H, 2 * tc), lambda j, i: (jnp.maximum(i * hb - 1, 0), j)),
                  pl.BlockSpec((H, 2 * tc), lambda j, i: (jnp.minimum((i + 1) * hb, nt * hb - 1), j)),
                  pl.BlockSpec((tr, tc), lambda j, i: (i, j)),
                  pl.BlockSpec((H, tc), lambda j, i: (jnp.minimum((i + 1) * hb, nt * hb - 1), j)),
                  pl.BlockSpec((3, tc), lambda j, i: (0, j)), pl.BlockSpec((1, tc), lambda j, i: (0, j))],
        out_specs=(pl.BlockSpec((tr, 2 * tc), lambda j, i: (i, j)), pl.BlockSpec((3, tc), lambda j, i: (0, j)),
                   pl.BlockSpec((1, tc), lambda j, i: (0, j))),
        out_shape=(jax.ShapeDtypeStruct((S, F2), BF16), jax.ShapeDtypeStruct((3, F), F32), jax.ShapeDtypeStruct((1, F), F32)),
        compiler_params=_params(("parallel", "arbitrary")),
    )(au, au, au, dp, dp, cw, cb)


def _row_tile(rows, cols):
    return _div_tile(rows, max(16, (2**18 // cols) // 16 * 16), 16)


def _sum_parts(parts, sel, rows, cols, name):
    tr = _row_tile(rows, cols)

    def body(sel_ref, *refs):
        acc = refs[0][...].astype(F32)
        for r in refs[1:-1]:
            acc = acc + r[...].astype(F32)
        refs[-1][...] = acc

    def spec(index):
        if isinstance(index, int):
            return pl.BlockSpec((None, tr, cols), lambda i, s: (index, i, 0))
        k, mul, off = index
        return pl.BlockSpec((None, tr, cols), lambda i, s: (s[k] * mul + off, i, 0))

    grid_spec = pltpu.PrefetchScalarGridSpec(
        num_scalar_prefetch=1, grid=(rows // tr,), in_specs=[spec(ix) for _, ix in parts],
        out_specs=pl.BlockSpec((tr, cols), lambda i, s: (i, 0)))
    return pl.pallas_call(
        body, name=name, grid_spec=grid_spec, out_shape=jax.ShapeDtypeStruct((rows, cols), F32),
        compiler_params=_params(("parallel",)),
    )(sel, *[a for a, _ in parts])


def _pair_sum(g, got, sel):
    n, rows, cols = got.shape
    tr = _row_tile(rows, cols)
    nb = rows // tr

    def body(sel_ref, a_ref, b_ref, o_ref):
        o_ref[...] = (a_ref[...] + b_ref[...]).astype(o_ref.dtype)

    grid_spec = pltpu.PrefetchScalarGridSpec(
        num_scalar_prefetch=1, grid=(n, nb),
        in_specs=[pl.BlockSpec((None, tr, cols), lambda q, i, s: (q, s[0] * nb + i, 0)),
                  pl.BlockSpec((None, tr, cols), lambda q, i, s: (q, i, 0))],
        out_specs=pl.BlockSpec((None, tr, cols), lambda q, i, s: (q, i, 0)))
    return pl.pallas_call(
        body, name="pair_sum", grid_spec=grid_spec, out_shape=jax.ShapeDtypeStruct((n, rows, cols), BF16),
        compiler_params=_params(("parallel", "parallel")),
    )(sel, g, got)


def _chip_sum_cols(pair, arrived, sel):
    n, rows, cg = arrived.shape[1:]
    tr = _row_tile(rows, cg)

    def body(sel_ref, p_ref, a0, a1, a2, o_ref):
        o_ref[...] = ((p_ref[...].astype(F32) + a0[...].astype(F32)) + a1[...].astype(F32)) + a2[...].astype(F32)

    def arr(j):
        return pl.BlockSpec((None, None, tr, cg), lambda k, i, s: (j, k, i, 0))

    grid_spec = pltpu.PrefetchScalarGridSpec(
        num_scalar_prefetch=1, grid=(n, rows // tr),
        in_specs=[pl.BlockSpec((None, tr, cg), lambda k, i, s: (k, i, s[1])), arr(0), arr(1), arr(2)],
        out_specs=pl.BlockSpec((None, tr, cg), lambda k, i, s: (k, i, 0)))
    return pl.pallas_call(
        body, name="chip_sum_cols", grid_spec=grid_spec, out_shape=jax.ShapeDtypeStruct((n, rows, cg), F32),
        compiler_params=_params(("parallel", "parallel")),
    )(sel, pair, arrived, arrived, arrived)


def _adamw_math(w, g, m, v):
    m2 = ADAM_B1 * m + (1.0 - ADAM_B1) * g
    v2 = ADAM_B2 * v + (1.0 - ADAM_B2) * (g * g)
    m_hat = m2 / (1.0 - ADAM_B1 ** ADAM_STEP)
    v_hat = v2 / (1.0 - ADAM_B2 ** ADAM_STEP)
    return -ADAM_LR * (m_hat / (jnp.sqrt(v_hat) + ADAM_EPS) + ADAM_WD * w), m2, v2


def _adamw(w, g, m, v):
    R, C = w.shape
    tr = _div_tile(R, TILES["ew"], 8)

    def body(w_ref, g_ref, m_ref, v_ref, d_ref, m2_ref, v2_ref):
        d_ref[...], m2_ref[...], v2_ref[...] = _adamw_math(w_ref[...], g_ref[...], m_ref[...], v_ref[...])

    spec = pl.BlockSpec((tr, C), lambda i: (i, 0))
    shape = jax.ShapeDtypeStruct((R, C), F32)
    return pl.pallas_call(
        body, name="adamw", grid=(R // tr,), in_specs=[spec] * 4, out_specs=(spec,) * 3, out_shape=(shape,) * 3,
        compiler_params=_params(("parallel",)),
    )(w, g, m, v)


def _ada_grad_adamw(cact_t, dmod, w, m, v):
    L, D, N = w.shape
    tm, tn = _div_tile(D, 256, 8), _div_tile(N, 1024, 128)

    def body(c_ref, d_ref, w_ref, m_ref, v_ref, g_ref, dl_ref, m2_ref, v2_ref):
        g = c_ref[:, 0:1] * d_ref[0:1, :]
        for b in range(1, N_DEV):
            g = g + c_ref[:, b:b + 1] * d_ref[b:b + 1, :]
        g_ref[...] = g
        dl_ref[...], m2_ref[...], v2_ref[...] = _adamw_math(w_ref[...], g, m_ref[...], v_ref[...])

    big = pl.BlockSpec((None, tm, tn), lambda l, i, j: (l, i, j))
    shape = jax.ShapeDtypeStruct((L, D, N), F32)
    return pl.pallas_call(
        body, name="ada_grad_adamw", grid=(L, D // tm, N // tn),
        in_specs=[pl.BlockSpec((tm, N_DEV), lambda l, i, j: (i, 0)),
                  pl.BlockSpec((None, N_DEV, tn), lambda l, i, j: (l, 0, j)), big, big, big],
        out_specs=(big,) * 4, out_shape=(shape,) * 4, compiler_params=_params(("parallel", "parallel", "parallel")),
    )(cact_t, dmod, w, m, v)


def _silu_rows(c):
    def body(c_ref, o_ref):
        x = c_ref[...]
        o_ref[...] = x * _sigmoid(x)

    return pl.pallas_call(body, name="silu_rows", out_shape=jax.ShapeDtypeStruct(c.shape, F32))(c)


ANY = pl.BlockSpec(memory_space=pl.ANY)


def _position():
    return lax.axis_index("x"), lax.axis_index("y"), lax.axis_index("c")


def _other_chips(x, y):
    return [(1 - x, y), (x, 1 - y), (1 - x, 1 - y)]


def _allgather8(v):
    R, C = v.shape

    def body(v_ref, out_ref, send_sems, recv_sems):
        x, y, c = _position()
        me = 4 * x + 2 * y + c
        sends, recvs = [], []
        for k in range(1, N_DEV):
            px, py, pc = (x + (k >> 2)) % 2, (y + ((k >> 1) & 1)) % 2, (c + (k & 1)) % 2
            sends.append(pltpu.make_async_remote_copy(
                src_ref=v_ref, dst_ref=out_ref.at[me], send_sem=send_sems.at[k - 1], recv_sem=recv_sems.at[k - 1],
                device_id=(px, py, pc), device_id_type=MESH))
            recvs.append(pltpu.make_async_remote_copy(
                src_ref=v_ref, dst_ref=out_ref.at[4 * px + 2 * py + pc], send_sem=send_sems.at[k - 1],
                recv_sem=recv_sems.at[k - 1], device_id=(px, py, pc), device_id_type=MESH))
        for cp in sends:
            cp.start()
        for cp in recvs:
            cp.wait_recv()
        for cp in sends:
            cp.wait_send()

    others = pl.pallas_call(
        body, name="comm_allgather8", out_shape=jax.ShapeDtypeStruct((N_DEV, R, C), v.dtype), in_specs=[ANY],
        out_specs=ANY, scratch_shapes=[pltpu.SemaphoreType.DMA((N_DEV - 1,)), pltpu.SemaphoreType.DMA((N_DEV - 1,))],
    )(v)
    x, y, c = _position()
    return lax.dynamic_update_slice(others, v[None], (4 * x + 2 * y + c, 0, 0))


def _remote(src, dst, send_sems, recv_sems, k, to):
    return pltpu.make_async_remote_copy(src_ref=src, dst_ref=dst, send_sem=send_sems.at[k], recv_sem=recv_sems.at[k],
                                        device_id=to, device_id_type=MESH)


def _half_rows(ref, h):
    n = ref.shape[0] // 2
    return ref.at[pl.ds(h * n, n)]


N_SHARDS = 5


class _GatherPlan:
    def __init__(self, ins, outs, sems):
        win, wout, gate, up, down = ins
        win4, wout4, gu, down4 = outs
        send_sems, recv_sems, local_sems = sems
        x, y, c = _position()
        q = 2 * x + y
        sibling = (x, y, 1 - c)
        CG = gate.shape[1]
        tensors = [(win, lambda p: win4.at[p]), (wout, lambda p: wout4.at[p]),
                   (gate, lambda p: gu.at[0, :, pl.ds(p * CG, CG)]), (up, lambda p: gu.at[1, :, pl.ds(p * CG, CG)]),
                   (down, lambda p: down4.at[p])]
        self.own, self.first, self.landed, self.passed, self.passed_landed = [], [], [], [], []
        for t, (src, dst) in enumerate(tensors):
            self.own.append(pltpu.make_async_copy(src, dst(q), local_sems.at[t]))
            for j, (px, py) in enumerate(_other_chips(x, y)):
                self.first.append(_remote(_half_rows(src, c), _half_rows(dst(q), c), send_sems, recv_sems, 6 * t + j,
                                          (px, py, c)))
                mine = _half_rows(dst(2 * px + py), c)
                self.landed.append(_remote(mine, mine, send_sems, recv_sems, 6 * t + j, (px, py, c)))
                self.passed.append(_remote(mine, mine, send_sems, recv_sems, 6 * t + 3 + j, sibling))
                other = _half_rows(dst(2 * px + py), 1 - c)
                self.passed_landed.append(_remote(other, other, send_sems, recv_sems, 6 * t + 3 + j, sibling))

    def start(self):
        for cp in self.own + self.first:
            cp.start()

    def forward(self):
        for landed, onward in zip(self.landed, self.passed):
            landed.wait_recv()
            onward.start()

    def finish(self):
        for cp in self.passed_landed:
            cp.wait_recv()
        for cp in self.first + self.passed:
            cp.wait_send()
        for cp in self.own:
            cp.wait()


def _gather_shapes(shards):
    win_s, wout_s, gate_s, up_s, down_s = shards
    return (jax.ShapeDtypeStruct((N_CHIPS,) + win_s.shape, BF16), jax.ShapeDtypeStruct((N_CHIPS,) + wout_s.shape, BF16),
            jax.ShapeDtypeStruct((2, gate_s.shape[0], N_CHIPS * gate_s.shape[1]), BF16),
            jax.ShapeDtypeStruct((N_CHIPS,) + down_s.shape, BF16))


GATHER_SEMS = [pltpu.SemaphoreType.DMA((6 * N_SHARDS,)), pltpu.SemaphoreType.DMA((6 * N_SHARDS,)),
               pltpu.SemaphoreType.DMA((N_SHARDS,))]


def _gather_weights(shards):
    def body(*refs):
        plan = _GatherPlan(refs[:N_SHARDS], refs[N_SHARDS:N_SHARDS + 4], refs[N_SHARDS + 4:])
        plan.start()
        plan.forward()
        plan.finish()

    return pl.pallas_call(
        body, name="comm_gather_weights", out_shape=_gather_shapes(shards), in_specs=[ANY] * N_SHARDS,
        out_specs=(ANY,) * 4, scratch_shapes=GATHER_SEMS,
    )(*shards)


def _sibling_swap_halves(gs):
    n_t = len(gs)

    def body(*refs):
        ins, outs, (send_sems, recv_sems) = refs[:n_t], refs[n_t:2 * n_t], refs[2 * n_t:]
        x, y, c = _position()
        cps = []
        for t in range(n_t):
            half = ins[t].shape[1] // 2
            cps.append(_remote(ins[t].at[:, pl.ds((1 - c) * half, half), :], outs[t], send_sems, recv_sems, t,
                               (x, y, 1 - c)))
        for cp in cps:
            cp.start()
        for cp in cps:
            cp.wait()

    shapes = tuple(jax.ShapeDtypeStruct((g.shape[0], g.shape[1] // 2, g.shape[2]), g.dtype) for g in gs)
    return pl.pallas_call(
        body, name="comm_sibling_swap", out_shape=shapes, in_specs=[ANY] * n_t, out_specs=(ANY,) * n_t,
        scratch_shapes=[pltpu.SemaphoreType.DMA((n_t,)), pltpu.SemaphoreType.DMA((n_t,))],
    )(*gs)


N_PAIRS = 4
SCATTER_COPIES = 3 * (N_PAIRS + 1)
SCATTER_SEMS = [pltpu.SemaphoreType.DMA((SCATTER_COPIES,)), pltpu.SemaphoreType.DMA((SCATTER_COPIES,))]


def _scatter_copies(ins, outs, sems):
    send_sems, recv_sems = sems
    n_l = N_PAIRS - 1
    CG = ins[n_l].shape[2] // N_CHIPS
    x, y, c = _position()
    cps = []
    for j, (px, py) in enumerate(_other_chips(x, y)):
        p = 2 * px + py
        for t in range(n_l):
            cps.append(_remote(ins[t].at[p], outs[t].at[j], send_sems, recv_sems, len(cps), (px, py, c)))
        for k in range(2):
            cps.append(_remote(ins[n_l].at[k, :, pl.ds(p * CG, CG)], outs[n_l].at[j, k], send_sems, recv_sems, len(cps),
                               (px, py, c)))
    return cps


def _scatter_shapes(pairs):
    gu = pairs[-1]
    return tuple(jax.ShapeDtypeStruct((3,) + g.shape[1:], g.dtype) for g in pairs[:-1]) + (
        jax.ShapeDtypeStruct((3, 2, gu.shape[1], gu.shape[2] // N_CHIPS), gu.dtype),)


def _chip_scatter(pairs):
    def body(*refs):
        cps = _scatter_copies(refs[:N_PAIRS], refs[N_PAIRS:2 * N_PAIRS], refs[2 * N_PAIRS:])
        for cp in cps:
            cp.start()
        for cp in cps:
            cp.wait()

    return pl.pallas_call(
        body, name="comm_chip_scatter", out_shape=_scatter_shapes(pairs), in_specs=[ANY] * N_PAIRS,
        out_specs=(ANY,) * N_PAIRS, scratch_shapes=SCATTER_SEMS,
    )(*pairs)


def _sibling_exchange(vs):
    n_t = len(vs)

    def body(*refs):
        ins, outs, (send_sems, recv_sems) = refs[:n_t], refs[n_t:2 * n_t], refs[2 * n_t:]
        x, y, c = _position()
        cps = [_remote(ins[t], outs[t], send_sems, recv_sems, t, (x, y, 1 - c)) for t in range(n_t)]
        for cp in cps:
            cp.start()
        for cp in cps:
            cp.wait()

    shapes = tuple(jax.ShapeDtypeStruct(v.shape, v.dtype) for v in vs)
    return pl.pallas_call(
        body, name="comm_sibling_exchange", out_shape=shapes, in_specs=[ANY] * n_t, out_specs=(ANY,) * n_t,
        scratch_shapes=[pltpu.SemaphoreType.DMA((n_t,)), pltpu.SemaphoreType.DMA((n_t,))],
    )(*vs)


def _join_halves(mine, theirs, sel):
    n, rows, cols = mine.shape
    tr = _row_tile(rows, cols)
    nb = rows // tr

    def body(sel_ref, a_ref, b_ref, o_ref):
        @pl.when(pl.program_id(1) == sel_ref[0])
        def _():
            o_ref[...] = a_ref[...]

        @pl.when(pl.program_id(1) != sel_ref[0])
        def _():
            o_ref[...] = b_ref[...]

    grid_spec = pltpu.PrefetchScalarGridSpec(
        num_scalar_prefetch=1, grid=(n, 2, nb),
        in_specs=[pl.BlockSpec((None, tr, cols), lambda k, h, i, s: (k, jnp.where(h == s[0], i, 0), 0)),
                  pl.BlockSpec((None, tr, cols), lambda k, h, i, s: (k, jnp.where(h == s[0], 0, i), 0))],
        out_specs=pl.BlockSpec((None, tr, cols), lambda k, h, i, s: (k, h * nb + i, 0)))
    return pl.pallas_call(
        body, name="join_halves", grid_spec=grid_spec, out_shape=jax.ShapeDtypeStruct((n, 2 * rows, cols), mine.dtype),
        compiler_params=_params(("parallel", "arbitrary", "arbitrary")),
    )(sel, mine, theirs)


def _reduce_start(grads, sel):
    got = _sibling_swap_halves(grads)
    return [_pair_sum(g, r, sel) for g, r in zip(grads, got)]


def _reduce_finish(pairs, arrived, sel):
    mine = [_sum_parts([(p, (1, 1, 0)), (a, 0), (a, 1), (a, 2)], sel, p.shape[1], p.shape[2], "chip_sum")
            for p, a in zip(pairs[:-1], arrived[:-1])]
    mine.append(_chip_sum_cols(pairs[-1], arrived[-1], sel))
    theirs = _sibling_exchange(mine)
    joined = [_join_halves(m.reshape((-1,) + m.shape[-2:]), t.reshape((-1,) + t.shape[-2:]), sel)
              for m, t in zip(mine, theirs)]
    return [j[0] for j in joined[:-1]] + [joined[-1]]


def _layer_fwd(x, mod, p, next_shards=None):
    sh1, sc1, gt1, sh2, sc2, gt2 = mod
    h1 = _norm_mod(x, p["g_mix"], sc1, sh1)
    z = _matmul(h1, p["w_in"], name="mm_in")
    y_pool = _pool_fwd(z, p["pool_w"], p["pool_scale"])
    F = _forget_cumsum(z, p["b_f"])
    Fh = F[:, :N_HEADS].T
    f_col, f_row = Fh[:, :, None], Fh[:, None, :]
    o_t, lse, *gathered = _flash_fwd(z, z[:, ZV:ZV + ATTN_WIDTH].T.astype(BF16), f_col, f_row, next_shards)
    y_lru, hs = _lru_fwd(z, p["lru_conv_w"], p["lru_conv_b"], p["lru_wa"], p["lru_ba"], p["lru_wi"], p["lru_bi"],
                         p["lru_lambda"])
    y = jnp.concatenate([y_pool.astype(BF16), o_t.T.astype(BF16), y_lru.astype(BF16)], axis=1)
    m1, x_mid = _matmul(y, p["w_out"], res=x, gate=gt1, name="mm_out")
    h2 = _norm_mod(x_mid, p["g_ffn"], sc2, sh2)
    au = _matmul(h2, p["w_gu"], gu="b", name="mm_gu")
    pa = _ffn_act(au, p["ffn_conv_w"], p["ffn_conv_b"])
    m2, x_out = _matmul(pa, p["w_down"], res=x_mid, gate=gt2, name="mm_down")
    saved = dict(x=x, h1=h1, z=z, f_col=f_col, f_row=f_row, o_t=o_t, lse=lse, hs=hs, y=y, m1=m1, x_mid=x_mid, h2=h2, au=au,
                 pa=pa, m2=m2)
    return x_out, saved, gathered


def _layer_bwd(dx_out, mod, p, s, pairs=None):
    sh1, sc1, gt1, sh2, sc2, gt2 = mod
    g = {}
    dm2, dgt2 = _gate_bwd(dx_out, s["m2"], gt2)
    dpa = _matmul(dm2, p["w_down"], nt=True, name="mm_down_dx")
    g["w_down"] = _matmul(s["pa"], dm2, ta=True, name="mm_down_dw")
    dau, g["ffn_conv_w"], g["ffn_conv_b"] = _ffn_act_bwd(s["au"], dpa, p["ffn_conv_w"], p["ffn_conv_b"])
    dh2 = _matmul(dau, p["w_gu"], nt=True, gu="b", name="mm_gu_dx")
    g["w_gu"] = _matmul(s["h2"], dau, ta=True, gu="out", name="mm_gu_dw")
    dx_mid, g["g_ffn"], dsc2, dsh2 = _norm_mod_bwd(s["x_mid"], dh2, dx_out, p["g_ffn"], sc2)
    dm1, dgt1 = _gate_bwd(dx_mid, s["m1"], gt1)
    dy = _matmul(dm1, p["w_out"], nt=True, name="mm_out_dx")
    g["w_out"] = _matmul(s["y"], dm1, ta=True, name="mm_out_dw")
    z = s["z"]
    (dzx, dzy, g["lru_conv_w"], g["lru_conv_b"], g["lru_wa"], g["lru_ba"], g["lru_wi"], g["lru_bi"],
     g["lru_lambda"]) = _lru_bwd(z, dy, s["hs"], p["lru_conv_w"], p["lru_conv_b"], p["lru_wa"], p["lru_ba"], p["lru_wi"],
                                 p["lru_bi"], p["lru_lambda"])
    k_t = z[:, ZK:ZK + ATTN_WIDTH].T.astype(BF16)
    do_t = dy[:, POOL_WIDTH:POOL_WIDTH + ATTN_WIDTH].T
    dq_t, dk, dv, dfk, dfq, *arrived = _flash_bwd(z, k_t, dy, do_t, s["o_t"], s["lse"], s["f_col"], s["f_row"], pairs)
    dq = dq_t.T
    dF_pad = jnp.pad((dfq[:, 0, :] + dfk[:, :, 0]).T, ((0, 0), (0, 128 - N_HEADS)))
    dzf, db_f = _forget_cumsum_bwd(z, p["b_f"], dF_pad)
    g["b_f"] = db_f[:, :N_HEADS]
    dzp, g["pool_w"], g["pool_scale"] = _pool_bwd(z, dy, p["pool_w"], p["pool_scale"])
    S = z.shape[0]
    dz = jnp.concatenate([dzp, dq, dk, dv, dzx, dzy, dzf, jnp.zeros((S, ZW - ZF - 128), BF16)], axis=1)
    dh1 = _matmul(dz, p["w_in"], nt=True, name="mm_in_dx")
    g["w_in"] = _matmul(s["h1"], dz, ta=True, name="mm_in_dw")
    dx_in, g["g_mix"], dsc1, dsh1 = _norm_mod_bwd(s["x"], dh1, dx_mid, p["g_mix"], sc1)
    return dx_in, g, (dsh1, dsc1, dgt1, dsh2, dsc2, dgt2), arrived


def _big_weights(gathered):
    win4, wout4, gu, down4 = gathered
    D, F = win4.shape[1], gu.shape[2]
    return dict(w_in=_pad_in_cols(jnp.transpose(win4, (1, 0, 2)).reshape(D, N_IN)), w_out=wout4.reshape(D, D), w_gu=gu,
                w_down=down4.reshape(F, D))


def _big_grads(g):
    D, F = g["w_out"].shape[0], g["w_down"].shape[0]
    dwin4 = jnp.transpose(_unpad_in_cols(g["w_in"]).reshape(D, N_CHIPS, N_IN // N_CHIPS), (1, 0, 2))
    return [dwin4, g["w_out"].reshape(N_CHIPS, D // N_CHIPS, D), g["w_down"].reshape(N_CHIPS, F // N_CHIPS, D), g["w_gu"]]


def _local_step(x, target, mods, layers, final_g, shards=None, sel=None):
    L = len(layers)
    saved, params = [], []
    gathered = _gather_weights(shards[0]) if shards else None
    for l in range(L):
        p = {**layers[l], **_big_weights(gathered)} if shards else layers[l]
        x, s, gathered = _layer_fwd(x, mods[l], p, shards[l + 1] if shards and l + 1 < L else None)
        saved.append(s)
        params.append(p)
    dx, dfinal_g, loss = _loss_head(x, target, final_g)
    grads, dmods, reduced, pairs = [None] * L, [None] * L, [None] * L, None
    for l in reversed(range(L)):
        dx, grads[l], dmods[l], arrived = _layer_bwd(dx, mods[l], params[l], saved[l], pairs)
        if shards:
            if pairs is not None:
                reduced[l + 1] = _reduce_finish(pairs, arrived, sel)
            pairs = _reduce_start(_big_grads(grads[l]), sel)
    if shards:
        reduced[0] = _reduce_finish(pairs, _chip_scatter(pairs), sel)
    return loss, dx, grads, dmods, dfinal_g, reduced


def _pad_in_cols(w):
    D = w.shape[0]
    return jnp.concatenate([w[:, :3584], w[:, 3592:N_IN], w[:, 3584:3592], jnp.zeros((D, ZW - N_IN), w.dtype)], axis=1)


def _unpad_in_cols(w):
    return jnp.concatenate([w[:, :3584], w[:, ZF:ZF + N_HEADS], w[:, 3584:ZF]], axis=1)


BIG = ("w_in", "w_out", "w_ffn_gate", "w_ffn_up", "w_ffn_down")
SMALL = ("b_ada", "g_mix", "b_f", "pool_w", "pool_scale", "lru_conv_w", "lru_conv_b", "lru_wa", "lru_ba", "lru_wi",
         "lru_bi", "lru_lambda", "g_ffn", "ffn_conv_w", "ffn_conv_b", "final_g")
SHARDED_SMALL = ("lru_conv_w", "ffn_conv_w")
WEIGHTS = ("w_ada", "b_ada", "g_mix", "w_in", "b_f", "pool_w", "pool_scale", "lru_conv_w", "lru_conv_b", "lru_wa",
           "lru_ba", "lru_wi", "lru_bi", "lru_lambda", "w_out", "g_ffn", "w_ffn_gate", "w_ffn_up", "ffn_conv_w",
           "ffn_conv_b", "w_ffn_down", "final_g")


PACK_QUANTUM = 512 * 128


def _pack(arrays):
    flat = jnp.concatenate([a.reshape(-1).astype(F32) for a in arrays])
    n = -(-flat.shape[0] // PACK_QUANTUM) * PACK_QUANTUM
    return jnp.pad(flat, (0, n - flat.shape[0])).reshape(n // 128, 128)


def _unpack(packed, shapes):
    flat = packed.reshape(-1)
    out, off = [], 0
    for shp in shapes:
        n = int(np.prod(shp))
        out.append(flat[off:off + n].reshape(shp))
        off += n
    return out


def kernel(x, c, w_ada, b_ada, g_mix, w_in, b_f, pool_w, pool_scale, lru_conv_w, lru_conv_b, lru_wa, lru_ba, lru_wi, lru_bi, lru_lambda, w_out, g_ffn, w_ffn_gate, w_ffn_up, ffn_conv_w, ffn_conv_b, w_ffn_down, final_g, loss_target, m_w_ada, m_b_ada, m_g_mix, m_w_in, m_b_f, m_pool_w, m_pool_scale, m_lru_conv_w, m_lru_conv_b, m_lru_wa, m_lru_ba, m_lru_wi, m_lru_bi, m_lru_lambda, m_w_out, m_g_ffn, m_w_ffn_gate, m_w_ffn_up, m_ffn_conv_w, m_ffn_conv_b, m_w_ffn_down, m_final_g, v_w_ada, v_b_ada, v_g_mix, v_w_in, v_b_f, v_pool_w, v_pool_scale, v_lru_conv_w, v_lru_conv_b, v_lru_wa, v_lru_ba, v_lru_wi, v_lru_bi, v_lru_lambda, v_w_out, v_g_ffn, v_w_ffn_gate, v_w_ffn_up, v_ffn_conv_w, v_ffn_conv_b, v_w_ffn_down, v_final_g):
    env = dict(locals())
    W = {n: env[n] for n in WEIGHTS}
    M = {n: env["m_" + n] for n in WEIGHTS}
    V = {n: env["v_" + n] for n in WEIGHTS}
    L, D = g_mix.shape
    S = x.shape[1]
    F = 4 * w_ffn_gate.shape[2]
    ix, iy, ic = lax.axis_index("x"), lax.axis_index("y"), lax.axis_index("c")
    q = 2 * ix + iy
    me = 2 * q + ic

    head = _allgather8(_pack([c, lru_conv_w, ffn_conv_w]))
    nlc, nfc = lru_conv_w.size, ffn_conv_w.size
    c_all = head.reshape(N_DEV, -1)[:, :D]
    lru_cw = jnp.concatenate([head[2 * k].reshape(-1)[D:D + nlc].reshape(L, 4, -1) for k in range(N_CHIPS)], axis=2)
    ffn_cw = jnp.concatenate([head[2 * k].reshape(-1)[D + nlc:D + nlc + nfc].reshape(L, 3, -1) for k in range(N_CHIPS)],
                             axis=2)
    cact = _silu_rows(c_all)

    NA = w_ada.shape[2]
    mod_part = jnp.stack([_matmul(cact, w_ada[l], name="mm_ada") for l in range(L)])
    mod_all = _allgather8(mod_part.reshape(L * N_DEV, NA)).reshape(N_DEV, L, N_DEV, NA)
    mod_full = jnp.concatenate([mod_all[2 * k] for k in range(N_CHIPS)], axis=2)
    mod_mine = lax.dynamic_index_in_dim(mod_full, me, axis=1, keepdims=False) + b_ada
    mods = [[mod_mine[l, k * D:(k + 1) * D].reshape(1, D) for k in range(6)] for l in range(L)]

    shards_all = [W[n].astype(BF16) for n in BIG]
    shards = [[w[l] for w in shards_all] for l in range(L)]
    layers = []
    for l in range(L):
        layers.append(dict(
            g_mix=g_mix[l][None], g_ffn=g_ffn[l][None], b_f=jnp.pad(b_f[l], (0, 128 - N_HEADS))[None],
            pool_w=pool_w[l], pool_scale=pool_scale[l][None], lru_conv_w=lru_cw[l], lru_conv_b=lru_conv_b[l][None],
            lru_wa=lru_wa[l], lru_ba=lru_ba[l][None], lru_wi=lru_wi[l], lru_bi=lru_bi[l][None],
            lru_lambda=lru_lambda[l][None], ffn_conv_w=ffn_cw[l], ffn_conv_b=ffn_conv_b[l][None]))

    sel = jnp.stack([ic, q]).astype(jnp.int32)
    loss, dx, grads, dmods, dfinal_g, reduced = _local_step(x[0], loss_target[0], mods, layers, final_g[None], shards, sel)

    G = {n: [] for n in BIG}
    for g_in, g_out, g_down, g_gu in reduced:
        for n, a in zip(BIG, (g_in, g_out, g_gu[0], g_gu[1], g_down)):
            G[n].append(a)
    G = {n: jnp.stack(G[n]) for n in BIG}

    stack = lambda name: jnp.stack([grads[l][name] for l in range(L)])
    dmod = jnp.stack([jnp.concatenate(dmods[l], axis=1)[0] for l in range(L)])
    small = dict(b_ada=dmod, g_mix=stack("g_mix"), b_f=stack("b_f"), pool_w=stack("pool_w"),
                 pool_scale=stack("pool_scale"), lru_conv_w=stack("lru_conv_w"), lru_conv_b=stack("lru_conv_b"),
                 lru_wa=stack("lru_wa"), lru_ba=stack("lru_ba"), lru_wi=stack("lru_wi"), lru_bi=stack("lru_bi"),
                 lru_lambda=stack("lru_lambda"), g_ffn=stack("g_ffn"), ffn_conv_w=stack("ffn_conv_w"),
                 ffn_conv_b=stack("ffn_conv_b"), final_g=dfinal_g)
    packed = _pack([small[n] for n in SMALL] + [loss[0, :1]])
    everyone = _allgather8(packed)
    zero_sel = jnp.zeros((2,), jnp.int32)
    total = _sum_parts([(everyone, k) for k in range(N_DEV)], zero_sel, packed.shape[0], 128, "device_sum")
    sums = _unpack(total, [small[n].shape for n in SMALL] + [(1,)])
    loss_total = sums[-1][0]
    for n, a in zip(SMALL, sums[:-1]):
        a = a.reshape((L, -1, a.shape[-1])) if n in SHARDED_SMALL else a.reshape(W[n].shape)
        if n in SHARDED_SMALL:
            a = lax.dynamic_slice_in_dim(a, q * W[n].shape[2], W[n].shape[2], axis=2)
        G[n] = a

    dmod_all = everyone.reshape(N_DEV, -1)[:, :L * 6 * D].reshape(N_DEV, L, 6 * D)
    dmod_cols = jnp.transpose(lax.dynamic_slice_in_dim(dmod_all, q * NA, NA, axis=2), (1, 0, 2))
    G["w_ada"], d_ada, m_ada, v_ada = _ada_grad_adamw(cact.T, dmod_cols, w_ada, m_w_ada, v_w_ada)
    delta, new_m, new_v = {"w_ada": d_ada}, {"w_ada": m_ada}, {"w_ada": v_ada}

    for n in BIG:
        cols = W[n].shape[-1]
        outs = _adamw(*[a.reshape(-1, cols) for a in (W[n], G[n], M[n], V[n])])
        delta[n], new_m[n], new_v[n] = [o.reshape(W[n].shape) for o in outs]
    outs = _adamw(*[_pack([t[n] for n in SMALL]) for t in (W, G, M, V)])
    shapes = [W[n].shape for n in SMALL]
    for tgt, o in zip((delta, new_m, new_v), outs):
        for n, a in zip(SMALL, _unpack(o, shapes)):
            tgt[n] = a

    return (loss_total, dx[None], *[G[n] for n in WEIGHTS], *[delta[n] for n in WEIGHTS],
            *[new_m[n] for n in WEIGHTS], *[new_v[n] for n in WEIGHTS])
```

```python
import functools
import math

import jax
import jax.numpy as jnp
import numpy as np
from jax import lax
from jax.experimental import pallas as pl
from jax.experimental.pallas import tpu as pltpu

F32 = jnp.float32
BF16 = jnp.bfloat16
MESH = pl.DeviceIdType.MESH

EPS = 1e-6
HEAD_DIM = 128
POOL_WIDTH = 512
POOL_WINDOWS = (2, 4, 8, 16)
ATTN_WIDTH = 1024
N_HEADS = 8
LRU_WIDTH = 512
LRU_C = 8.0
N_IN = 4616
ZP, ZQ, ZK, ZV, ZX, ZY, ZF, ZW = 0, 512, 1536, 2560, 3584, 4096, 4608, 5120
FFN_CHUNK = 512
N_CHIPS = 4
N_DEV = 8

ADAM_LR, ADAM_B1, ADAM_B2, ADAM_EPS, ADAM_WD, ADAM_STEP = 0.001, 0.9, 0.999, 1e-08, 0.01, 10

TILES = dict(mm_m=512, mm_n=1024, mm_k=2048, row=512, attn=512, lru=256, cum=512, ew=256)
VMEM_LIMIT = 48 * 2**20


def _params(sem):
    return pltpu.CompilerParams(dimension_semantics=sem, vmem_limit_bytes=VMEM_LIMIT)


def _div_tile(n, pref, align):
    if n <= pref:
        return n
    t = (pref // align) * align
    while t >= align:
        if n % t == 0:
            return t
        t -= align
    raise ValueError(f"no tile for {n}")


def _sigmoid(x):
    return 0.5 * jnp.tanh(0.5 * x) + 0.5


def _gelu_parts(x):
    k = math.sqrt(2.0 / math.pi)
    u = k * (x + 0.044715 * x * x * x)
    t = jnp.tanh(u)
    gel = 0.5 * x * (1.0 + t)
    dgel = 0.5 * (1.0 + t) + 0.5 * x * (1.0 - t * t) * k * (1.0 + 3 * 0.044715 * x * x)
    return gel, dgel


def _neg_expm1(y):
    series = -y * (1.0 + y * (0.5 + y * (1.0 / 6 + y * (1.0 / 24 + y * (1.0 / 120)))))
    return jnp.where(y > -0.1, series, 1.0 - jnp.exp(y))


def _dot(a, b):
    return jnp.dot(a.astype(BF16), b.astype(BF16), preferred_element_type=F32)


def _dot_nt(a, b):
    return lax.dot_general(a.astype(BF16), b.astype(BF16), (((1,), (1,)), ((), ())), preferred_element_type=F32)


def _dot3(tri, v):
    hi = v.astype(BF16)
    r1 = v - hi.astype(F32)
    mid = r1.astype(BF16)
    lo = (r1 - mid.astype(F32)).astype(BF16)
    t = tri.astype(BF16)
    return (jnp.dot(t, hi, preferred_element_type=F32) + jnp.dot(t, mid, preferred_element_type=F32)
            + jnp.dot(t, lo, preferred_element_type=F32))


def _colsum(v):
    return jnp.sum(v, axis=0, keepdims=True)


def _rows(n, cols=128):
    return lax.broadcasted_iota(jnp.int32, (n, cols), 0)


def _matmul(a, b, *, nt=False, ta=False, out_dtype=F32, res=None, gate=None, gu=None, name="matmul"):
    M, K = a.shape[::-1] if ta else a.shape
    pair = 2 * FFN_CHUNK
    if gu == "b":
        N = b.shape[1] if nt else 2 * b.shape[2]
    else:
        N = b.shape[0] if nt else b.shape[1]
    tm = _div_tile(M, TILES["mm_m"] * (1 if gu is None else 2), 8)
    tn = _div_tile(N, TILES["mm_n"], 128)
    tk = _div_tile(K, TILES["mm_k"], 128)
    if gu == "b" and nt:
        tk = pair
    elif gu == "b":
        tn = pair
    elif gu == "out":
        tn = FFN_CHUNK
    nk = K // tk
    epi = res is not None
    n_b = 2 if gu == "b" else 1

    def body(*refs):
        a_ref, b_refs, rest = refs[0], refs[1:1 + n_b], refs[1 + n_b:]
        if epi:
            res_ref, gate_ref, o_ref, x_ref = rest[:4]
        else:
            o_ref = rest[0]
        if gu == "b" and nt:
            part = (_dot_nt(a_ref[:, :FFN_CHUNK], b_refs[0][...]) + _dot_nt(a_ref[:, FFN_CHUNK:], b_refs[1][...]))
        elif gu == "b":
            part = jnp.concatenate([_dot(a_ref[...], b_refs[0][...]), _dot(a_ref[...], b_refs[1][...])], axis=1)
        else:
            if ta:
                part = lax.dot_general(a_ref[...].astype(BF16), b_refs[0][...].astype(BF16), (((0,), (0,)), ((), ())),
                                       preferred_element_type=F32)
            else:
                part = _dot_nt(a_ref[...], b_refs[0][...]) if nt else _dot(a_ref[...], b_refs[0][...])

        def finish(acc):
            o_ref[...] = acc.astype(o_ref.dtype)
            if epi:
                x_ref[...] = res_ref[...] + gate_ref[...] * acc

        if nk == 1:
            finish(part)
        else:
            acc_ref = refs[-1]
            k = pl.program_id(2)

            @pl.when(k == 0)
            def _():
                acc_ref[...] = part

            @pl.when(k > 0)
            def _():
                acc_ref[...] += part

            @pl.when(k == nk - 1)
            def _():
                finish(acc_ref[...])

    a_spec = pl.BlockSpec((tk, tm), lambda i, j, k: (k, i)) if ta else pl.BlockSpec((tm, tk), lambda i, j, k: (i, k))
    o_spec = pl.BlockSpec((tm, tn), lambda i, j, k: (i, j))
    out_shape = jax.ShapeDtypeStruct((M, N), out_dtype)
    if gu == "b" and nt:
        b_specs = [pl.BlockSpec((None, tn, FFN_CHUNK), lambda i, j, k, p=p: (p, j, k)) for p in range(2)]
    elif gu == "b":
        b_specs = [pl.BlockSpec((None, tk, FFN_CHUNK), lambda i, j, k, p=p: (p, k, j)) for p in range(2)]
    elif nt:
        b_specs = [pl.BlockSpec((tn, tk), lambda i, j, k: (j, k))]
    else:
        b_specs = [pl.BlockSpec((tk, tn), lambda i, j, k: (k, j))]
    if gu == "out":
        o_spec = pl.BlockSpec((None, tm, tn), lambda i, j, k: (j % 2, i, j // 2))
        out_shape = jax.ShapeDtypeStruct((2, M, N // 2), out_dtype)
    in_specs, args = [a_spec] + b_specs, [a] + [b] * n_b
    out_specs = o_spec
    if epi:
        in_specs += [o_spec, pl.BlockSpec((1, tn), lambda i, j, k: (0, j))]
        args += [res, gate]
        out_specs = (o_spec, o_spec)
        out_shape = (out_shape, jax.ShapeDtypeStruct((M, N), F32))
    return pl.pallas_call(
        body, name=name, grid=(M // tm, N // tn, nk), in_specs=in_specs, out_specs=out_specs, out_shape=out_shape,
        scratch_shapes=[pltpu.VMEM((tm, tn), F32)] if nk > 1 else [],
        compiler_params=_params(("parallel", "parallel", "arbitrary")),
    )(*args)


def _norm_mod(x, g, sc, sh):
    S, D = x.shape
    tr = _div_tile(S, TILES["row"], 8)

    def body(x_ref, g_ref, sc_ref, sh_ref, h_ref):
        xf = x_ref[...]
        r = lax.rsqrt(jnp.mean(xf * xf, axis=-1, keepdims=True) + EPS)
        h_ref[...] = (((xf * r) * g_ref[...]) * (1.0 + sc_ref[...]) + sh_ref[...]).astype(h_ref.dtype)

    row = pl.BlockSpec((tr, D), lambda i: (i, 0))
    vec = pl.BlockSpec((1, D), lambda i: (0, 0))
    return pl.pallas_call(
        body, name="norm_mod", grid=(S // tr,), in_specs=[row, vec, vec, vec], out_specs=row,
        out_shape=jax.ShapeDtypeStruct((S, D), BF16), compiler_params=_params(("parallel",)),
    )(x, g, sc, sh)


def _norm_mod_bwd(x, dh, dres, g, sc):
    S, D = x.shape
    tr = _div_tile(S, TILES["ew"], 8)

    def body(x_ref, dh_ref, dres_ref, g_ref, sc_ref, dx_ref, dg_ref, dsc_ref, dsh_ref):
        xf, dh_ = x_ref[...], dh_ref[...]
        r = lax.rsqrt(jnp.mean(xf * xf, axis=-1, keepdims=True) + EPS)
        xhat = xf * r
        dxhat = dh_ * (g_ref[...] * (1.0 + sc_ref[...]))
        dx_ref[...] = dres_ref[...] + r * (dxhat - xhat * jnp.mean(dxhat * xhat, axis=-1, keepdims=True))
        t = _colsum(dh_ * xhat)

        @pl.when(pl.program_id(0) == 0)
        def _():
            dg_ref[...] = jnp.zeros_like(dg_ref)
            dsc_ref[...] = jnp.zeros_like(dsc_ref)
            dsh_ref[...] = jnp.zeros_like(dsh_ref)

        dg_ref[...] += t * (1.0 + sc_ref[...])
        dsc_ref[...] += t * g_ref[...]
        dsh_ref[...] += _colsum(dh_)

    row = pl.BlockSpec((tr, D), lambda i: (i, 0))
    vec = pl.BlockSpec((1, D), lambda i: (0, 0))
    vshape = jax.ShapeDtypeStruct((1, D), F32)
    return pl.pallas_call(
        body, name="norm_mod_bwd", grid=(S // tr,), in_specs=[row, row, row, vec, vec], out_specs=(row, vec, vec, vec),
        out_shape=(jax.ShapeDtypeStruct((S, D), F32), vshape, vshape, vshape), compiler_params=_params(("arbitrary",)),
    )(x, dh, dres, g, sc)


def _gate_bwd(dx, m, gt):
    S, D = dx.shape
    tr = _div_tile(S, TILES["row"], 8)

    def body(dx_ref, m_ref, gt_ref, dm_ref, dgt_ref):
        d = dx_ref[...]
        dm_ref[...] = (d * gt_ref[...]).astype(dm_ref.dtype)

        @pl.when(pl.program_id(0) == 0)
        def _():
            dgt_ref[...] = jnp.zeros_like(dgt_ref)

        dgt_ref[...] += _colsum(d * m_ref[...])

    row = pl.BlockSpec((tr, D), lambda i: (i, 0))
    vec = pl.BlockSpec((1, D), lambda i: (0, 0))
    return pl.pallas_call(
        body, name="gate_bwd", grid=(S // tr,), in_specs=[row, row, vec], out_specs=(row, vec),
        out_shape=(jax.ShapeDtypeStruct((S, D), BF16), jax.ShapeDtypeStruct((1, D), F32)),
        compiler_params=_params(("arbitrary",)),
    )(dx, m, gt)


def _loss_head(x, target, g):
    S, D = x.shape
    tr = _div_tile(S, TILES["ew"], 8)

    def body(x_ref, t_ref, g_ref, dx_ref, dg_ref, loss_ref):
        xf = x_ref[...]
        r = lax.rsqrt(jnp.mean(xf * xf, axis=-1, keepdims=True) + EPS)
        xhat = xf * r
        err = xhat * g_ref[...] - t_ref[...]
        dy = err * (1.0 / D)
        dxhat = dy * g_ref[...]
        dx_ref[...] = r * (dxhat - xhat * jnp.mean(dxhat * xhat, axis=-1, keepdims=True))

        @pl.when(pl.program_id(0) == 0)
        def _():
            dg_ref[...] = jnp.zeros_like(dg_ref)
            loss_ref[...] = jnp.zeros_like(loss_ref)

        dg_ref[...] += _colsum(dy * xhat)
        loss_ref[...] += 0.5 * jnp.sum(jnp.mean(err * err, axis=-1, keepdims=True))

    row = pl.BlockSpec((tr, D), lambda i: (i, 0))
    vec = pl.BlockSpec((1, D), lambda i: (0, 0))
    one = pl.BlockSpec((1, 128), lambda i: (0, 0))
    return pl.pallas_call(
        body, name="loss_head", grid=(S // tr,), in_specs=[row, row, vec], out_specs=(row, vec, one),
        out_shape=(jax.ShapeDtypeStruct((S, D), F32), jax.ShapeDtypeStruct((1, D), F32),
                   jax.ShapeDtypeStruct((1, 128), F32)),
        compiler_params=_params(("arbitrary",)),
    )(x, target, g)


POOL_HALO = 16


def _pool_delta(ext, u, first_pos, tr):
    pos = (first_pos + _rows(tr) + 1).astype(F32)
    outs = []
    for gi, win in enumerate(POOL_WINDOWS):
        s = ext[:, gi * 128:(gi + 1) * 128]
        d = 1
        while d < win:
            s = s + pltpu.roll(s, d, 0)
            d *= 2
        outs.append(s[POOL_HALO:] / jnp.minimum(pos, float(win)) - u[:, gi * 128:(gi + 1) * 128])
    return outs


def _pool_fwd(z, w, scale):
    S = z.shape[0]
    tr = _div_tile(S, TILES["row"], POOL_HALO)
    hb = tr // POOL_HALO

    def body(z_ref, halo_ref, w_ref, sc_ref, y_ref):
        i = pl.program_id(0)
        u = z_ref[...]
        halo = jnp.where(i > 0, halo_ref[...], 0.0)
        ds_ = _pool_delta(jnp.concatenate([halo, u], axis=0), u, i * tr, tr)
        for gi in range(4):
            y_ref[:, gi * 128:(gi + 1) * 128] = _dot(ds_[gi], w_ref[gi]) * sc_ref[:, gi * 128:(gi + 1) * 128]

    return pl.pallas_call(
        body, name="pool_fwd", grid=(S // tr,),
        in_specs=[pl.BlockSpec((tr, POOL_WIDTH), lambda i: (i, 0)),
                  pl.BlockSpec((POOL_HALO, POOL_WIDTH), lambda i: (jnp.maximum(i * hb - 1, 0), 0)),
                  pl.BlockSpec((4, 128, 128), lambda i: (0, 0, 0)), pl.BlockSpec((1, POOL_WIDTH), lambda i: (0, 0))],
        out_specs=pl.BlockSpec((tr, POOL_WIDTH), lambda i: (i, 0)),
        out_shape=jax.ShapeDtypeStruct((S, POOL_WIDTH), F32), compiler_params=_params(("parallel",)),
    )(z, z, w, scale)


def _pool_bwd(z, dy, w, scale):
    S = z.shape[0]
    tr = _div_tile(S, TILES["row"], POOL_HALO)
    hb = tr // POOL_HALO
    nt = S // tr

    def body(z_ref, halo_ref, dy_ref, dyn_ref, w_ref, sc_ref, dz_ref, dw_ref, dsc_ref):
        i = pl.program_id(0)
        u = z_ref[...]
        halo = jnp.where(i > 0, halo_ref[...], 0.0)
        ds_ = _pool_delta(jnp.concatenate([halo, u], axis=0), u, i * tr, tr)
        dy_ext = jnp.concatenate([dy_ref[...], jnp.where(i < nt - 1, dyn_ref[...], 0.0)], axis=0)
        pos = (i * tr + _rows(tr + POOL_HALO) + 1).astype(F32)

        @pl.when(i == 0)
        def _():
            dw_ref[...] = jnp.zeros_like(dw_ref)
            dsc_ref[...] = jnp.zeros_like(dsc_ref)

        for gi, win in enumerate(POOL_WINDOWS):
            cols = slice(gi * 128, (gi + 1) * 128)
            dyg = dy_ext[:, cols]
            dys = dyg * sc_ref[:, cols]
            dsc_ref[:, cols] += _colsum(dyg[:tr] * _dot(ds_[gi], w_ref[gi]))
            dw_ref[gi] += _dot(ds_[gi].T, dys[:tr])
            dd = _dot_nt(dys, w_ref[gi])
            e = dd / jnp.minimum(pos, float(win))
            d = 1
            while d < win:
                e = e + pltpu.roll(e, tr + POOL_HALO - d, 0)
                d *= 2
            dz_ref[:, cols] = (e[:tr] - dd[:tr]).astype(dz_ref.dtype)

    return pl.pallas_call(
        body, name="pool_bwd", grid=(nt,),
        in_specs=[pl.BlockSpec((tr, POOL_WIDTH), lambda i: (i, 0)),
                  pl.BlockSpec((POOL_HALO, POOL_WIDTH), lambda i: (jnp.maximum(i * hb - 1, 0), 0)),
                  pl.BlockSpec((tr, POOL_WIDTH), lambda i: (i, 0)),
                  pl.BlockSpec((POOL_HALO, POOL_WIDTH), lambda i: (jnp.minimum((i + 1) * hb, nt * hb - 1), 0)),
                  pl.BlockSpec((4, 128, 128), lambda i: (0, 0, 0)), pl.BlockSpec((1, POOL_WIDTH), lambda i: (0, 0))],
        out_specs=(pl.BlockSpec((tr, POOL_WIDTH), lambda i: (i, 0)), pl.BlockSpec((4, 128, 128), lambda i: (0, 0, 0)),
                   pl.BlockSpec((1, POOL_WIDTH), lambda i: (0, 0))),
        out_shape=(jax.ShapeDtypeStruct((S, POOL_WIDTH), BF16), jax.ShapeDtypeStruct((4, 128, 128), F32),
                   jax.ShapeDtypeStruct((1, POOL_WIDTH), F32)),
        compiler_params=_params(("arbitrary",)),
    )(z, z, dy, dy, w, scale)


def _log_sigmoid(x):
    return jnp.minimum(x, 0.0) - jnp.log(1.0 + jnp.exp(-jnp.abs(x)))


def _forget_cumsum(z, b_f):
    S = z.shape[0]
    tr = _div_tile(S, TILES["cum"], 8)
    zf_block = ZF // 128

    def body(z_ref, b_ref, f_ref, carry):
        @pl.when(pl.program_id(0) == 0)
        def _():
            carry[...] = jnp.zeros_like(carry)

        lf = _log_sigmoid(z_ref[...] + b_ref[...])
        tri = lax.broadcasted_iota(jnp.int32, (tr, tr), 1) <= lax.broadcasted_iota(jnp.int32, (tr, tr), 0)
        f_ref[...] = _dot3(tri, lf) + carry[...]
        carry[...] += _colsum(lf)

    return pl.pallas_call(
        body, name="forget_cumsum", grid=(S // tr,),
        in_specs=[pl.BlockSpec((tr, 128), lambda i: (i, zf_block)), pl.BlockSpec((1, 128), lambda i: (0, 0))],
        out_specs=pl.BlockSpec((tr, 128), lambda i: (i, 0)), out_shape=jax.ShapeDtypeStruct((S, 128), F32),
        scratch_shapes=[pltpu.VMEM((1, 128), F32)], compiler_params=_params(("arbitrary",)),
    )(z, b_f)


def _forget_cumsum_bwd(z, b_f, dF):
    S = z.shape[0]
    tr = _div_tile(S, TILES["cum"], 8)
    nt = S // tr
    zf_block = ZF // 128

    def body(z_ref, b_ref, df_ref, dz_ref, db_ref, carry):
        @pl.when(pl.program_id(0) == 0)
        def _():
            carry[...] = jnp.zeros_like(carry)
            db_ref[...] = jnp.zeros_like(db_ref)

        dF_ = df_ref[...]
        tri = lax.broadcasted_iota(jnp.int32, (tr, tr), 1) >= lax.broadcasted_iota(jnp.int32, (tr, tr), 0)
        dlf = _dot3(tri, dF_) + carry[...]
        carry[...] += _colsum(dF_)
        lane = lax.broadcasted_iota(jnp.int32, (tr, 128), 1)
        dzf = jnp.where(lane < N_HEADS, dlf * _sigmoid(-(z_ref[...] + b_ref[...])), 0.0)
        dz_ref[...] = dzf.astype(dz_ref.dtype)
        db_ref[...] += _colsum(dzf)

    return pl.pallas_call(
        body, name="forget_cumsum_bwd", grid=(nt,),
        in_specs=[pl.BlockSpec((tr, 128), lambda i: (nt - 1 - i, zf_block)), pl.BlockSpec((1, 128), lambda i: (0, 0)),
                  pl.BlockSpec((tr, 128), lambda i: (nt - 1 - i, 0))],
        out_specs=(pl.BlockSpec((tr, 128), lambda i: (nt - 1 - i, 0)), pl.BlockSpec((1, 128), lambda i: (0, 0))),
        out_shape=(jax.ShapeDtypeStruct((S, 128), BF16), jax.ShapeDtypeStruct((1, 128), F32)),
        scratch_shapes=[pltpu.VMEM((1, 128), F32)], compiler_params=_params(("arbitrary",)),
    )(z, b_f, dF)


NEG = -1e30
ATTN_SCALE = HEAD_DIM ** -0.5


def _on_block_kind(qi, kj, fn):
    @pl.when(qi == kj)
    def _():
        fn(True)

    @pl.when(qi != kj)
    def _():
        fn(False)


FIRST, LAST, HEAD_FIRST, HEAD_LAST, KEY_ZERO = 1, 2, 4, 8, 16


def _tri_schedule(n, by_key=False):
    outer, inner, flags = [], [], []
    for a in range(n):
        partners = list(range(a, n)) if by_key else list(range(a + 1))
        for idx, b in enumerate(partners):
            f = FIRST if idx == 0 else 0
            f |= LAST if idx == len(partners) - 1 else 0
            f |= KEY_ZERO if (a if by_key else b) == 0 else 0
            outer.append(a)
            inner.append(b)
            flags.append(f)
    flags[0] |= HEAD_FIRST
    flags[-1] |= HEAD_LAST
    return [jnp.asarray(np.array(v, np.int32)) for v in (outer, inner, flags)]


def _flash_call(body, name, sched, in_specs, out_specs, out_shape, scratch, comm_ins=0, comm_shapes=(), comm_sems=()):
    grid_spec = pltpu.PrefetchScalarGridSpec(
        num_scalar_prefetch=3, grid=(N_HEADS, int(sched[0].shape[0])), in_specs=list(in_specs) + [ANY] * comm_ins,
        out_specs=tuple(out_specs) + (ANY,) * len(comm_shapes), scratch_shapes=list(scratch) + list(comm_sems))
    heads = "arbitrary" if comm_sems else "parallel"
    return pl.pallas_call(body, name=name, grid_spec=grid_spec, out_shape=tuple(out_shape) + tuple(comm_shapes),
                          compiler_params=_params((heads, "arbitrary")))


def _scores_t(q, k, fq_row, fk_col, diagonal, floor):
    st = _dot_nt(k, q) * ATTN_SCALE + fq_row - fk_col
    if not diagonal:
        return st
    t = st.shape[0]
    return jnp.where(lax.broadcasted_iota(jnp.int32, (t, t), 0) <= lax.broadcasted_iota(jnp.int32, (t, t), 1), st, floor)


def _flash_fwd(z, v_t, f_col, f_row, next_shards=None):
    S = z.shape[0]
    t = _div_tile(S, TILES["attn"], 128)
    sched = _tri_schedule(S // t)
    n_steps = int(sched[0].shape[0])
    qb, kb = 0, N_HEADS
    n_comm = N_SHARDS if next_shards is not None else 0

    def body(qt, kt, ft, q_ref, k_ref, vt_ref, fq_ref, fk_ref, *rest):
        shard_refs, rest = rest[:n_comm], rest[n_comm:]
        (o_ref, lse_ref), rest = rest[:2], rest[2:]
        gathered, rest = (rest[:4], rest[4:]) if n_comm else ((), rest)
        (m_sc, l_sc, acc_sc), sems = rest[:3], rest[3:]
        head, step = pl.program_id(0), pl.program_id(1)
        qi, kj, fl = qt[step], kt[step], ft[step]

        if n_comm:
            @pl.when((head == 0) & (step == 0))
            def _():
                _GatherPlan(shard_refs, gathered, sems).start()

            @pl.when((head == N_HEADS - 2) & (step == 0))
            def _():
                _GatherPlan(shard_refs, gathered, sems).forward()

            @pl.when((head == N_HEADS - 1) & (step == n_steps - 1))
            def _():
                _GatherPlan(shard_refs, gathered, sems).finish()

        @pl.when((fl & FIRST) != 0)
        def _():
            m_sc[...] = jnp.full_like(m_sc, NEG)
            l_sc[...] = jnp.zeros_like(l_sc)
            acc_sc[...] = jnp.zeros_like(acc_sc)

        def update(diagonal):
            st = _scores_t(q_ref[...], k_ref[...], fq_ref[...], fk_ref[...], diagonal, NEG)
            m_new = jnp.maximum(m_sc[...], jnp.max(st, axis=0, keepdims=True))
            alpha = jnp.exp(m_sc[...] - m_new)
            pt = jnp.exp(st - m_new)
            l_sc[...] = alpha * l_sc[...] + jnp.sum(pt, axis=0, keepdims=True)
            acc_sc[...] = alpha * acc_sc[...] + _dot(vt_ref[...], pt)
            m_sc[...] = m_new

        _on_block_kind(qi, kj, update)

        @pl.when((fl & LAST) != 0)
        def _():
            o_ref[...] = acc_sc[...] / l_sc[...]
            lse_ref[...] = m_sc[...] + jnp.log(l_sc[...])

    row = pl.BlockSpec((None, 1, t), lambda h, s, qt, kt, ft: (h, 0, qt[s]))
    return _flash_call(
        body, "flash_fwd", sched,
        [pl.BlockSpec((t, 128), lambda h, s, qt, kt, ft: (qt[s], qb + h)),
         pl.BlockSpec((t, 128), lambda h, s, qt, kt, ft: (kt[s], kb + h)),
         pl.BlockSpec((128, t), lambda h, s, qt, kt, ft: (h, kt[s])), row,
         pl.BlockSpec((None, t, 1), lambda h, s, qt, kt, ft: (h, kt[s], 0))],
        (pl.BlockSpec((128, t), lambda h, s, qt, kt, ft: (h, qt[s])), row),
        (jax.ShapeDtypeStruct((ATTN_WIDTH, S), F32), jax.ShapeDtypeStruct((N_HEADS, 1, S), F32)),
        [pltpu.VMEM((1, t), F32), pltpu.VMEM((1, t), F32), pltpu.VMEM((128, t), F32)],
        comm_ins=n_comm, comm_shapes=_gather_shapes(next_shards) if n_comm else (),
        comm_sems=GATHER_SEMS if n_comm else (),
    )(*sched, z, z, v_t, f_row, f_col, *(next_shards or ()))


def _flash_bwd(z, k_t, dy, do_t, o_t, lse_row, f_col, f_row, pairs=None):
    S = z.shape[0]
    t = _div_tile(S, TILES["attn"], 128)
    n = S // t
    sched = _tri_schedule(n, by_key=True)
    n_steps = int(sched[0].shape[0])
    qb, kb, vb = 0, N_HEADS, 2 * N_HEADS
    dob = 0
    n_comm = N_PAIRS if pairs is not None else 0

    def body(kt, qt, ft, q_ref, k_ref, v_ref, kt_ref, do_ref, dot_ref, ot_ref, lse_ref, fq_ref, fk_ref, *rest):
        pair_refs, rest = rest[:n_comm], rest[n_comm:]
        (dq_ref, dk_ref, dv_ref, dfk_ref, dfq_ref), rest = rest[:5], rest[5:]
        arrived, rest = rest[:n_comm], rest[n_comm:]
        (dk_sc, dv_sc, dfk_sc, dq_sc, dfq_sc, delta_sc), sems = rest[:6], rest[6:]
        head, step = pl.program_id(0), pl.program_id(1)
        kj, qi, fl = kt[step], qt[step], ft[step]

        if n_comm:
            @pl.when((head == 0) & (step == 0))
            def _():
                for cp in _scatter_copies(pair_refs, arrived, sems):
                    cp.start()

            @pl.when((head == N_HEADS - 1) & (step == n_steps - 1))
            def _():
                for cp in _scatter_copies(pair_refs, arrived, sems):
                    cp.wait()

        @pl.when((fl & HEAD_FIRST) != 0)
        def _():
            dq_sc[...] = jnp.zeros_like(dq_sc)
            dfq_sc[...] = jnp.zeros_like(dfq_sc)

        @pl.when((fl & FIRST) != 0)
        def _():
            dk_sc[...] = jnp.zeros_like(dk_sc)
            dv_sc[...] = jnp.zeros_like(dv_sc)
            dfk_sc[...] = jnp.zeros_like(dfk_sc)

        @pl.when((fl & KEY_ZERO) != 0)
        def _():
            delta_sc[qi] = jnp.sum(dot_ref[...] * ot_ref[...], axis=0, keepdims=True)

        def update(diagonal):
            pt = jnp.exp(_scores_t(q_ref[...], k_ref[...], fq_ref[...], fk_ref[...], diagonal, NEG) - lse_ref[...])
            dv_sc[...] += _dot(pt, do_ref[...])
            dst = pt * (_dot_nt(v_ref[...], do_ref[...]) - delta_sc[qi])
            dk_sc[...] += _dot(dst, q_ref[...])
            dfk_sc[...] += jnp.sum(dst, axis=1, keepdims=True)
            dfq_sc[qi] = dfq_sc[qi] + jnp.sum(dst, axis=0, keepdims=True)
            dq_sc[qi] = dq_sc[qi] + _dot(kt_ref[...], dst)

        _on_block_kind(qi, kj, update)

        @pl.when((fl & LAST) != 0)
        def _():
            dk_ref[...] = (dk_sc[...] * ATTN_SCALE).astype(dk_ref.dtype)
            dv_ref[...] = dv_sc[...].astype(dv_ref.dtype)
            dfk_ref[...] = -dfk_sc[...]

        @pl.when((fl & HEAD_LAST) != 0)
        def _():
            for i in range(n):
                dq_ref[:, i * t:(i + 1) * t] = (dq_sc[i] * ATTN_SCALE).astype(dq_ref.dtype)
                dfq_ref[:, i * t:(i + 1) * t] = dfq_sc[i]

    def qs(block):
        return pl.BlockSpec((t, 128), lambda h, s, kt, qt, ft: (qt[s], block + h))

    def kv(block):
        return pl.BlockSpec((t, 128), lambda h, s, kt, qt, ft: (kt[s], block + h))

    first_sweep = pl.BlockSpec((128, t), lambda h, s, kt, qt, ft: (h, jnp.where(kt[s] == 0, qt[s], n - 1)))
    qrow = pl.BlockSpec((None, 1, t), lambda h, s, kt, qt, ft: (h, 0, qt[s]))
    kcol = pl.BlockSpec((None, t, 1), lambda h, s, kt, qt, ft: (h, kt[s], 0))
    out = pl.BlockSpec((t, 128), lambda h, s, kt, qt, ft: (kt[s], h))
    return _flash_call(
        body, "flash_bwd", sched,
        [qs(qb), kv(kb), kv(vb), pl.BlockSpec((128, t), lambda h, s, kt, qt, ft: (h, kt[s])), qs(dob), first_sweep,
         first_sweep, qrow, qrow, kcol],
        (pl.BlockSpec((128, S), lambda h, s, kt, qt, ft: (h, 0)), out, out, kcol,
         pl.BlockSpec((None, 1, S), lambda h, s, kt, qt, ft: (h, 0, 0))),
        (jax.ShapeDtypeStruct((ATTN_WIDTH, S), BF16), jax.ShapeDtypeStruct((S, ATTN_WIDTH), BF16),
         jax.ShapeDtypeStruct((S, ATTN_WIDTH), BF16), jax.ShapeDtypeStruct((N_HEADS, S, 1), F32),
         jax.ShapeDtypeStruct((N_HEADS, 1, S), F32)),
        [pltpu.VMEM((t, 128), F32), pltpu.VMEM((t, 128), F32), pltpu.VMEM((t, 1), F32), pltpu.VMEM((n, 128, t), F32),
         pltpu.VMEM((n, 1, t), F32), pltpu.VMEM((n, 1, t), F32)],
        comm_ins=n_comm, comm_shapes=_scatter_shapes(pairs) if n_comm else (), comm_sems=SCATTER_SEMS if n_comm else (),
    )(*sched, z, z, z, k_t, dy, do_t, o_t, lse_row, f_row, f_col, *(pairs or ()))


LRU_HALO = 8


def _lru_gates(ext, cw_ref, cb_ref, wa_ref, ba_ref, wi_ref, bi_ref, lam_ref, tr):
    taps = [pltpu.roll(ext, 3 - k, 0)[LRU_HALO:] if k < 3 else ext[LRU_HALO:] for k in range(4)]
    xc = cb_ref[...] + taps[0] * cw_ref[0:1, :]
    for k in range(1, 4):
        xc = xc + taps[k] * cw_ref[k:k + 1, :]
    ga = jnp.concatenate([_dot(xc[:, g * 128:(g + 1) * 128], wa_ref[g]) for g in range(4)], axis=1) + ba_ref[...]
    gi = jnp.concatenate([_dot(xc[:, g * 128:(g + 1) * 128], wi_ref[g]) for g in range(4)], axis=1) + bi_ref[...]
    r, ig = _sigmoid(ga), _sigmoid(gi)
    nl = -lam_ref[...]
    sp = jnp.maximum(nl, 0.0) + jnp.log(1.0 + jnp.exp(-jnp.abs(nl)))
    la = -LRU_C * r * sp
    a = jnp.exp(la)
    mult = jnp.sqrt(_neg_expm1(2.0 * la))
    return xc, r, ig, sp, a, mult, taps


def _lru_specs(tr, nt, rev):
    hb = tr // LRU_HALO
    ti = (lambda i: nt - 1 - i) if rev else (lambda i: i)
    zx_b, zy_b = ZX // LRU_WIDTH, ZY // LRU_WIDTH
    cur = lambda b: pl.BlockSpec((tr, LRU_WIDTH), lambda i: (ti(i), b))
    prev = lambda b: pl.BlockSpec((LRU_HALO, LRU_WIDTH), lambda i: (jnp.maximum(ti(i) * hb - 1, 0), b))
    vec = pl.BlockSpec((1, LRU_WIDTH), lambda i: (0, 0))
    cw = pl.BlockSpec((4, LRU_WIDTH), lambda i: (0, 0))
    blk = pl.BlockSpec((4, 128, 128), lambda i: (0, 0, 0))
    return ti, cur, prev, vec, cw, blk, zx_b, zy_b


def _lru_fwd(z, cw, cb, wa, ba, wi, bi, lam):
    S = z.shape[0]
    tr = _div_tile(S, TILES["lru"], 8)
    nt = S // tr
    ti, cur, prev, vec, cwspec, blk, zx_b, zy_b = _lru_specs(tr, nt, False)

    def body(zx_ref, halo_ref, zy_ref, cw_ref, cb_ref, wa_ref, ba_ref, wi_ref, bi_ref, lam_ref, y_ref, h_ref, carry):
        i = pl.program_id(0)

        @pl.when(i == 0)
        def _():
            carry[...] = jnp.zeros_like(carry)

        ext = jnp.concatenate([jnp.where(i > 0, halo_ref[...], 0.0), zx_ref[...]], axis=0)
        xc, r, ig, sp, a, mult, _ = _lru_gates(ext, cw_ref, cb_ref, wa_ref, ba_ref, wi_ref, bi_ref, lam_ref, tr)
        A, B = a, mult * (ig * xc)
        row = _rows(tr, LRU_WIDTH)
        d = 1
        while d < tr:
            a_sh = jnp.where(row >= d, pltpu.roll(A, d, 0), 1.0)
            b_sh = jnp.where(row >= d, pltpu.roll(B, d, 0), 0.0)
            B = A * b_sh + B
            A = A * a_sh
            d *= 2
        h = B + A * carry[...]
        h_ref[...] = h
        carry[...] = h_ref[pl.ds(tr - 1, 1), :]
        y_ref[...] = h * _gelu_parts(zy_ref[...])[0]

    out = pl.BlockSpec((tr, LRU_WIDTH), lambda i: (i, 0))
    shape = jax.ShapeDtypeStruct((S, LRU_WIDTH), F32)
    return pl.pallas_call(
        body, name="lru_fwd", grid=(nt,),
        in_specs=[cur(zx_b), prev(zx_b), cur(zy_b), cwspec, vec, blk, vec, blk, vec, vec],
        out_specs=(out, out), out_shape=(shape, shape), scratch_shapes=[pltpu.VMEM((1, LRU_WIDTH), F32)],
        compiler_params=_params(("arbitrary",)),
    )(z, z, z, cw, cb, wa, ba, wi, bi, lam)


def _lru_bwd(z, dy, hs, cw, cb, wa, ba, wi, bi, lam):
    S = z.shape[0]
    tr = _div_tile(S, TILES["lru"], 8)
    nt = S // tr
    ti, cur, prev, vec, cwspec, blk, zx_b, zy_b = _lru_specs(tr, nt, True)
    dy_b = (POOL_WIDTH + ATTN_WIDTH) // LRU_WIDTH

    def body(zx_ref, halo_ref, zy_ref, dy_ref, h_ref, hprev_ref, cw_ref, cb_ref, wa_ref, ba_ref, wi_ref, bi_ref, lam_ref,
             dzx_ref, dzy_ref, dcw_ref, dcb_ref, dwa_ref, dba_ref, dwi_ref, dbi_ref, dlam_ref, gcarry, dxc_next, tmp):
        i = pl.program_id(0)
        t_idx = nt - 1 - i

        @pl.when(i == 0)
        def _():
            gcarry[...] = jnp.zeros_like(gcarry)
            dxc_next[...] = jnp.zeros_like(dxc_next)
            for ref in (dcw_ref, dcb_ref, dwa_ref, dba_ref, dwi_ref, dbi_ref, dlam_ref):
                ref[...] = jnp.zeros_like(ref)

        ext = jnp.concatenate([jnp.where(t_idx > 0, halo_ref[...], 0.0), zx_ref[...]], axis=0)
        xc, r, ig, sp, a, mult, taps = _lru_gates(ext, cw_ref, cb_ref, wa_ref, ba_ref, wi_ref, bi_ref, lam_ref, tr)
        h = h_ref[...]
        gel, dgel = _gelu_parts(zy_ref[...])
        dy_ = dy_ref[...]
        dzy_ref[...] = (dy_ * h * dgel).astype(dzy_ref.dtype)
        row = _rows(tr, LRU_WIDTH)
        B = dy_ * gel + jnp.where(row == tr - 1, gcarry[...], 0.0)
        A = jnp.where(row < tr - 1, pltpu.roll(a, tr - 1, 0), 0.0)
        d = 1
        while d < tr:
            keep = row < tr - d
            b_sh = jnp.where(keep, pltpu.roll(B, tr - d, 0), 0.0)
            a_sh = jnp.where(keep, pltpu.roll(A, tr - d, 0), 0.0)
            B = B + A * b_sh
            A = A * a_sh
            d *= 2
        g = B
        tmp[...] = a * g
        gcarry[...] = tmp[pl.ds(0, 1), :]
        h_ext = jnp.concatenate([jnp.where(t_idx > 0, hprev_ref[...], 0.0), h], axis=0)
        hprev = pltpu.roll(h_ext, 1, 0)[LRU_HALO:]
        t1 = g * mult
        dig = t1 * xc
        dxc = t1 * ig
        dla = (g * hprev) * a - (g * (ig * xc)) * (a * a) / mult
        dr = dla * (-LRU_C * sp)
        dga = dr * r * (1.0 - r)
        dgi = dig * ig * (1.0 - ig)
        dlam_ref[...] += _colsum(dla * (-LRU_C * r)) * (-_sigmoid(-lam_ref[...]))
        dba_ref[...] += _colsum(dga)
        dbi_ref[...] += _colsum(dgi)
        parts = []
        for gidx in range(4):
            cols = slice(gidx * 128, (gidx + 1) * 128)
            xct = xc[:, cols].T
            dwa_ref[gidx] += _dot(xct, dga[:, cols])
            dwi_ref[gidx] += _dot(xct, dgi[:, cols])
            parts.append(_dot_nt(dga[:, cols], wa_ref[gidx]) + _dot_nt(dgi[:, cols], wi_ref[gidx]))
        dxc = dxc + jnp.concatenate(parts, axis=1)
        dcb_ref[...] += _colsum(dxc)
        for k in range(4):
            dcw_ref[k:k + 1, :] += _colsum(dxc * taps[k])
        ext_d = jnp.concatenate([dxc, dxc_next[...]], axis=0)
        dzx = dxc * cw_ref[3:4, :]
        for k in range(3):
            dzx = dzx + pltpu.roll(ext_d, tr + LRU_HALO - (3 - k), 0)[:tr] * cw_ref[k:k + 1, :]
        dzx_ref[...] = dzx.astype(dzx_ref.dtype)
        dxc_next[...] = dxc[:LRU_HALO]

    rev = pl.BlockSpec((tr, LRU_WIDTH), lambda i: (nt - 1 - i, 0))
    hb = tr // LRU_HALO
    hprev_spec = pl.BlockSpec((LRU_HALO, LRU_WIDTH), lambda i: (jnp.maximum((nt - 1 - i) * hb - 1, 0), 0))
    dy_spec = pl.BlockSpec((tr, LRU_WIDTH), lambda i: (nt - 1 - i, dy_b))
    vshape = jax.ShapeDtypeStruct((1, LRU_WIDTH), F32)
    bshape = jax.ShapeDtypeStruct((4, 128, 128), F32)
    return pl.pallas_call(
        body, name="lru_bwd", grid=(nt,),
        in_specs=[cur(zx_b), prev(zx_b), cur(zy_b), dy_spec, rev, hprev_spec, cwspec, vec, blk, vec, blk, vec, vec],
        out_specs=(rev, rev, cwspec, vec, blk, vec, blk, vec, vec),
        out_shape=(jax.ShapeDtypeStruct((S, LRU_WIDTH), BF16), jax.ShapeDtypeStruct((S, LRU_WIDTH), BF16),
                   jax.ShapeDtypeStruct((4, LRU_WIDTH), F32), vshape, bshape, vshape, bshape, vshape, vshape),
        scratch_shapes=[pltpu.VMEM((1, LRU_WIDTH), F32), pltpu.VMEM((LRU_HALO, LRU_WIDTH), F32),
                        pltpu.VMEM((tr, LRU_WIDTH), F32)],
        compiler_params=_params(("arbitrary",)),
    )(z, z, z, dy, hs, hs, cw, cb, wa, ba, wi, bi, lam)


FFN_HALO = 8


def _ffn_act(au, cw, cb):
    S, F2 = au.shape
    F = F2 // 2
    tc = FFN_CHUNK
    tr = _div_tile(S, TILES["row"], 8)
    hb = tr // FFN_HALO

    def body(au_ref, halo_ref, cw_ref, cb_ref, p_ref):
        i = pl.program_id(0)
        a_ = au_ref[:, :tc]
        ext = jnp.concatenate([jnp.where(i > 0, halo_ref[:, :tc], 0.0), a_], axis=0)
        gc = cb_ref[...] + a_ * cw_ref[2:3, :]
        for k in range(2):
            gc = gc + pltpu.roll(ext, 2 - k, 0)[FFN_HALO:] * cw_ref[k:k + 1, :]
        p_ref[...] = (gc * _sigmoid(gc) * au_ref[:, tc:]).astype(p_ref.dtype)

    return pl.pallas_call(
        body, name="ffn_act", grid=(S // tr, F // tc),
        in_specs=[pl.BlockSpec((tr, 2 * tc), lambda i, j: (i, j)),
                  pl.BlockSpec((FFN_HALO, 2 * tc), lambda i, j: (jnp.maximum(i * hb - 1, 0), j)),
                  pl.BlockSpec((3, tc), lambda i, j: (0, j)), pl.BlockSpec((1, tc), lambda i, j: (0, j))],
        out_specs=pl.BlockSpec((tr, tc), lambda i, j: (i, j)), out_shape=jax.ShapeDtypeStruct((S, F), BF16),
        compiler_params=_params(("parallel", "parallel")),
    )(au, au, cw, cb)


def _ffn_act_bwd(au, dp, cw, cb):
    S, F2 = au.shape
    F = F2 // 2
    tc = FFN_CHUNK
    tr = _div_tile(S, TILES["ew"], 8)
    hb = tr // FFN_HALO
    nt = S // tr
    H = FFN_HALO

    def body(au_ref, prev_ref, next_ref, dp_ref, dpn_ref, cw_ref, cb_ref, dau_ref, dcw_ref, dcb_ref):
        i = pl.program_id(1)
        last = i == nt - 1
        a_ext = jnp.concatenate([jnp.where(i > 0, prev_ref[:, :tc], 0.0), au_ref[:, :tc], next_ref[:, :tc]], axis=0)
        u_ext = jnp.concatenate([au_ref[:, tc:], next_ref[:, tc:]], axis=0)
        dp_ext = jnp.concatenate([dp_ref[...], jnp.where(last, 0.0, dpn_ref[...])], axis=0)
        taps = [pltpu.roll(a_ext, 2 - k, 0)[H:] if k < 2 else a_ext[H:] for k in range(3)]
        gc = cb_ref[...] + taps[0] * cw_ref[0:1, :] + taps[1] * cw_ref[1:2, :] + taps[2] * cw_ref[2:3, :]
        sig = _sigmoid(gc)
        dgc = dp_ext * u_ext * (sig * (1.0 + gc * (1.0 - sig)))
        da = dgc[:tr] * cw_ref[2:3, :]
        for k in range(2):
            da = da + pltpu.roll(dgc, tr + H - (2 - k), 0)[:tr] * cw_ref[k:k + 1, :]
        dau_ref[:, :tc] = da.astype(dau_ref.dtype)
        dau_ref[:, tc:] = (dp_ref[...] * (gc[:tr] * sig[:tr])).astype(dau_ref.dtype)

        @pl.when(i == 0)
        def _():
            dcw_ref[...] = jnp.zeros_like(dcw_ref)
            dcb_ref[...] = jnp.zeros_like(dcb_ref)

        dcb_ref[...] += _colsum(dgc[:tr])
        for k in range(3):
            dcw_ref[k:k + 1, :] += _colsum(dgc[:tr] * taps[k][:tr])

    return pl.pallas_call(
        body, name="ffn_act_bwd", grid=(F // tc, nt),
        in_specs=[pl.BlockSpec((tr, 2 * tc), lambda j, i: (i, j)),
                  pl.BlockSpec((H, 2 * tc), lambda j, i: (jnp.maximum(i * hb - 1, 0), j)),
                  pl.BlockSpec((H, 2 * tc), lambda j, i: (jnp.minimum((i + 1) * hb, nt * hb - 1), j)),
                  pl.BlockSpec((tr, tc), lambda j, i: (i, j)),
                  pl.BlockSpec((H, tc), lambda j, i: (jnp.minimum((i + 1) * hb, nt * hb - 1), j)),
                  pl.BlockSpec((3, tc), lambda j, i: (0, j)), pl.BlockSpec((1, tc), lambda j, i: (0, j))],
        out_specs=(pl.BlockSpec((tr, 2 * tc), lambda j, i: (i, j)), pl.BlockSpec((3, tc), lambda j, i: (0, j)),
                   pl.BlockSpec((1, tc), lambda j, i: (0, j))),
        out_shape=(jax.ShapeDtypeStruct((S, F2), BF16), jax.ShapeDtypeStruct((3, F), F32), jax.ShapeDtypeStruct((1, F), F32)),
        compiler_params=_params(("parallel", "arbitrary")),
    )(au, au, au, dp, dp, cw, cb)


def _row_tile(rows, cols):
    return _div_tile(rows, max(16, (2**18 // cols) // 16 * 16), 16)


def _sum_parts(parts, sel, rows, cols, name):
    tr = _row_tile(rows, cols)

    def body(sel_ref, *refs):
        acc = refs[0][...].astype(F32)
        for r in refs[1:-1]:
            acc = acc + r[...].astype(F32)
        refs[-1][...] = acc

    def spec(index):
        if isinstance(index, int):
            return pl.BlockSpec((None, tr, cols), lambda i, s: (index, i, 0))
        k, mul, off = index
        return pl.BlockSpec((None, tr, cols), lambda i, s: (s[k] * mul + off, i, 0))

    grid_spec = pltpu.PrefetchScalarGridSpec(
        num_scalar_prefetch=1, grid=(rows // tr,), in_specs=[spec(ix) for _, ix in parts],
        out_specs=pl.BlockSpec((tr, cols), lambda i, s: (i, 0)))
    return pl.pallas_call(
        body, name=name, grid_spec=grid_spec, out_shape=jax.ShapeDtypeStruct((rows, cols), F32),
        compiler_params=_params(("parallel",)),
    )(sel, *[a for a, _ in parts])


def _pair_sum(g, got, sel):
    n, rows, cols = got.shape
    tr = _row_tile(rows, cols)
    nb = rows // tr

    def body(sel_ref, a_ref, b_ref, o_ref):
        o_ref[...] = (a_ref[...] + b_ref[...]).astype(o_ref.dtype)

    grid_spec = pltpu.PrefetchScalarGridSpec(
        num_scalar_prefetch=1, grid=(n, nb),
        in_specs=[pl.BlockSpec((None, tr, cols), lambda q, i, s: (q, s[0] * nb + i, 0)),
                  pl.BlockSpec((None, tr, cols), lambda q, i, s: (q, i, 0))],
        out_specs=pl.BlockSpec((None, tr, cols), lambda q, i, s: (q, i, 0)))
    return pl.pallas_call(
        body, name="pair_sum", grid_spec=grid_spec, out_shape=jax.ShapeDtypeStruct((n, rows, cols), BF16),
        compiler_params=_params(("parallel", "parallel")),
    )(sel, g, got)


def _chip_sum_cols(pair, arrived, sel):
    n, rows, cg = arrived.shape[1:]
    tr = _row_tile(rows, cg)

    def body(sel_ref, p_ref, a0, a1, a2, o_ref):
        o_ref[...] = ((p_ref[...].astype(F32) + a0[...].astype(F32)) + a1[...].astype(F32)) + a2[...].astype(F32)

    def arr(j):
        return pl.BlockSpec((None, None, tr, cg), lambda k, i, s: (j, k, i, 0))

    grid_spec = pltpu.PrefetchScalarGridSpec(
        num_scalar_prefetch=1, grid=(n, rows // tr),
        in_specs=[pl.BlockSpec((None, tr, cg), lambda k, i, s: (k, i, s[1])), arr(0), arr(1), arr(2)],
        out_specs=pl.BlockSpec((None, tr, cg), lambda k, i, s: (k, i, 0)))
    return pl.pallas_call(
        body, name="chip_sum_cols", grid_spec=grid_spec, out_shape=jax.ShapeDtypeStruct((n, rows, cg), F32),
        compiler_params=_params(("parallel", "parallel")),
    )(sel, pair, arrived, arrived, arrived)


def _adamw_math(w, g, m, v):
    m2 = ADAM_B1 * m + (1.0 - ADAM_B1) * g
    v2 = ADAM_B2 * v + (1.0 - ADAM_B2) * (g * g)
    m_hat = m2 / (1.0 - ADAM_B1 ** ADAM_STEP)
    v_hat = v2 / (1.0 - ADAM_B2 ** ADAM_STEP)
    return -ADAM_LR * (m_hat / (jnp.sqrt(v_hat) + ADAM_EPS) + ADAM_WD * w), m2, v2


def _adamw(w, g, m, v):
    R, C = w.shape
    tr = _div_tile(R, TILES["ew"], 8)

    def body(w_ref, g_ref, m_ref, v_ref, d_ref, m2_ref, v2_ref):
        d_ref[...], m2_ref[...], v2_ref[...] = _adamw_math(w_ref[...], g_ref[...], m_ref[...], v_ref[...])

    spec = pl.BlockSpec((tr, C), lambda i: (i, 0))
    shape = jax.ShapeDtypeStruct((R, C), F32)
    return pl.pallas_call(
        body, name="adamw", grid=(R // tr,), in_specs=[spec] * 4, out_specs=(spec,) * 3, out_shape=(shape,) * 3,
        compiler_params=_params(("parallel",)),
    )(w, g, m, v)


def _ada_grad_adamw(cact_t, dmod, w, m, v):
    L, D, N = w.shape
    tm, tn = _div_tile(D, 256, 8), _div_tile(N, 1024, 128)

    def body(c_ref, d_ref, w_ref, m_ref, v_ref, g_ref, dl_ref, m2_ref, v2_ref):
        g = c_ref[:, 0:1] * d_ref[0:1, :]
        for b in range(1, N_DEV):
            g = g + c_ref[:, b:b + 1] * d_ref[b:b + 1, :]
        g_ref[...] = g
        dl_ref[...], m2_ref[...], v2_ref[...] = _adamw_math(w_ref[...], g, m_ref[...], v_ref[...])

    big = pl.BlockSpec((None, tm, tn), lambda l, i, j: (l, i, j))
    shape = jax.ShapeDtypeStruct((L, D, N), F32)
    return pl.pallas_call(
        body, name="ada_grad_adamw", grid=(L, D // tm, N // tn),
        in_specs=[pl.BlockSpec((tm, N_DEV), lambda l, i, j: (i, 0)),
                  pl.BlockSpec((None, N_DEV, tn), lambda l, i, j: (l, 0, j)), big, big, big],
        out_specs=(big,) * 4, out_shape=(shape,) * 4, compiler_params=_params(("parallel", "parallel", "parallel")),
    )(cact_t, dmod, w, m, v)


def _silu_rows(c):
    def body(c_ref, o_ref):
        x = c_ref[...]
        o_ref[...] = x * _sigmoid(x)

    return pl.pallas_call(body, name="silu_rows", out_shape=jax.ShapeDtypeStruct(c.shape, F32))(c)


ANY = pl.BlockSpec(memory_space=pl.ANY)


def _position():
    return lax.axis_index("x"), lax.axis_index("y"), lax.axis_index("c")


def _other_chips(x, y):
    return [(1 - x, y), (x, 1 - y), (1 - x, 1 - y)]


def _allgather8(v):
    R, C = v.shape

    def body(v_ref, out_ref, send_sems, recv_sems):
        x, y, c = _position()
        me = 4 * x + 2 * y + c
        sends, recvs = [], []
        for k in range(1, N_DEV):
            px, py, pc = (x + (k >> 2)) % 2, (y + ((k >> 1) & 1)) % 2, (c + (k & 1)) % 2
            sends.append(pltpu.make_async_remote_copy(
                src_ref=v_ref, dst_ref=out_ref.at[me], send_sem=send_sems.at[k - 1], recv_sem=recv_sems.at[k - 1],
                device_id=(px, py, pc), device_id_type=MESH))
            recvs.append(pltpu.make_async_remote_copy(
                src_ref=v_ref, dst_ref=out_ref.at[4 * px + 2 * py + pc], send_sem=send_sems.at[k - 1],
                recv_sem=recv_sems.at[k - 1], device_id=(px, py, pc), device_id_type=MESH))
        for cp in sends:
            cp.start()
        for cp in recvs:
            cp.wait_recv()
        for cp in sends:
            cp.wait_send()

    others = pl.pallas_call(
        body, name="comm_allgather8", out_shape=jax.ShapeDtypeStruct((N_DEV, R, C), v.dtype), in_specs=[ANY],
        out_specs=ANY, scratch_shapes=[pltpu.SemaphoreType.DMA((N_DEV - 1,)), pltpu.SemaphoreType.DMA((N_DEV - 1,))],
    )(v)
    x, y, c = _position()
    return lax.dynamic_update_slice(others, v[None], (4 * x + 2 * y + c, 0, 0))


def _remote(src, dst, send_sems, recv_sems, k, to):
    return pltpu.make_async_remote_copy(src_ref=src, dst_ref=dst, send_sem=send_sems.at[k], recv_sem=recv_sems.at[k],
                                        device_id=to, device_id_type=MESH)


def _half_rows(ref, h):
    n = ref.shape[0] // 2
    return ref.at[pl.ds(h * n, n)]


N_SHARDS = 5


class _GatherPlan:
    def __init__(self, ins, outs, sems):
        win, wout, gate, up, down = ins
        win4, wout4, gu, down4 = outs
        send_sems, recv_sems, local_sems = sems
        x, y, c = _position()
        q = 2 * x + y
        sibling = (x, y, 1 - c)
        CG = gate.shape[1]
        tensors = [(win, lambda p: win4.at[p]), (wout, lambda p: wout4.at[p]),
                   (gate, lambda p: gu.at[0, :, pl.ds(p * CG, CG)]), (up, lambda p: gu.at[1, :, pl.ds(p * CG, CG)]),
                   (down, lambda p: down4.at[p])]
        self.own, self.first, self.landed, self.passed, self.passed_landed = [], [], [], [], []
        for t, (src, dst) in enumerate(tensors):
            self.own.append(pltpu.make_async_copy(src, dst(q), local_sems.at[t]))
            for j, (px, py) in enumerate(_other_chips(x, y)):
                self.first.append(_remote(_half_rows(src, c), _half_rows(dst(q), c), send_sems, recv_sems, 6 * t + j,
                                          (px, py, c)))
                mine = _half_rows(dst(2 * px + py), c)
                self.landed.append(_remote(mine, mine, send_sems, recv_sems, 6 * t + j, (px, py, c)))
                self.passed.append(_remote(mine, mine, send_sems, recv_sems, 6 * t + 3 + j, sibling))
                other = _half_rows(dst(2 * px + py), 1 - c)
                self.passed_landed.append(_remote(other, other, send_sems, recv_sems, 6 * t + 3 + j, sibling))

    def start(self):
        for cp in self.own + self.first:
            cp.start()

    def forward(self):
        for landed, onward in zip(self.landed, self.passed):
            landed.wait_recv()
            onward.start()

    def finish(self):
        for cp in self.passed_landed:
            cp.wait_recv()
        for cp in self.first + self.passed:
            cp.wait_send()
        for cp in self.own:
            cp.wait()


def _gather_shapes(shards):
    win_s, wout_s, gate_s, up_s, down_s = shards
    return (jax.ShapeDtypeStruct((N_CHIPS,) + win_s.shape, BF16), jax.ShapeDtypeStruct((N_CHIPS,) + wout_s.shape, BF16),
            jax.ShapeDtypeStruct((2, gate_s.shape[0], N_CHIPS * gate_s.shape[1]), BF16),
            jax.ShapeDtypeStruct((N_CHIPS,) + down_s.shape, BF16))


GATHER_SEMS = [pltpu.SemaphoreType.DMA((6 * N_SHARDS,)), pltpu.SemaphoreType.DMA((6 * N_SHARDS,)),
               pltpu.SemaphoreType.DMA((N_SHARDS,))]


def _gather_weights(shards):
    def body(*refs):
        plan = _GatherPlan(refs[:N_SHARDS], refs[N_SHARDS:N_SHARDS + 4], refs[N_SHARDS + 4:])
        plan.start()
        plan.forward()
        plan.finish()

    return pl.pallas_call(
        body, name="comm_gather_weights", out_shape=_gather_shapes(shards), in_specs=[ANY] * N_SHARDS,
        out_specs=(ANY,) * 4, scratch_shapes=GATHER_SEMS,
    )(*shards)


def _sibling_swap_halves(gs):
    n_t = len(gs)

    def body(*refs):
        ins, outs, (send_sems, recv_sems) = refs[:n_t], refs[n_t:2 * n_t], refs[2 * n_t:]
        x, y, c = _position()
        cps = []
        for t in range(n_t):
            half = ins[t].shape[1] // 2
            cps.append(_remote(ins[t].at[:, pl.ds((1 - c) * half, half), :], outs[t], send_sems, recv_sems, t,
                               (x, y, 1 - c)))
        for cp in cps:
            cp.start()
        for cp in cps:
            cp.wait()

    shapes = tuple(jax.ShapeDtypeStruct((g.shape[0], g.shape[1] // 2, g.shape[2]), g.dtype) for g in gs)
    return pl.pallas_call(
        body, name="comm_sibling_swap", out_shape=shapes, in_specs=[ANY] * n_t, out_specs=(ANY,) * n_t,
        scratch_shapes=[pltpu.SemaphoreType.DMA((n_t,)), pltpu.SemaphoreType.DMA((n_t,))],
    )(*gs)


N_PAIRS = 4
SCATTER_COPIES = 3 * (N_PAIRS + 1)
SCATTER_SEMS = [pltpu.SemaphoreType.DMA((SCATTER_COPIES,)), pltpu.SemaphoreType.DMA((SCATTER_COPIES,))]


def _scatter_copies(ins, outs, sems):
    send_sems, recv_sems = sems
    n_l = N_PAIRS - 1
    CG = ins[n_l].shape[2] // N_CHIPS
    x, y, c = _position()
    cps = []
    for j, (px, py) in enumerate(_other_chips(x, y)):
        p = 2 * px + py
        for t in range(n_l):
            cps.append(_remote(ins[t].at[p], outs[t].at[j], send_sems, recv_sems, len(cps), (px, py, c)))
        for k in range(2):
            cps.append(_remote(ins[n_l].at[k, :, pl.ds(p * CG, CG)], outs[n_l].at[j, k], send_sems, recv_sems, len(cps),
                               (px, py, c)))
    return cps


def _scatter_shapes(pairs):
    gu = pairs[-1]
    return tuple(jax.ShapeDtypeStruct((3,) + g.shape[1:], g.dtype) for g in pairs[:-1]) + (
        jax.ShapeDtypeStruct((3, 2, gu.shape[1], gu.shape[2] // N_CHIPS), gu.dtype),)


def _chip_scatter(pairs):
    def body(*refs):
        cps = _scatter_copies(refs[:N_PAIRS], refs[N_PAIRS:2 * N_PAIRS], refs[2 * N_PAIRS:])
        for cp in cps:
            cp.start()
        for cp in cps:
            cp.wait()

    return pl.pallas_call(
        body, name="comm_chip_scatter", out_shape=_scatter_shapes(pairs), in_specs=[ANY] * N_PAIRS,
        out_specs=(ANY,) * N_PAIRS, scratch_shapes=SCATTER_SEMS,
    )(*pairs)


def _sibling_exchange(vs):
    n_t = len(vs)

    def body(*refs):
        ins, outs, (send_sems, recv_sems) = refs[:n_t], refs[n_t:2 * n_t], refs[2 * n_t:]
        x, y, c = _position()
        cps = [_remote(ins[t], outs[t], send_sems, recv_sems, t, (x, y, 1 - c)) for t in range(n_t)]
        for cp in cps:
            cp.start()
        for cp in cps:
            cp.wait()

    shapes = tuple(jax.ShapeDtypeStruct(v.shape, v.dtype) for v in vs)
    return pl.pallas_call(
        body, name="comm_sibling_exchange", out_shape=shapes, in_specs=[ANY] * n_t, out_specs=(ANY,) * n_t,
        scratch_shapes=[pltpu.SemaphoreType.DMA((n_t,)), pltpu.SemaphoreType.DMA((n_t,))],
    )(*vs)


def _join_halves(mine, theirs, sel):
    n, rows, cols = mine.shape
    tr = _row_tile(rows, cols)
    nb = rows // tr

    def body(sel_ref, a_ref, b_ref, o_ref):
        @pl.when(pl.program_id(1) == sel_ref[0])
        def _():
            o_ref[...] = a_ref[...]

        @pl.when(pl.program_id(1) != sel_ref[0])
        def _():
            o_ref[...] = b_ref[...]

    grid_spec = pltpu.PrefetchScalarGridSpec(
        num_scalar_prefetch=1, grid=(n, 2, nb),
        in_specs=[pl.BlockSpec((None, tr, cols), lambda k, h, i, s: (k, jnp.where(h == s[0], i, 0), 0)),
                  pl.BlockSpec((None, tr, cols), lambda k, h, i, s: (k, jnp.where(h == s[0], 0, i), 0))],
        out_specs=pl.BlockSpec((None, tr, cols), lambda k, h, i, s: (k, h * nb + i, 0)))
    return pl.pallas_call(
        body, name="join_halves", grid_spec=grid_spec, out_shape=jax.ShapeDtypeStruct((n, 2 * rows, cols), mine.dtype),
        compiler_params=_params(("parallel", "arbitrary", "arbitrary")),
    )(sel, mine, theirs)


def _reduce_start(grads, sel):
    got = _sibling_swap_halves(grads)
    return [_pair_sum(g, r, sel) for g, r in zip(grads, got)]


def _reduce_finish(pairs, arrived, sel):
    mine = [_sum_parts([(p, (1, 1, 0)), (a, 0), (a, 1), (a, 2)], sel, p.shape[1], p.shape[2], "chip_sum")
            for p, a in zip(pairs[:-1], arrived[:-1])]
    mine.append(_chip_sum_cols(pairs[-1], arrived[-1], sel))
    theirs = _sibling_exchange(mine)
    joined = [_join_halves(m.reshape((-1,) + m.shape[-2:]), t.reshape((-1,) + t.shape[-2:]), sel)
              for m, t in zip(mine, theirs)]
    return [j[0] for j in joined[:-1]] + [joined[-1]]


def _layer_fwd(x, mod, p, next_shards=None):
    sh1, sc1, gt1, sh2, sc2, gt2 = mod
    h1 = _norm_mod(x, p["g_mix"], sc1, sh1)
    z = _matmul(h1, p["w_in"], name="mm_in")
    y_pool = _pool_fwd(z, p["pool_w"], p["pool_scale"])
    F = _forget_cumsum(z, p["b_f"])
    Fh = F[:, :N_HEADS].T
    f_col, f_row = Fh[:, :, None], Fh[:, None, :]
    qkv = z[:, ZQ:ZQ + 3 * ATTN_WIDTH].astype(BF16)
    o_t, lse, *gathered = _flash_fwd(qkv, qkv[:, 2 * ATTN_WIDTH:].T, f_col, f_row, next_shards)
    y_lru, hs = _lru_fwd(z, p["lru_conv_w"], p["lru_conv_b"], p["lru_wa"], p["lru_ba"], p["lru_wi"], p["lru_bi"],
                         p["lru_lambda"])
    y = jnp.concatenate([y_pool.astype(BF16), o_t.T.astype(BF16), y_lru.astype(BF16)], axis=1)
    m1, x_mid = _matmul(y, p["w_out"], res=x, gate=gt1, name="mm_out")
    h2 = _norm_mod(x_mid, p["g_ffn"], sc2, sh2)
    au = _matmul(h2, p["w_gu"], gu="b", name="mm_gu")
    pa = _ffn_act(au, p["ffn_conv_w"], p["ffn_conv_b"])
    m2, x_out = _matmul(pa, p["w_down"], res=x_mid, gate=gt2, name="mm_down")
    saved = dict(x=x, h1=h1, z=z, qkv=qkv, f_col=f_col, f_row=f_row, o_t=o_t, lse=lse, hs=hs, y=y, m1=m1, x_mid=x_mid, h2=h2, au=au,
                 pa=pa, m2=m2)
    return x_out, saved, gathered


def _layer_bwd(dx_out, mod, p, s, pairs=None):
    sh1, sc1, gt1, sh2, sc2, gt2 = mod
    g = {}
    dm2, dgt2 = _gate_bwd(dx_out, s["m2"], gt2)
    dpa = _matmul(dm2, p["w_down"], nt=True, name="mm_down_dx")
    g["w_down"] = _matmul(s["pa"], dm2, ta=True, name="mm_down_dw")
    dau, g["ffn_conv_w"], g["ffn_conv_b"] = _ffn_act_bwd(s["au"], dpa, p["ffn_conv_w"], p["ffn_conv_b"])
    dh2 = _matmul(dau, p["w_gu"], nt=True, gu="b", name="mm_gu_dx")
    g["w_gu"] = _matmul(s["h2"], dau, ta=True, gu="out", name="mm_gu_dw")
    dx_mid, g["g_ffn"], dsc2, dsh2 = _norm_mod_bwd(s["x_mid"], dh2, dx_out, p["g_ffn"], sc2)
    dm1, dgt1 = _gate_bwd(dx_mid, s["m1"], gt1)
    dy = _matmul(dm1, p["w_out"], nt=True, name="mm_out_dx")
    g["w_out"] = _matmul(s["y"], dm1, ta=True, name="mm_out_dw")
    z = s["z"]
    (dzx, dzy, g["lru_conv_w"], g["lru_conv_b"], g["lru_wa"], g["lru_ba"], g["lru_wi"], g["lru_bi"],
     g["lru_lambda"]) = _lru_bwd(z, dy, s["hs"], p["lru_conv_w"], p["lru_conv_b"], p["lru_wa"], p["lru_ba"], p["lru_wi"],
                                 p["lru_bi"], p["lru_lambda"])
    qkv = s["qkv"]
    k_t = qkv[:, ATTN_WIDTH:2 * ATTN_WIDTH].T
    do = dy[:, POOL_WIDTH:POOL_WIDTH + ATTN_WIDTH]
    dq_t, dk, dv, dfk, dfq, *arrived = _flash_bwd(qkv, k_t, do.astype(BF16), do.T, s["o_t"], s["lse"], s["f_col"],
                                                  s["f_row"], pairs)
    dq = dq_t.T
    dF_pad = jnp.pad((dfq[:, 0, :] + dfk[:, :, 0]).T, ((0, 0), (0, 128 - N_HEADS)))
    dzf, db_f = _forget_cumsum_bwd(z, p["b_f"], dF_pad)
    g["b_f"] = db_f[:, :N_HEADS]
    dzp, g["pool_w"], g["pool_scale"] = _pool_bwd(z, dy, p["pool_w"], p["pool_scale"])
    S = z.shape[0]
    dz = jnp.concatenate([dzp, dq, dk, dv, dzx, dzy, dzf, jnp.zeros((S, ZW - ZF - 128), BF16)], axis=1)
    dh1 = _matmul(dz, p["w_in"], nt=True, name="mm_in_dx")
    g["w_in"] = _matmul(s["h1"], dz, ta=True, name="mm_in_dw")
    dx_in, g["g_mix"], dsc1, dsh1 = _norm_mod_bwd(s["x"], dh1, dx_mid, p["g_mix"], sc1)
    return dx_in, g, (dsh1, dsc1, dgt1, dsh2, dsc2, dgt2), arrived


def _big_weights(gathered):
    win4, wout4, gu, down4 = gathered
    D, F = win4.shape[1], gu.shape[2]
    return dict(w_in=_pad_in_cols(jnp.transpose(win4, (1, 0, 2)).reshape(D, N_IN)), w_out=wout4.reshape(D, D), w_gu=gu,
                w_down=down4.reshape(F, D))


def _big_grads(g):
    D, F = g["w_out"].shape[0], g["w_down"].shape[0]
    dwin4 = jnp.transpose(_unpad_in_cols(g["w_in"]).reshape(D, N_CHIPS, N_IN // N_CHIPS), (1, 0, 2))
    return [dwin4, g["w_out"].reshape(N_CHIPS, D // N_CHIPS, D), g["w_down"].reshape(N_CHIPS, F // N_CHIPS, D), g["w_gu"]]


def _local_step(x, target, mods, layers, final_g, shards=None, sel=None):
    L = len(layers)
    saved, params = [], []
    gathered = _gather_weights(shards[0]) if shards else None
    for l in range(L):
        p = {**layers[l], **_big_weights(gathered)} if shards else layers[l]
        x, s, gathered = _layer_fwd(x, mods[l], p, shards[l + 1] if shards and l + 1 < L else None)
        saved.append(s)
        params.append(p)
    dx, dfinal_g, loss = _loss_head(x, target, final_g)
    grads, dmods, reduced, pairs = [None] * L, [None] * L, [None] * L, None
    for l in reversed(range(L)):
        dx, grads[l], dmods[l], arrived = _layer_bwd(dx, mods[l], params[l], saved[l], pairs)
        if shards:
            if pairs is not None:
                reduced[l + 1] = _reduce_finish(pairs, arrived, sel)
            pairs = _reduce_start(_big_grads(grads[l]), sel)
    if shards:
        reduced[0] = _reduce_finish(pairs, _chip_scatter(pairs), sel)
    return loss, dx, grads, dmods, dfinal_g, reduced


def _pad_in_cols(w):
    D = w.shape[0]
    return jnp.concatenate([w[:, :3584], w[:, 3592:N_IN], w[:, 3584:3592], jnp.zeros((D, ZW - N_IN), w.dtype)], axis=1)


def _unpad_in_cols(w):
    return jnp.concatenate([w[:, :3584], w[:, ZF:ZF + N_HEADS], w[:, 3584:ZF]], axis=1)


BIG = ("w_in", "w_out", "w_ffn_gate", "w_ffn_up", "w_ffn_down")
SMALL = ("b_ada", "g_mix", "b_f", "pool_w", "pool_scale", "lru_conv_w", "lru_conv_b", "lru_wa", "lru_ba", "lru_wi",
         "lru_bi", "lru_lambda", "g_ffn", "ffn_conv_w", "ffn_conv_b", "final_g")
SHARDED_SMALL = ("lru_conv_w", "ffn_conv_w")
WEIGHTS = ("w_ada", "b_ada", "g_mix", "w_in", "b_f", "pool_w", "pool_scale", "lru_conv_w", "lru_conv_b", "lru_wa",
           "lru_ba", "lru_wi", "lru_bi", "lru_lambda", "w_out", "g_ffn", "w_ffn_gate", "w_ffn_up", "ffn_conv_w",
           "ffn_conv_b", "w_ffn_down", "final_g")


PACK_QUANTUM = 512 * 128


def _pack(arrays):
    flat = jnp.concatenate([a.reshape(-1).astype(F32) for a in arrays])
    n = -(-flat.shape[0] // PACK_QUANTUM) * PACK_QUANTUM
    return jnp.pad(flat, (0, n - flat.shape[0])).reshape(n // 128, 128)


def _unpack(packed, shapes):
    flat = packed.reshape(-1)
    out, off = [], 0
    for shp in shapes:
        n = int(np.prod(shp))
        out.append(flat[off:off + n].reshape(shp))
        off += n
    return out


def kernel(x, c, w_ada, b_ada, g_mix, w_in, b_f, pool_w, pool_scale, lru_conv_w, lru_conv_b, lru_wa, lru_ba, lru_wi, lru_bi, lru_lambda, w_out, g_ffn, w_ffn_gate, w_ffn_up, ffn_conv_w, ffn_conv_b, w_ffn_down, final_g, loss_target, m_w_ada, m_b_ada, m_g_mix, m_w_in, m_b_f, m_pool_w, m_pool_scale, m_lru_conv_w, m_lru_conv_b, m_lru_wa, m_lru_ba, m_lru_wi, m_lru_bi, m_lru_lambda, m_w_out, m_g_ffn, m_w_ffn_gate, m_w_ffn_up, m_ffn_conv_w, m_ffn_conv_b, m_w_ffn_down, m_final_g, v_w_ada, v_b_ada, v_g_mix, v_w_in, v_b_f, v_pool_w, v_pool_scale, v_lru_conv_w, v_lru_conv_b, v_lru_wa, v_lru_ba, v_lru_wi, v_lru_bi, v_lru_lambda, v_w_out, v_g_ffn, v_w_ffn_gate, v_w_ffn_up, v_ffn_conv_w, v_ffn_conv_b, v_w_ffn_down, v_final_g):
    env = dict(locals())
    W = {n: env[n] for n in WEIGHTS}
    M = {n: env["m_" + n] for n in WEIGHTS}
    V = {n: env["v_" + n] for n in WEIGHTS}
    L, D = g_mix.shape
    S = x.shape[1]
    F = 4 * w_ffn_gate.shape[2]
    ix, iy, ic = lax.axis_index("x"), lax.axis_index("y"), lax.axis_index("c")
    q = 2 * ix + iy
    me = 2 * q + ic

    head = _allgather8(_pack([c, lru_conv_w, ffn_conv_w]))
    nlc, nfc = lru_conv_w.size, ffn_conv_w.size
    c_all = head.reshape(N_DEV, -1)[:, :D]
    lru_cw = jnp.concatenate([head[2 * k].reshape(-1)[D:D + nlc].reshape(L, 4, -1) for k in range(N_CHIPS)], axis=2)
    ffn_cw = jnp.concatenate([head[2 * k].reshape(-1)[D + nlc:D + nlc + nfc].reshape(L, 3, -1) for k in range(N_CHIPS)],
                             axis=2)
    cact = _silu_rows(c_all)

    NA = w_ada.shape[2]
    mod_part = jnp.stack([_matmul(cact, w_ada[l], name="mm_ada") for l in range(L)])
    mod_all = _allgather8(mod_part.reshape(L * N_DEV, NA)).reshape(N_DEV, L, N_DEV, NA)
    mod_full = jnp.concatenate([mod_all[2 * k] for k in range(N_CHIPS)], axis=2)
    mod_mine = lax.dynamic_index_in_dim(mod_full, me, axis=1, keepdims=False) + b_ada
    mods = [[mod_mine[l, k * D:(k + 1) * D].reshape(1, D) for k in range(6)] for l in range(L)]

    shards_all = [W[n].astype(BF16) for n in BIG]
    shards = [[w[l] for w in shards_all] for l in range(L)]
    layers = []
    for l in range(L):
        layers.append(dict(
            g_mix=g_mix[l][None], g_ffn=g_ffn[l][None], b_f=jnp.pad(b_f[l], (0, 128 - N_HEADS))[None],
            pool_w=pool_w[l], pool_scale=pool_scale[l][None], lru_conv_w=lru_cw[l], lru_conv_b=lru_conv_b[l][None],
            lru_wa=lru_wa[l], lru_ba=lru_ba[l][None], lru_wi=lru_wi[l], lru_bi=lru_bi[l][None],
            lru_lambda=lru_lambda[l][None], ffn_conv_w=ffn_cw[l], ffn_conv_b=ffn_conv_b[l][None]))

    sel = jnp.stack([ic, q]).astype(jnp.int32)
    loss, dx, grads, dmods, dfinal_g, reduced = _local_step(x[0], loss_target[0], mods, layers, final_g[None], shards, sel)

    G = {n: [] for n in BIG}
    for g_in, g_out, g_down, g_gu in reduced:
        for n, a in zip(BIG, (g_in, g_out, g_gu[0], g_gu[1], g_down)):
            G[n].append(a)
    G = {n: jnp.stack(G[n]) for n in BIG}

    stack = lambda name: jnp.stack([grads[l][name] for l in range(L)])
    dmod = jnp.stack([jnp.concatenate(dmods[l], axis=1)[0] for l in range(L)])
    small = dict(b_ada=dmod, g_mix=stack("g_mix"), b_f=stack("b_f"), pool_w=stack("pool_w"),
                 pool_scale=stack("pool_scale"), lru_conv_w=stack("lru_conv_w"), lru_conv_b=stack("lru_conv_b"),
                 lru_wa=stack("lru_wa"), lru_ba=stack("lru_ba"), lru_wi=stack("lru_wi"), lru_bi=stack("lru_bi"),
                 lru_lambda=stack("lru_lambda"), g_ffn=stack("g_ffn"), ffn_conv_w=stack("ffn_conv_w"),
                 ffn_conv_b=stack("ffn_conv_b"), final_g=dfinal_g)
    packed = _pack([small[n] for n in SMALL] + [loss[0, :1]])
    everyone = _allgather8(packed)
    zero_sel = jnp.zeros((2,), jnp.int32)
    total = _sum_parts([(everyone, k) for k in range(N_DEV)], zero_sel, packed.shape[0], 128, "device_sum")
    sums = _unpack(total, [small[n].shape for n in SMALL] + [(1,)])
    loss_total = sums[-1][0]
    for n, a in zip(SMALL, sums[:-1]):
        a = a.reshape((L, -1, a.shape[-1])) if n in SHARDED_SMALL else a.reshape(W[n].shape)
        if n in SHARDED_SMALL:
            a = lax.dynamic_slice_in_dim(a, q * W[n].shape[2], W[n].shape[2], axis=2)
        G[n] = a

    dmod_all = everyone.reshape(N_DEV, -1)[:, :L * 6 * D].reshape(N_DEV, L, 6 * D)
    dmod_cols = jnp.transpose(lax.dynamic_slice_in_dim(dmod_all, q * NA, NA, axis=2), (1, 0, 2))
    G["w_ada"], d_ada, m_ada, v_ada = _ada_grad_adamw(cact.T, dmod_cols, w_ada, m_w_ada, v_w_ada)
    delta, new_m, new_v = {"w_ada": d_ada}, {"w_ada": m_ada}, {"w_ada": v_ada}

    for n in BIG:
        cols = W[n].shape[-1]
        outs = _adamw(*[a.reshape(-1, cols) for a in (W[n], G[n], M[n], V[n])])
        delta[n], new_m[n], new_v[n] = [o.reshape(W[n].shape) for o in outs]
    outs = _adamw(*[_pack([t[n] for n in SMALL]) for t in (W, G, M, V)])
    shapes = [W[n].shape for n in SMALL]
    for tgt, o in zip((delta, new_m, new_v), outs):
        for n, a in zip(SMALL, _unpack(o, shapes)):
            tgt[n] = a

    return (loss_total, dx[None], *[G[n] for n in WEIGHTS], *[delta[n] for n in WEIGHTS],
            *[new_m[n] for n in WEIGHTS], *[new_v[n] for n in WEIGHTS])
```

```python
import functools
import math

import jax
import jax.numpy as jnp
import numpy as np
from jax import lax
from jax.experimental import pallas as pl
from jax.experimental.pallas import tpu as pltpu

F32 = jnp.float32
BF16 = jnp.bfloat16
MESH = pl.DeviceIdType.MESH

EPS = 1e-6
HEAD_DIM = 128
POOL_WIDTH = 512
POOL_WINDOWS = (2, 4, 8, 16)
ATTN_WIDTH = 1024
N_HEADS = 8
LRU_WIDTH = 512
LRU_C = 8.0
N_IN = 4616
ZP, ZQ, ZK, ZV, ZX, ZY, ZF, ZW = 0, 512, 1536, 2560, 3584, 4096, 4608, 5120
FFN_CHUNK = 512
N_CHIPS = 4
N_DEV = 8

ADAM_LR, ADAM_B1, ADAM_B2, ADAM_EPS, ADAM_WD, ADAM_STEP = 0.001, 0.9, 0.999, 1e-08, 0.01, 10

TILES = dict(mm_m=512, mm_n=1024, mm_k=2816, row=512, attn=512, lru=256, cum=512, ew=256)
VMEM_LIMIT = 48 * 2**20


def _params(sem):
    return pltpu.CompilerParams(dimension_semantics=sem, vmem_limit_bytes=VMEM_LIMIT)


def _div_tile(n, pref, align):
    if n <= pref:
        return n
    t = (pref // align) * align
    while t >= align:
        if n % t == 0:
            return t
        t -= align
    raise ValueError(f"no tile for {n}")


def _sigmoid(x):
    return 0.5 * jnp.tanh(0.5 * x) + 0.5


def _gelu_parts(x):
    k = math.sqrt(2.0 / math.pi)
    u = k * (x + 0.044715 * x * x * x)
    t = jnp.tanh(u)
    gel = 0.5 * x * (1.0 + t)
    dgel = 0.5 * (1.0 + t) + 0.5 * x * (1.0 - t * t) * k * (1.0 + 3 * 0.044715 * x * x)
    return gel, dgel


def _neg_expm1(y):
    series = -y * (1.0 + y * (0.5 + y * (1.0 / 6 + y * (1.0 / 24 + y * (1.0 / 120)))))
    return jnp.where(y > -0.1, series, 1.0 - jnp.exp(y))


def _dot(a, b):
    return jnp.dot(a.astype(BF16), b.astype(BF16), preferred_element_type=F32)


def _dot_nt(a, b):
    return lax.dot_general(a.astype(BF16), b.astype(BF16), (((1,), (1,)), ((), ())), preferred_element_type=F32)


def _dot3(tri, v):
    hi = v.astype(BF16)
    r1 = v - hi.astype(F32)
    mid = r1.astype(BF16)
    lo = (r1 - mid.astype(F32)).astype(BF16)
    t = tri.astype(BF16)
    return (jnp.dot(t, hi, preferred_element_type=F32) + jnp.dot(t, mid, preferred_element_type=F32)
            + jnp.dot(t, lo, preferred_element_type=F32))


def _colsum(v):
    return jnp.sum(v, axis=0, keepdims=True)


def _rows(n, cols=128):
    return lax.broadcasted_iota(jnp.int32, (n, cols), 0)


def _matmul(a, b, *, nt=False, ta=False, out_dtype=F32, res=None, gate=None, gu=None, name="matmul"):
    M, K = a.shape[::-1] if ta else a.shape
    pair = 2 * FFN_CHUNK
    if gu == "b":
        N = b.shape[1] if nt else 2 * b.shape[2]
    else:
        N = b.shape[0] if nt else b.shape[1]
    tm = _div_tile(M, TILES["mm_m"] * (1 if gu is None else 2), 8)
    tn = _div_tile(N, TILES["mm_n"], 128)
    tk = _div_tile(K, TILES["mm_k"], 128)
    if gu == "b" and nt:
        tk = pair
    elif gu == "b":
        tn = pair
    elif gu == "out":
        tn = FFN_CHUNK
    nk = K // tk
    epi = res is not None
    n_b = 2 if gu == "b" else 1

    def body(*refs):
        a_ref, b_refs, rest = refs[0], refs[1:1 + n_b], refs[1 + n_b:]
        if epi:
            res_ref, gate_ref, o_ref, x_ref = rest[:4]
        else:
            o_ref = rest[0]
        if gu == "b" and nt:
            part = (_dot_nt(a_ref[:, :FFN_CHUNK], b_refs[0][...]) + _dot_nt(a_ref[:, FFN_CHUNK:], b_refs[1][...]))
        elif gu == "b":
            part = jnp.concatenate([_dot(a_ref[...], b_refs[0][...]), _dot(a_ref[...], b_refs[1][...])], axis=1)
        else:
            if ta:
                part = lax.dot_general(a_ref[...].astype(BF16), b_refs[0][...].astype(BF16), (((0,), (0,)), ((), ())),
                                       preferred_element_type=F32)
            else:
                part = _dot_nt(a_ref[...], b_refs[0][...]) if nt else _dot(a_ref[...], b_refs[0][...])

        def finish(acc):
            o_ref[...] = acc.astype(o_ref.dtype)
            if epi:
                x_ref[...] = res_ref[...] + gate_ref[...] * acc

        if nk == 1:
            finish(part)
        else:
            acc_ref = refs[-1]
            k = pl.program_id(2)

            @pl.when(k == 0)
            def _():
                acc_ref[...] = part

            @pl.when(k > 0)
            def _():
                acc_ref[...] += part

            @pl.when(k == nk - 1)
            def _():
                finish(acc_ref[...])

    a_spec = pl.BlockSpec((tk, tm), lambda i, j, k: (k, i)) if ta else pl.BlockSpec((tm, tk), lambda i, j, k: (i, k))
    o_spec = pl.BlockSpec((tm, tn), lambda i, j, k: (i, j))
    out_shape = jax.ShapeDtypeStruct((M, N), out_dtype)
    if gu == "b" and nt:
        b_specs = [pl.BlockSpec((None, tn, FFN_CHUNK), lambda i, j, k, p=p: (p, j, k)) for p in range(2)]
    elif gu == "b":
        b_specs = [pl.BlockSpec((None, tk, FFN_CHUNK), lambda i, j, k, p=p: (p, k, j)) for p in range(2)]
    elif nt:
        b_specs = [pl.BlockSpec((tn, tk), lambda i, j, k: (j, k))]
    else:
        b_specs = [pl.BlockSpec((tk, tn), lambda i, j, k: (k, j))]
    if gu == "out":
        o_spec = pl.BlockSpec((None, tm, tn), lambda i, j, k: (j % 2, i, j // 2))
        out_shape = jax.ShapeDtypeStruct((2, M, N // 2), out_dtype)
    in_specs, args = [a_spec] + b_specs, [a] + [b] * n_b
    out_specs = o_spec
    if epi:
        in_specs += [o_spec, pl.BlockSpec((1, tn), lambda i, j, k: (0, j))]
        args += [res, gate]
        out_specs = (o_spec, o_spec)
        out_shape = (out_shape, jax.ShapeDtypeStruct((M, N), F32))
    return pl.pallas_call(
        body, name=name, grid=(M // tm, N // tn, nk), in_specs=in_specs, out_specs=out_specs, out_shape=out_shape,
        scratch_shapes=[pltpu.VMEM((tm, tn), F32)] if nk > 1 else [],
        compiler_params=_params(("parallel", "parallel", "arbitrary")),
    )(*args)


def _norm_mod(x, g, sc, sh):
    S, D = x.shape
    tr = _div_tile(S, TILES["row"], 8)

    def body(x_ref, g_ref, sc_ref, sh_ref, h_ref):
        xf = x_ref[...]
        r = lax.rsqrt(jnp.mean(xf * xf, axis=-1, keepdims=True) + EPS)
        h_ref[...] = (((xf * r) * g_ref[...]) * (1.0 + sc_ref[...]) + sh_ref[...]).astype(h_ref.dtype)

    row = pl.BlockSpec((tr, D), lambda i: (i, 0))
    vec = pl.BlockSpec((1, D), lambda i: (0, 0))
    return pl.pallas_call(
        body, name="norm_mod", grid=(S // tr,), in_specs=[row, vec, vec, vec], out_specs=row,
        out_shape=jax.ShapeDtypeStruct((S, D), BF16), compiler_params=_params(("parallel",)),
    )(x, g, sc, sh)


def _norm_mod_bwd(x, dh, dres, g, sc):
    S, D = x.shape
    tr = _div_tile(S, TILES["ew"], 8)

    def body(x_ref, dh_ref, dres_ref, g_ref, sc_ref, dx_ref, dg_ref, dsc_ref, dsh_ref):
        xf, dh_ = x_ref[...], dh_ref[...]
        r = lax.rsqrt(jnp.mean(xf * xf, axis=-1, keepdims=True) + EPS)
        xhat = xf * r
        dxhat = dh_ * (g_ref[...] * (1.0 + sc_ref[...]))
        dx_ref[...] = dres_ref[...] + r * (dxhat - xhat * jnp.mean(dxhat * xhat, axis=-1, keepdims=True))
        t = _colsum(dh_ * xhat)

        @pl.when(pl.program_id(0) == 0)
        def _():
            dg_ref[...] = jnp.zeros_like(dg_ref)
            dsc_ref[...] = jnp.zeros_like(dsc_ref)
            dsh_ref[...] = jnp.zeros_like(dsh_ref)

        dg_ref[...] += t * (1.0 + sc_ref[...])
        dsc_ref[...] += t * g_ref[...]
        dsh_ref[...] += _colsum(dh_)

    row = pl.BlockSpec((tr, D), lambda i: (i, 0))
    vec = pl.BlockSpec((1, D), lambda i: (0, 0))
    vshape = jax.ShapeDtypeStruct((1, D), F32)
    return pl.pallas_call(
        body, name="norm_mod_bwd", grid=(S // tr,), in_specs=[row, row, row, vec, vec], out_specs=(row, vec, vec, vec),
        out_shape=(jax.ShapeDtypeStruct((S, D), F32), vshape, vshape, vshape), compiler_params=_params(("arbitrary",)),
    )(x, dh, dres, g, sc)


def _gate_bwd(dx, m, gt):
    S, D = dx.shape
    tr = _div_tile(S, TILES["row"], 8)

    def body(dx_ref, m_ref, gt_ref, dm_ref, dgt_ref):
        d = dx_ref[...]
        dm_ref[...] = (d * gt_ref[...]).astype(dm_ref.dtype)

        @pl.when(pl.program_id(0) == 0)
        def _():
            dgt_ref[...] = jnp.zeros_like(dgt_ref)

        dgt_ref[...] += _colsum(d * m_ref[...])

    row = pl.BlockSpec((tr, D), lambda i: (i, 0))
    vec = pl.BlockSpec((1, D), lambda i: (0, 0))
    return pl.pallas_call(
        body, name="gate_bwd", grid=(S // tr,), in_specs=[row, row, vec], out_specs=(row, vec),
        out_shape=(jax.ShapeDtypeStruct((S, D), BF16), jax.ShapeDtypeStruct((1, D), F32)),
        compiler_params=_params(("arbitrary",)),
    )(dx, m, gt)


def _loss_head(x, target, g):
    S, D = x.shape
    tr = _div_tile(S, TILES["ew"], 8)

    def body(x_ref, t_ref, g_ref, dx_ref, dg_ref, loss_ref):
        xf = x_ref[...]
        r = lax.rsqrt(jnp.mean(xf * xf, axis=-1, keepdims=True) + EPS)
        xhat = xf * r
        err = xhat * g_ref[...] - t_ref[...]
        dy = err * (1.0 / D)
        dxhat = dy * g_ref[...]
        dx_ref[...] = r * (dxhat - xhat * jnp.mean(dxhat * xhat, axis=-1, keepdims=True))

        @pl.when(pl.program_id(0) == 0)
        def _():
            dg_ref[...] = jnp.zeros_like(dg_ref)
            loss_ref[...] = jnp.zeros_like(loss_ref)

        dg_ref[...] += _colsum(dy * xhat)
        loss_ref[...] += 0.5 * jnp.sum(jnp.mean(err * err, axis=-1, keepdims=True))

    row = pl.BlockSpec((tr, D), lambda i: (i, 0))
    vec = pl.BlockSpec((1, D), lambda i: (0, 0))
    one = pl.BlockSpec((1, 128), lambda i: (0, 0))
    return pl.pallas_call(
        body, name="loss_head", grid=(S // tr,), in_specs=[row, row, vec], out_specs=(row, vec, one),
        out_shape=(jax.ShapeDtypeStruct((S, D), F32), jax.ShapeDtypeStruct((1, D), F32),
                   jax.ShapeDtypeStruct((1, 128), F32)),
        compiler_params=_params(("arbitrary",)),
    )(x, target, g)


POOL_HALO = 16


def _pool_delta(ext, u, first_pos, tr):
    pos = (first_pos + _rows(tr) + 1).astype(F32)
    outs = []
    for gi, win in enumerate(POOL_WINDOWS):
        s = ext[:, gi * 128:(gi + 1) * 128]
        d = 1
        while d < win:
            s = s + pltpu.roll(s, d, 0)
            d *= 2
        outs.append(s[POOL_HALO:] / jnp.minimum(pos, float(win)) - u[:, gi * 128:(gi + 1) * 128])
    return outs


def _pool_fwd(z, w, scale):
    S = z.shape[0]
    tr = _div_tile(S, TILES["row"], POOL_HALO)
    hb = tr // POOL_HALO

    def body(z_ref, halo_ref, w_ref, sc_ref, y_ref):
        i = pl.program_id(0)
        u = z_ref[...]
        halo = jnp.where(i > 0, halo_ref[...], 0.0)
        ds_ = _pool_delta(jnp.concatenate([halo, u], axis=0), u, i * tr, tr)
        for gi in range(4):
            y_ref[:, gi * 128:(gi + 1) * 128] = _dot(ds_[gi], w_ref[gi]) * sc_ref[:, gi * 128:(gi + 1) * 128]

    return pl.pallas_call(
        body, name="pool_fwd", grid=(S // tr,),
        in_specs=[pl.BlockSpec((tr, POOL_WIDTH), lambda i: (i, 0)),
                  pl.BlockSpec((POOL_HALO, POOL_WIDTH), lambda i: (jnp.maximum(i * hb - 1, 0), 0)),
                  pl.BlockSpec((4, 128, 128), lambda i: (0, 0, 0)), pl.BlockSpec((1, POOL_WIDTH), lambda i: (0, 0))],
        out_specs=pl.BlockSpec((tr, POOL_WIDTH), lambda i: (i, 0)),
        out_shape=jax.ShapeDtypeStruct((S, POOL_WIDTH), F32), compiler_params=_params(("parallel",)),
    )(z, z, w, scale)


def _pool_bwd(z, dy, w, scale):
    S = z.shape[0]
    tr = _div_tile(S, TILES["row"], POOL_HALO)
    hb = tr // POOL_HALO
    nt = S // tr

    def body(z_ref, halo_ref, dy_ref, dyn_ref, w_ref, sc_ref, dz_ref, dw_ref, dsc_ref):
        i = pl.program_id(0)
        u = z_ref[...]
        halo = jnp.where(i > 0, halo_ref[...], 0.0)
        ds_ = _pool_delta(jnp.concatenate([halo, u], axis=0), u, i * tr, tr)
        dy_ext = jnp.concatenate([dy_ref[...], jnp.where(i < nt - 1, dyn_ref[...], 0.0)], axis=0)
        pos = (i * tr + _rows(tr + POOL_HALO) + 1).astype(F32)

        @pl.when(i == 0)
        def _():
            dw_ref[...] = jnp.zeros_like(dw_ref)
            dsc_ref[...] = jnp.zeros_like(dsc_ref)

        for gi, win in enumerate(POOL_WINDOWS):
            cols = slice(gi * 128, (gi + 1) * 128)
            dyg = dy_ext[:, cols]
            dys = dyg * sc_ref[:, cols]
            dsc_ref[:, cols] += _colsum(dyg[:tr] * _dot(ds_[gi], w_ref[gi]))
            dw_ref[gi] += _dot(ds_[gi].T, dys[:tr])
            dd = _dot_nt(dys, w_ref[gi])
            e = dd / jnp.minimum(pos, float(win))
            d = 1
            while d < win:
                e = e + pltpu.roll(e, tr + POOL_HALO - d, 0)
                d *= 2
            dz_ref[:, cols] = (e[:tr] - dd[:tr]).astype(dz_ref.dtype)

    return pl.pallas_call(
        body, name="pool_bwd", grid=(nt,),
        in_specs=[pl.BlockSpec((tr, POOL_WIDTH), lambda i: (i, 0)),
                  pl.BlockSpec((POOL_HALO, POOL_WIDTH), lambda i: (jnp.maximum(i * hb - 1, 0), 0)),
                  pl.BlockSpec((tr, POOL_WIDTH), lambda i: (i, 0)),
                  pl.BlockSpec((POOL_HALO, POOL_WIDTH), lambda i: (jnp.minimum((i + 1) * hb, nt * hb - 1), 0)),
                  pl.BlockSpec((4, 128, 128), lambda i: (0, 0, 0)), pl.BlockSpec((1, POOL_WIDTH), lambda i: (0, 0))],
        out_specs=(pl.BlockSpec((tr, POOL_WIDTH), lambda i: (i, 0)), pl.BlockSpec((4, 128, 128), lambda i: (0, 0, 0)),
                   pl.BlockSpec((1, POOL_WIDTH), lambda i: (0, 0))),
        out_shape=(jax.ShapeDtypeStruct((S, POOL_WIDTH), BF16), jax.ShapeDtypeStruct((4, 128, 128), F32),
                   jax.ShapeDtypeStruct((1, POOL_WIDTH), F32)),
        compiler_params=_params(("arbitrary",)),
    )(z, z, dy, dy, w, scale)


def _log_sigmoid(x):
    return jnp.minimum(x, 0.0) - jnp.log(1.0 + jnp.exp(-jnp.abs(x)))


def _forget_cumsum(z, b_f):
    S = z.shape[0]
    tr = _div_tile(S, TILES["cum"], 8)
    zf_block = ZF // 128

    def body(z_ref, b_ref, f_ref, carry):
        @pl.when(pl.program_id(0) == 0)
        def _():
            carry[...] = jnp.zeros_like(carry)

        lf = _log_sigmoid(z_ref[...] + b_ref[...])
        tri = lax.broadcasted_iota(jnp.int32, (tr, tr), 1) <= lax.broadcasted_iota(jnp.int32, (tr, tr), 0)
        f_ref[...] = _dot3(tri, lf) + carry[...]
        carry[...] += _colsum(lf)

    return pl.pallas_call(
        body, name="forget_cumsum", grid=(S // tr,),
        in_specs=[pl.BlockSpec((tr, 128), lambda i: (i, zf_block)), pl.BlockSpec((1, 128), lambda i: (0, 0))],
        out_specs=pl.BlockSpec((tr, 128), lambda i: (i, 0)), out_shape=jax.ShapeDtypeStruct((S, 128), F32),
        scratch_shapes=[pltpu.VMEM((1, 128), F32)], compiler_params=_params(("arbitrary",)),
    )(z, b_f)


def _forget_cumsum_bwd(z, b_f, dF):
    S = z.shape[0]
    tr = _div_tile(S, TILES["cum"], 8)
    nt = S // tr
    zf_block = ZF // 128

    def body(z_ref, b_ref, df_ref, dz_ref, db_ref, carry):
        @pl.when(pl.program_id(0) == 0)
        def _():
            carry[...] = jnp.zeros_like(carry)
            db_ref[...] = jnp.zeros_like(db_ref)

        dF_ = df_ref[...]
        tri = lax.broadcasted_iota(jnp.int32, (tr, tr), 1) >= lax.broadcasted_iota(jnp.int32, (tr, tr), 0)
        dlf = _dot3(tri, dF_) + carry[...]
        carry[...] += _colsum(dF_)
        lane = lax.broadcasted_iota(jnp.int32, (tr, 128), 1)
        dzf = jnp.where(lane < N_HEADS, dlf * _sigmoid(-(z_ref[...] + b_ref[...])), 0.0)
        dz_ref[...] = dzf.astype(dz_ref.dtype)
        db_ref[...] += _colsum(dzf)

    return pl.pallas_call(
        body, name="forget_cumsum_bwd", grid=(nt,),
        in_specs=[pl.BlockSpec((tr, 128), lambda i: (nt - 1 - i, zf_block)), pl.BlockSpec((1, 128), lambda i: (0, 0)),
                  pl.BlockSpec((tr, 128), lambda i: (nt - 1 - i, 0))],
        out_specs=(pl.BlockSpec((tr, 128), lambda i: (nt - 1 - i, 0)), pl.BlockSpec((1, 128), lambda i: (0, 0))),
        out_shape=(jax.ShapeDtypeStruct((S, 128), BF16), jax.ShapeDtypeStruct((1, 128), F32)),
        scratch_shapes=[pltpu.VMEM((1, 128), F32)], compiler_params=_params(("arbitrary",)),
    )(z, b_f, dF)


NEG = -1e30
ATTN_SCALE = HEAD_DIM ** -0.5


def _on_block_kind(qi, kj, fn):
    @pl.when(qi == kj)
    def _():
        fn(True)

    @pl.when(qi != kj)
    def _():
        fn(False)


FIRST, LAST, HEAD_FIRST, HEAD_LAST, KEY_ZERO = 1, 2, 4, 8, 16


def _tri_schedule(n, by_key=False):
    outer, inner, flags = [], [], []
    for a in range(n):
        partners = list(range(a, n)) if by_key else list(range(a + 1))
        for idx, b in enumerate(partners):
            f = FIRST if idx == 0 else 0
            f |= LAST if idx == len(partners) - 1 else 0
            f |= KEY_ZERO if (a if by_key else b) == 0 else 0
            outer.append(a)
            inner.append(b)
            flags.append(f)
    flags[0] |= HEAD_FIRST
    flags[-1] |= HEAD_LAST
    return [jnp.asarray(np.array(v, np.int32)) for v in (outer, inner, flags)]


def _flash_call(body, name, sched, in_specs, out_specs, out_shape, scratch, comm_ins=0, comm_shapes=(), comm_sems=()):
    grid_spec = pltpu.PrefetchScalarGridSpec(
        num_scalar_prefetch=3, grid=(N_HEADS, int(sched[0].shape[0])), in_specs=list(in_specs) + [ANY] * comm_ins,
        out_specs=tuple(out_specs) + (ANY,) * len(comm_shapes), scratch_shapes=list(scratch) + list(comm_sems))
    heads = "arbitrary" if comm_sems else "parallel"
    return pl.pallas_call(body, name=name, grid_spec=grid_spec, out_shape=tuple(out_shape) + tuple(comm_shapes),
                          compiler_params=_params((heads, "arbitrary")))


def _scores_t(q, k, fq_row, fk_col, diagonal, floor):
    st = _dot_nt(k, q) * ATTN_SCALE + fq_row - fk_col
    if not diagonal:
        return st
    t = st.shape[0]
    return jnp.where(lax.broadcasted_iota(jnp.int32, (t, t), 0) <= lax.broadcasted_iota(jnp.int32, (t, t), 1), st, floor)


def _flash_fwd(z, v_t, f_col, f_row, next_shards=None):
    S = z.shape[0]
    t = _div_tile(S, TILES["attn"], 128)
    sched = _tri_schedule(S // t)
    n_steps = int(sched[0].shape[0])
    qb, kb = ZQ // 128, ZK // 128
    n_comm = N_SHARDS if next_shards is not None else 0

    def body(qt, kt, ft, q_ref, k_ref, vt_ref, fq_ref, fk_ref, *rest):
        shard_refs, rest = rest[:n_comm], rest[n_comm:]
        (o_ref, lse_ref), rest = rest[:2], rest[2:]
        gathered, rest = (rest[:4], rest[4:]) if n_comm else ((), rest)
        (m_sc, l_sc, acc_sc), sems = rest[:3], rest[3:]
        head, step = pl.program_id(0), pl.program_id(1)
        qi, kj, fl = qt[step], kt[step], ft[step]

        if n_comm:
            @pl.when((head == 0) & (step == 0))
            def _():
                _GatherPlan(shard_refs, gathered, sems).start()

            @pl.when((head == N_HEADS - 2) & (step == 0))
            def _():
                _GatherPlan(shard_refs, gathered, sems).forward()

            @pl.when((head == N_HEADS - 1) & (step == n_steps - 1))
            def _():
                _GatherPlan(shard_refs, gathered, sems).finish()

        @pl.when((fl & FIRST) != 0)
        def _():
            m_sc[...] = jnp.full_like(m_sc, NEG)
            l_sc[...] = jnp.zeros_like(l_sc)
            acc_sc[...] = jnp.zeros_like(acc_sc)

        def update(diagonal):
            st = _scores_t(q_ref[...], k_ref[...], fq_ref[...], fk_ref[...], diagonal, NEG)
            m_new = jnp.maximum(m_sc[...], jnp.max(st, axis=0, keepdims=True))
            alpha = jnp.exp(m_sc[...] - m_new)
            pt = jnp.exp(st - m_new)
            l_sc[...] = alpha * l_sc[...] + jnp.sum(pt, axis=0, keepdims=True)
            acc_sc[...] = alpha * acc_sc[...] + _dot(vt_ref[...], pt)
            m_sc[...] = m_new

        _on_block_kind(qi, kj, update)

        @pl.when((fl & LAST) != 0)
        def _():
            o_ref[...] = acc_sc[...] / l_sc[...]
            lse_ref[...] = m_sc[...] + jnp.log(l_sc[...])

    row = pl.BlockSpec((None, 1, t), lambda h, s, qt, kt, ft: (h, 0, qt[s]))
    return _flash_call(
        body, "flash_fwd", sched,
        [pl.BlockSpec((t, 128), lambda h, s, qt, kt, ft: (qt[s], qb + h)),
         pl.BlockSpec((t, 128), lambda h, s, qt, kt, ft: (kt[s], kb + h)),
         pl.BlockSpec((128, t), lambda h, s, qt, kt, ft: (h, kt[s])), row,
         pl.BlockSpec((None, t, 1), lambda h, s, qt, kt, ft: (h, kt[s], 0))],
        (pl.BlockSpec((128, t), lambda h, s, qt, kt, ft: (h, qt[s])), row),
        (jax.ShapeDtypeStruct((ATTN_WIDTH, S), F32), jax.ShapeDtypeStruct((N_HEADS, 1, S), F32)),
        [pltpu.VMEM((1, t), F32), pltpu.VMEM((1, t), F32), pltpu.VMEM((128, t), F32)],
        comm_ins=n_comm, comm_shapes=_gather_shapes(next_shards) if n_comm else (),
        comm_sems=GATHER_SEMS if n_comm else (),
    )(*sched, z, z, v_t, f_row, f_col, *(next_shards or ()))


def _flash_bwd(z, k_t, dy, do_t, o_t, lse_row, f_col, f_row, pairs=None):
    S = z.shape[0]
    t = _div_tile(S, TILES["attn"], 128)
    n = S // t
    sched = _tri_schedule(n, by_key=True)
    n_steps = int(sched[0].shape[0])
    qb, kb, vb = ZQ // 128, ZK // 128, ZV // 128
    dob = POOL_WIDTH // 128
    n_comm = N_PAIRS if pairs is not None else 0

    def body(kt, qt, ft, q_ref, k_ref, v_ref, kt_ref, do_ref, dot_ref, ot_ref, lse_ref, fq_ref, fk_ref, *rest):
        pair_refs, rest = rest[:n_comm], rest[n_comm:]
        (dq_ref, dk_ref, dv_ref, dfk_ref, dfq_ref), rest = rest[:5], rest[5:]
        arrived, rest = rest[:n_comm], rest[n_comm:]
        (dk_sc, dv_sc, dfk_sc, dq_sc, dfq_sc, delta_sc), sems = rest[:6], rest[6:]
        head, step = pl.program_id(0), pl.program_id(1)
        kj, qi, fl = kt[step], qt[step], ft[step]

        if n_comm:
            @pl.when((head == 0) & (step == 0))
            def _():
                for cp in _scatter_copies(pair_refs, arrived, sems):
                    cp.start()

            @pl.when((head == N_HEADS - 1) & (step == n_steps - 1))
            def _():
                for cp in _scatter_copies(pair_refs, arrived, sems):
                    cp.wait()

        @pl.when((fl & HEAD_FIRST) != 0)
        def _():
            dq_sc[...] = jnp.zeros_like(dq_sc)
            dfq_sc[...] = jnp.zeros_like(dfq_sc)

        @pl.when((fl & FIRST) != 0)
        def _():
            dk_sc[...] = jnp.zeros_like(dk_sc)
            dv_sc[...] = jnp.zeros_like(dv_sc)
            dfk_sc[...] = jnp.zeros_like(dfk_sc)

        @pl.when((fl & KEY_ZERO) != 0)
        def _():
            delta_sc[qi] = jnp.sum(dot_ref[...] * ot_ref[...], axis=0, keepdims=True)

        def update(diagonal):
            pt = jnp.exp(_scores_t(q_ref[...], k_ref[...], fq_ref[...], fk_ref[...], diagonal, NEG) - lse_ref[...])
            dv_sc[...] += _dot(pt, do_ref[...])
            dst = pt * (_dot_nt(v_ref[...], do_ref[...]) - delta_sc[qi])
            dk_sc[...] += _dot(dst, q_ref[...])
            dfk_sc[...] += jnp.sum(dst, axis=1, keepdims=True)
            dfq_sc[qi] = dfq_sc[qi] + jnp.sum(dst, axis=0, keepdims=True)
            dq_sc[qi] = dq_sc[qi] + _dot(kt_ref[...], dst)

        _on_block_kind(qi, kj, update)

        @pl.when((fl & LAST) != 0)
        def _():
            dk_ref[...] = (dk_sc[...] * ATTN_SCALE).astype(dk_ref.dtype)
            dv_ref[...] = dv_sc[...].astype(dv_ref.dtype)
            dfk_ref[...] = -dfk_sc[...]

        @pl.when((fl & HEAD_LAST) != 0)
        def _():
            for i in range(n):
                dq_ref[:, i * t:(i + 1) * t] = (dq_sc[i] * ATTN_SCALE).astype(dq_ref.dtype)
                dfq_ref[:, i * t:(i + 1) * t] = dfq_sc[i]

    def qs(block):
        return pl.BlockSpec((t, 128), lambda h, s, kt, qt, ft: (qt[s], block + h))

    def kv(block):
        return pl.BlockSpec((t, 128), lambda h, s, kt, qt, ft: (kt[s], block + h))

    first_sweep = pl.BlockSpec((128, t), lambda h, s, kt, qt, ft: (h, jnp.where(kt[s] == 0, qt[s], n - 1)))
    qrow = pl.BlockSpec((None, 1, t), lambda h, s, kt, qt, ft: (h, 0, qt[s]))
    kcol = pl.BlockSpec((None, t, 1), lambda h, s, kt, qt, ft: (h, kt[s], 0))
    out = pl.BlockSpec((t, 128), lambda h, s, kt, qt, ft: (kt[s], h))
    return _flash_call(
        body, "flash_bwd", sched,
        [qs(qb), kv(kb), kv(vb), pl.BlockSpec((128, t), lambda h, s, kt, qt, ft: (h, kt[s])), qs(dob), first_sweep,
         first_sweep, qrow, qrow, kcol],
        (pl.BlockSpec((128, S), lambda h, s, kt, qt, ft: (h, 0)), out, out, kcol,
         pl.BlockSpec((None, 1, S), lambda h, s, kt, qt, ft: (h, 0, 0))),
        (jax.ShapeDtypeStruct((ATTN_WIDTH, S), BF16), jax.ShapeDtypeStruct((S, ATTN_WIDTH), BF16),
         jax.ShapeDtypeStruct((S, ATTN_WIDTH), BF16), jax.ShapeDtypeStruct((N_HEADS, S, 1), F32),
         jax.ShapeDtypeStruct((N_HEADS, 1, S), F32)),
        [pltpu.VMEM((t, 128), F32), pltpu.VMEM((t, 128), F32), pltpu.VMEM((t, 1), F32), pltpu.VMEM((n, 128, t), F32),
         pltpu.VMEM((n, 1, t), F32), pltpu.VMEM((n, 1, t), F32)],
        comm_ins=n_comm, comm_shapes=_scatter_shapes(pairs) if n_comm else (), comm_sems=SCATTER_SEMS if n_comm else (),
    )(*sched, z, z, z, k_t, dy, do_t, o_t, lse_row, f_row, f_col, *(pairs or ()))


LRU_HALO = 8


def _lru_gates(ext, cw_ref, cb_ref, wa_ref, ba_ref, wi_ref, bi_ref, lam_ref, tr):
    taps = [pltpu.roll(ext, 3 - k, 0)[LRU_HALO:] if k < 3 else ext[LRU_HALO:] for k in range(4)]
    xc = cb_ref[...] + taps[0] * cw_ref[0:1, :]
    for k in range(1, 4):
        xc = xc + taps[k] * cw_ref[k:k + 1, :]
    ga = jnp.concatenate([_dot(xc[:, g * 128:(g + 1) * 128], wa_ref[g]) for g in range(4)], axis=1) + ba_ref[...]
    gi = jnp.concatenate([_dot(xc[:, g * 128:(g + 1) * 128], wi_ref[g]) for g in range(4)], axis=1) + bi_ref[...]
    r, ig = _sigmoid(ga), _sigmoid(gi)
    nl = -lam_ref[...]
    sp = jnp.maximum(nl, 0.0) + jnp.log(1.0 + jnp.exp(-jnp.abs(nl)))
    la = -LRU_C * r * sp
    a = jnp.exp(la)
    mult = jnp.sqrt(_neg_expm1(2.0 * la))
    return xc, r, ig, sp, a, mult, taps


def _lru_specs(tr, nt, rev):
    hb = tr // LRU_HALO
    ti = (lambda i: nt - 1 - i) if rev else (lambda i: i)
    zx_b, zy_b = ZX // LRU_WIDTH, ZY // LRU_WIDTH
    cur = lambda b: pl.BlockSpec((tr, LRU_WIDTH), lambda i: (ti(i), b))
    prev = lambda b: pl.BlockSpec((LRU_HALO, LRU_WIDTH), lambda i: (jnp.maximum(ti(i) * hb - 1, 0), b))
    vec = pl.BlockSpec((1, LRU_WIDTH), lambda i: (0, 0))
    cw = pl.BlockSpec((4, LRU_WIDTH), lambda i: (0, 0))
    blk = pl.BlockSpec((4, 128, 128), lambda i: (0, 0, 0))
    return ti, cur, prev, vec, cw, blk, zx_b, zy_b


def _lru_fwd(z, cw, cb, wa, ba, wi, bi, lam):
    S = z.shape[0]
    tr = _div_tile(S, TILES["lru"], 8)
    nt = S // tr
    ti, cur, prev, vec, cwspec, blk, zx_b, zy_b = _lru_specs(tr, nt, False)

    def body(zx_ref, halo_ref, zy_ref, cw_ref, cb_ref, wa_ref, ba_ref, wi_ref, bi_ref, lam_ref, y_ref, h_ref, carry):
        i = pl.program_id(0)

        @pl.when(i == 0)
        def _():
            carry[...] = jnp.zeros_like(carry)

        ext = jnp.concatenate([jnp.where(i > 0, halo_ref[...], 0.0), zx_ref[...]], axis=0)
        xc, r, ig, sp, a, mult, _ = _lru_gates(ext, cw_ref, cb_ref, wa_ref, ba_ref, wi_ref, bi_ref, lam_ref, tr)
        A, B = a, mult * (ig * xc)
        row = _rows(tr, LRU_WIDTH)
        d = 1
        while d < tr:
            a_sh = jnp.where(row >= d, pltpu.roll(A, d, 0), 1.0)
            b_sh = jnp.where(row >= d, pltpu.roll(B, d, 0), 0.0)
            B = A * b_sh + B
            A = A * a_sh
            d *= 2
        h = B + A * carry[...]
        h_ref[...] = h
        carry[...] = h_ref[pl.ds(tr - 1, 1), :]
        y_ref[...] = h * _gelu_parts(zy_ref[...])[0]

    out = pl.BlockSpec((tr, LRU_WIDTH), lambda i: (i, 0))
    shape = jax.ShapeDtypeStruct((S, LRU_WIDTH), F32)
    return pl.pallas_call(
        body, name="lru_fwd", grid=(nt,),
        in_specs=[cur(zx_b), prev(zx_b), cur(zy_b), cwspec, vec, blk, vec, blk, vec, vec],
        out_specs=(out, out), out_shape=(shape, shape), scratch_shapes=[pltpu.VMEM((1, LRU_WIDTH), F32)],
        compiler_params=_params(("arbitrary",)),
    )(z, z, z, cw, cb, wa, ba, wi, bi, lam)


def _lru_bwd(z, dy, hs, cw, cb, wa, ba, wi, bi, lam):
    S = z.shape[0]
    tr = _div_tile(S, TILES["lru"], 8)
    nt = S // tr
    ti, cur, prev, vec, cwspec, blk, zx_b, zy_b = _lru_specs(tr, nt, True)
    dy_b = (POOL_WIDTH + ATTN_WIDTH) // LRU_WIDTH

    def body(zx_ref, halo_ref, zy_ref, dy_ref, h_ref, hprev_ref, cw_ref, cb_ref, wa_ref, ba_ref, wi_ref, bi_ref, lam_ref,
             dzx_ref, dzy_ref, dcw_ref, dcb_ref, dwa_ref, dba_ref, dwi_ref, dbi_ref, dlam_ref, gcarry, dxc_next, tmp):
        i = pl.program_id(0)
        t_idx = nt - 1 - i

        @pl.when(i == 0)
        def _():
            gcarry[...] = jnp.zeros_like(gcarry)
            dxc_next[...] = jnp.zeros_like(dxc_next)
            for ref in (dcw_ref, dcb_ref, dwa_ref, dba_ref, dwi_ref, dbi_ref, dlam_ref):
                ref[...] = jnp.zeros_like(ref)

        ext = jnp.concatenate([jnp.where(t_idx > 0, halo_ref[...], 0.0), zx_ref[...]], axis=0)
        xc, r, ig, sp, a, mult, taps = _lru_gates(ext, cw_ref, cb_ref, wa_ref, ba_ref, wi_ref, bi_ref, lam_ref, tr)
        h = h_ref[...]
        gel, dgel = _gelu_parts(zy_ref[...])
        dy_ = dy_ref[...]
        dzy_ref[...] = (dy_ * h * dgel).astype(dzy_ref.dtype)
        row = _rows(tr, LRU_WIDTH)
        B = dy_ * gel + jnp.where(row == tr - 1, gcarry[...], 0.0)
        A = jnp.where(row < tr - 1, pltpu.roll(a, tr - 1, 0), 0.0)
        d = 1
        while d < tr:
            keep = row < tr - d
            b_sh = jnp.where(keep, pltpu.roll(B, tr - d, 0), 0.0)
            a_sh = jnp.where(keep, pltpu.roll(A, tr - d, 0), 0.0)
            B = B + A * b_sh
            A = A * a_sh
            d *= 2
        g = B
        tmp[...] = a * g
        gcarry[...] = tmp[pl.ds(0, 1), :]
        h_ext = jnp.concatenate([jnp.where(t_idx > 0, hprev_ref[...], 0.0), h], axis=0)
        hprev = pltpu.roll(h_ext, 1, 0)[LRU_HALO:]
        t1 = g * mult
        dig = t1 * xc
        dxc = t1 * ig
        dla = (g * hprev) * a - (g * (ig * xc)) * (a * a) / mult
        dr = dla * (-LRU_C * sp)
        dga = dr * r * (1.0 - r)
        dgi = dig * ig * (1.0 - ig)
        dlam_ref[...] += _colsum(dla * (-LRU_C * r)) * (-_sigmoid(-lam_ref[...]))
        dba_ref[...] += _colsum(dga)
        dbi_ref[...] += _colsum(dgi)
        parts = []
        for gidx in range(4):
            cols = slice(gidx * 128, (gidx + 1) * 128)
            xct = xc[:, cols].T
            dwa_ref[gidx] += _dot(xct, dga[:, cols])
            dwi_ref[gidx] += _dot(xct, dgi[:, cols])
            parts.append(_dot_nt(dga[:, cols], wa_ref[gidx]) + _dot_nt(dgi[:, cols], wi_ref[gidx]))
        dxc = dxc + jnp.concatenate(parts, axis=1)
        dcb_ref[...] += _colsum(dxc)
        for k in range(4):
            dcw_ref[k:k + 1, :] += _colsum(dxc * taps[k])
        ext_d = jnp.concatenate([dxc, dxc_next[...]], axis=0)
        dzx = dxc * cw_ref[3:4, :]
        for k in range(3):
            dzx = dzx + pltpu.roll(ext_d, tr + LRU_HALO - (3 - k), 0)[:tr] * cw_ref[k:k + 1, :]
        dzx_ref[...] = dzx.astype(dzx_ref.dtype)
        dxc_next[...] = dxc[:LRU_HALO]

    rev = pl.BlockSpec((tr, LRU_WIDTH), lambda i: (nt - 1 - i, 0))
    hb = tr // LRU_HALO
    hprev_spec = pl.BlockSpec((LRU_HALO, LRU_WIDTH), lambda i: (jnp.maximum((nt - 1 - i) * hb - 1, 0), 0))
    dy_spec = pl.BlockSpec((tr, LRU_WIDTH), lambda i: (nt - 1 - i, dy_b))
    vshape = jax.ShapeDtypeStruct((1, LRU_WIDTH), F32)
    bshape = jax.ShapeDtypeStruct((4, 128, 128), F32)
    return pl.pallas_call(
        body, name="lru_bwd", grid=(nt,),
        in_specs=[cur(zx_b), prev(zx_b), cur(zy_b), dy_spec, rev, hprev_spec, cwspec, vec, blk, vec, blk, vec, vec],
        out_specs=(rev, rev, cwspec, vec, blk, vec, blk, vec, vec),
        out_shape=(jax.ShapeDtypeStruct((S, LRU_WIDTH), BF16), jax.ShapeDtypeStruct((S, LRU_WIDTH), BF16),
                   jax.ShapeDtypeStruct((4, LRU_WIDTH), F32), vshape, bshape, vshape, bshape, vshape, vshape),
        scratch_shapes=[pltpu.VMEM((1, LRU_WIDTH), F32), pltpu.VMEM((LRU_HALO, LRU_WIDTH), F32),
                        pltpu.VMEM((tr, LRU_WIDTH), F32)],
        compiler_params=_params(("arbitrary",)),
    )(z, z, z, dy, hs, hs, cw, cb, wa, ba, wi, bi, lam)


FFN_HALO = 8


def _ffn_act(au, cw, cb):
    S, F2 = au.shape
    F = F2 // 2
    tc = FFN_CHUNK
    tr = _div_tile(S, TILES["row"], 8)
    hb = tr // FFN_HALO

    def body(au_ref, halo_ref, cw_ref, cb_ref, p_ref):
        i = pl.program_id(0)
        a_ = au_ref[:, :tc]
        ext = jnp.concatenate([jnp.where(i > 0, halo_ref[:, :tc], 0.0), a_], axis=0)
        gc = cb_ref[...] + a_ * cw_ref[2:3, :]
        for k in range(2):
            gc = gc + pltpu.roll(ext, 2 - k, 0)[FFN_HALO:] * cw_ref[k:k + 1, :]
        p_ref[...] = (gc * _sigmoid(gc) * au_ref[:, tc:]).astype(p_ref.dtype)

    return pl.pallas_call(
        body, name="ffn_act", grid=(S // tr, F // tc),
        in_specs=[pl.BlockSpec((tr, 2 * tc), lambda i, j: (i, j)),
                  pl.BlockSpec((FFN_HALO, 2 * tc), lambda i, j: (jnp.maximum(i * hb - 1, 0), j)),
                  pl.BlockSpec((3, tc), lambda i, j: (0, j)), pl.BlockSpec((1, tc), lambda i, j: (0, j))],
        out_specs=pl.BlockSpec((tr, tc), lambda i, j: (i, j)), out_shape=jax.ShapeDtypeStruct((S, F), BF16),
        compiler_params=_params(("parallel", "parallel")),
    )(au, au, cw, cb)


def _ffn_act_bwd(au, dp, cw, cb):
    S, F2 = au.shape
    F = F2 // 2
    tc = FFN_CHUNK
    tr = _div_tile(S, TILES["ew"], 8)
    hb = tr // FFN_HALO
    nt = S // tr
    H = FFN_HALO

    def body(au_ref, prev_ref, next_ref, dp_ref, dpn_ref, cw_ref, cb_ref, dau_ref, dcw_ref, dcb_ref):
        i = pl.program_id(1)
        last = i == nt - 1
        a_ext = jnp.concatenate([jnp.where(i > 0, prev_ref[:, :tc], 0.0), au_ref[:, :tc], next_ref[:, :tc]], axis=0)
        u_ext = jnp.concatenate([au_ref[:, tc:], next_ref[:, tc:]], axis=0)
        dp_ext = jnp.concatenate([dp_ref[...], jnp.where(last, 0.0, dpn_ref[...])], axis=0)
        taps = [pltpu.roll(a_ext, 2 - k, 0)[H:] if k < 2 else a_ext[H:] for k in range(3)]
        gc = cb_ref[...] + taps[0] * cw_ref[0:1, :] + taps[1] * cw_ref[1:2, :] + taps[2] * cw_ref[2:3, :]
        sig = _sigmoid(gc)
        dgc = dp_ext * u_ext * (sig * (1.0 + gc * (1.0 - sig)))
        da = dgc[:tr] * cw_ref[2:3, :]
        for k in range(2):
            da = da + pltpu.roll(dgc, tr + H - (2 - k), 0)[:tr] * cw_ref[k:k + 1, :]
        dau_ref[:, :tc] = da.astype(dau_ref.dtype)
        dau_ref[:, tc:] = (dp_ref[...] * (gc[:tr] * sig[:tr])).astype(dau_ref.dtype)

        @pl.when(i == 0)
        def _():
            dcw_ref[...] = jnp.zeros_like(dcw_ref)
            dcb_ref[...] = jnp.zeros_like(dcb_ref)

        dcb_ref[...] += _colsum(dgc[:tr])
        for k in range(3):
            dcw_ref[k:k + 1, :] += _colsum(dgc[:tr] * taps[k][:tr])

    return pl.pallas_call(
        body, name="ffn_act_bwd", grid=(F // tc, nt),
        in_specs=[pl.BlockSpec((tr, 2 * tc), lambda j, i: (i, j)),
                  pl.BlockSpec((H, 2 * tc), lambda j, i: (jnp.maximum(i * hb - 1, 0), j)),
                  pl.BlockSpec((H, 2 * tc), lambda j, i: (jnp.minimum((i + 1) * hb, nt * hb - 1), j)),
                  pl.BlockSpec((tr, tc), lambda j, i: (i, j)),
                  pl.BlockSpec((H, tc), lambda j, i: (jnp.minimum((i + 1) * hb, nt * hb - 1), j)),
                  pl.BlockSpec((3, tc), lambda j, i: (0, j)), pl.BlockSpec((1, tc), lambda j, i: (0, j))],
        out_specs=(pl.BlockSpec((tr, 2 * tc), lambda j, i: (i, j)), pl.BlockSpec((3, tc), lambda j, i: (0, j)),
                   pl.BlockSpec((1, tc), lambda j, i: (0, j))),
        out_shape=(jax.ShapeDtypeStruct((S, F2), BF16), jax.ShapeDtypeStruct((3, F), F32), jax.ShapeDtypeStruct((1, F), F32)),
        compiler_params=_params(("parallel", "arbitrary")),
    )(au, au, au, dp, dp, cw, cb)


def _row_tile(rows, cols):
    return _div_tile(rows, max(16, (2**18 // cols) // 16 * 16), 16)


def _sum_parts(parts, sel, rows, cols, name):
    tr = _row_tile(rows, cols)

    def body(sel_ref, *refs):
        acc = refs[0][...].astype(F32)
        for r in refs[1:-1]:
            acc = acc + r[...].astype(F32)
        refs[-1][...] = acc

    def spec(index):
        if isinstance(index, int):
            return pl.BlockSpec((None, tr, cols), lambda i, s: (index, i, 0))
        k, mul, off = index
        return pl.BlockSpec((None, tr, cols), lambda i, s: (s[k] * mul + off, i, 0))

    grid_spec = pltpu.PrefetchScalarGridSpec(
        num_scalar_prefetch=1, grid=(rows // tr,), in_specs=[spec(ix) for _, ix in parts],
        out_specs=pl.BlockSpec((tr, cols), lambda i, s: (i, 0)))
    return pl.pallas_call(
        body, name=name, grid_spec=grid_spec, out_shape=jax.ShapeDtypeStruct((rows, cols), F32),
        compiler_params=_params(("parallel",)),
    )(sel, *[a for a, _ in parts])


def _pair_sum(g, got, sel):
    n, rows, cols = got.shape
    tr = _row_tile(rows, cols)
    nb = rows // tr

    def body(sel_ref, a_ref, b_ref, o_ref):
        o_ref[...] = (a_ref[...] + b_ref[...]).astype(o_ref.dtype)

    grid_spec = pltpu.PrefetchScalarGridSpec(
        num_scalar_prefetch=1, grid=(n, nb),
        in_specs=[pl.BlockSpec((None, tr, cols), lambda q, i, s: (q, s[0] * nb + i, 0)),
                  pl.BlockSpec((None, tr, cols), lambda q, i, s: (q, i, 0))],
        out_specs=pl.BlockSpec((None, tr, cols), lambda q, i, s: (q, i, 0)))
    return pl.pallas_call(
        body, name="pair_sum", grid_spec=grid_spec, out_shape=jax.ShapeDtypeStruct((n, rows, cols), BF16),
        compiler_params=_params(("parallel", "parallel")),
    )(sel, g, got)


def _chip_sum_cols(pair, arrived, sel):
    n, rows, cg = arrived.shape[1:]
    tr = _row_tile(rows, cg)

    def body(sel_ref, p_ref, a0, a1, a2, o_ref):
        o_ref[...] = ((p_ref[...].astype(F32) + a0[...].astype(F32)) + a1[...].astype(F32)) + a2[...].astype(F32)

    def arr(j):
        return pl.BlockSpec((None, None, tr, cg), lambda k, i, s: (j, k, i, 0))

    grid_spec = pltpu.PrefetchScalarGridSpec(
        num_scalar_prefetch=1, grid=(n, rows // tr),
        in_specs=[pl.BlockSpec((None, tr, cg), lambda k, i, s: (k, i, s[1])), arr(0), arr(1), arr(2)],
        out_specs=pl.BlockSpec((None, tr, cg), lambda k, i, s: (k, i, 0)))
    return pl.pallas_call(
        body, name="chip_sum_cols", grid_spec=grid_spec, out_shape=jax.ShapeDtypeStruct((n, rows, cg), F32),
        compiler_params=_params(("parallel", "parallel")),
    )(sel, pair, arrived, arrived, arrived)


def _adamw_math(w, g, m, v):
    m2 = ADAM_B1 * m + (1.0 - ADAM_B1) * g
    v2 = ADAM_B2 * v + (1.0 - ADAM_B2) * (g * g)
    m_hat = m2 / (1.0 - ADAM_B1 ** ADAM_STEP)
    v_hat = v2 / (1.0 - ADAM_B2 ** ADAM_STEP)
    return -ADAM_LR * (m_hat / (jnp.sqrt(v_hat) + ADAM_EPS) + ADAM_WD * w), m2, v2


def _adamw(w, g, m, v):
    R, C = w.shape
    tr = _div_tile(R, TILES["ew"], 8)

    def body(w_ref, g_ref, m_ref, v_ref, d_ref, m2_ref, v2_ref):
        d_ref[...], m2_ref[...], v2_ref[...] = _adamw_math(w_ref[...], g_ref[...], m_ref[...], v_ref[...])

    spec = pl.BlockSpec((tr, C), lambda i: (i, 0))
    shape = jax.ShapeDtypeStruct((R, C), F32)
    return pl.pallas_call(
        body, name="adamw", grid=(R // tr,), in_specs=[spec] * 4, out_specs=(spec,) * 3, out_shape=(shape,) * 3,
        compiler_params=_params(("parallel",)),
    )(w, g, m, v)


def _ada_grad_adamw(cact_t, dmod, w, m, v):
    L, D, N = w.shape
    tm, tn = _div_tile(D, 256, 8), _div_tile(N, 1024, 128)

    def body(c_ref, d_ref, w_ref, m_ref, v_ref, g_ref, dl_ref, m2_ref, v2_ref):
        g = c_ref[:, 0:1] * d_ref[0:1, :]
        for b in range(1, N_DEV):
            g = g + c_ref[:, b:b + 1] * d_ref[b:b + 1, :]
        g_ref[...] = g
        dl_ref[...], m2_ref[...], v2_ref[...] = _adamw_math(w_ref[...], g, m_ref[...], v_ref[...])

    big = pl.BlockSpec((None, tm, tn), lambda l, i, j: (l, i, j))
    shape = jax.ShapeDtypeStruct((L, D, N), F32)
    return pl.pallas_call(
        body, name="ada_grad_adamw", grid=(L, D // tm, N // tn),
        in_specs=[pl.BlockSpec((tm, N_DEV), lambda l, i, j: (i, 0)),
                  pl.BlockSpec((None, N_DEV, tn), lambda l, i, j: (l, 0, j)), big, big, big],
        out_specs=(big,) * 4, out_shape=(shape,) * 4, compiler_params=_params(("parallel", "parallel", "parallel")),
    )(cact_t, dmod, w, m, v)


def _silu_rows(c):
    def body(c_ref, o_ref):
        x = c_ref[...]
        o_ref[...] = x * _sigmoid(x)

    return pl.pallas_call(body, name="silu_rows", out_shape=jax.ShapeDtypeStruct(c.shape, F32))(c)


ANY = pl.BlockSpec(memory_space=pl.ANY)


def _position():
    return lax.axis_index("x"), lax.axis_index("y"), lax.axis_index("c")


def _other_chips(x, y):
    return [(1 - x, y), (x, 1 - y), (1 - x, 1 - y)]


def _allgather8(v):
    R, C = v.shape

    def body(v_ref, out_ref, send_sems, recv_sems):
        x, y, c = _position()
        me = 4 * x + 2 * y + c
        sends, recvs = [], []
        for k in range(1, N_DEV):
            px, py, pc = (x + (k >> 2)) % 2, (y + ((k >> 1) & 1)) % 2, (c + (k & 1)) % 2
            sends.append(pltpu.make_async_remote_copy(
                src_ref=v_ref, dst_ref=out_ref.at[me], send_sem=send_sems.at[k - 1], recv_sem=recv_sems.at[k - 1],
                device_id=(px, py, pc), device_id_type=MESH))
            recvs.append(pltpu.make_async_remote_copy(
                src_ref=v_ref, dst_ref=out_ref.at[4 * px + 2 * py + pc], send_sem=send_sems.at[k - 1],
                recv_sem=recv_sems.at[k - 1], device_id=(px, py, pc), device_id_type=MESH))
        for cp in sends:
            cp.start()
        for cp in recvs:
            cp.wait_recv()
        for cp in sends:
            cp.wait_send()

    others = pl.pallas_call(
        body, name="comm_allgather8", out_shape=jax.ShapeDtypeStruct((N_DEV, R, C), v.dtype), in_specs=[ANY],
        out_specs=ANY, scratch_shapes=[pltpu.SemaphoreType.DMA((N_DEV - 1,)), pltpu.SemaphoreType.DMA((N_DEV - 1,))],
    )(v)
    x, y, c = _position()
    return lax.dynamic_update_slice(others, v[None], (4 * x + 2 * y + c, 0, 0))


def _remote(src, dst, send_sems, recv_sems, k, to):
    return pltpu.make_async_remote_copy(src_ref=src, dst_ref=dst, send_sem=send_sems.at[k], recv_sem=recv_sems.at[k],
                                        device_id=to, device_id_type=MESH)


def _half_rows(ref, h):
    n = ref.shape[0] // 2
    return ref.at[pl.ds(h * n, n)]


N_SHARDS = 5


class _GatherPlan:
    def __init__(self, ins, outs, sems):
        win, wout, gate, up, down = ins
        win4, wout4, gu, down4 = outs
        send_sems, recv_sems, local_sems = sems
        x, y, c = _position()
        q = 2 * x + y
        sibling = (x, y, 1 - c)
        CG = gate.shape[1]
        tensors = [(win, lambda p: win4.at[p]), (wout, lambda p: wout4.at[p]),
                   (gate, lambda p: gu.at[0, :, pl.ds(p * CG, CG)]), (up, lambda p: gu.at[1, :, pl.ds(p * CG, CG)]),
                   (down, lambda p: down4.at[p])]
        self.own, self.first, self.landed, self.passed, self.passed_landed = [], [], [], [], []
        for t, (src, dst) in enumerate(tensors):
            self.own.append(pltpu.make_async_copy(src, dst(q), local_sems.at[t]))
            for j, (px, py) in enumerate(_other_chips(x, y)):
                self.first.append(_remote(_half_rows(src, c), _half_rows(dst(q), c), send_sems, recv_sems, 6 * t + j,
                                          (px, py, c)))
                mine = _half_rows(dst(2 * px + py), c)
                self.landed.append(_remote(mine, mine, send_sems, recv_sems, 6 * t + j, (px, py, c)))
                self.passed.append(_remote(mine, mine, send_sems, recv_sems, 6 * t + 3 + j, sibling))
                other = _half_rows(dst(2 * px + py), 1 - c)
                self.passed_landed.append(_remote(other, other, send_sems, recv_sems, 6 * t + 3 + j, sibling))

    def start(self):
        for cp in self.own + self.first:
            cp.start()

    def forward(self):
        for landed, onward in zip(self.landed, self.passed):
            landed.wait_recv()
            onward.start()

    def finish(self):
        for cp in self.passed_landed:
            cp.wait_recv()
        for cp in self.first + self.passed:
            cp.wait_send()
        for cp in self.own:
            cp.wait()


def _gather_shapes(shards):
    win_s, wout_s, gate_s, up_s, down_s = shards
    return (jax.ShapeDtypeStruct((N_CHIPS,) + win_s.shape, BF16), jax.ShapeDtypeStruct((N_CHIPS,) + wout_s.shape, BF16),
            jax.ShapeDtypeStruct((2, gate_s.shape[0], N_CHIPS * gate_s.shape[1]), BF16),
            jax.ShapeDtypeStruct((N_CHIPS,) + down_s.shape, BF16))


GATHER_SEMS = [pltpu.SemaphoreType.DMA((6 * N_SHARDS,)), pltpu.SemaphoreType.DMA((6 * N_SHARDS,)),
               pltpu.SemaphoreType.DMA((N_SHARDS,))]


def _gather_weights(shards):
    def body(*refs):
        plan = _GatherPlan(refs[:N_SHARDS], refs[N_SHARDS:N_SHARDS + 4], refs[N_SHARDS + 4:])
        plan.start()
        plan.forward()
        plan.finish()

    return pl.pallas_call(
        body, name="comm_gather_weights", out_shape=_gather_shapes(shards), in_specs=[ANY] * N_SHARDS,
        out_specs=(ANY,) * 4, scratch_shapes=GATHER_SEMS,
    )(*shards)


def _sibling_swap_halves(gs):
    n_t = len(gs)

    def body(*refs):
        ins, outs, (send_sems, recv_sems) = refs[:n_t], refs[n_t:2 * n_t], refs[2 * n_t:]
        x, y, c = _position()
        cps = []
        for t in range(n_t):
            half = ins[t].shape[1] // 2
            cps.append(_remote(ins[t].at[:, pl.ds((1 - c) * half, half), :], outs[t], send_sems, recv_sems, t,
                               (x, y, 1 - c)))
        for cp in cps:
            cp.start()
        for cp in cps:
            cp.wait()

    shapes = tuple(jax.ShapeDtypeStruct((g.shape[0], g.shape[1] // 2, g.shape[2]), g.dtype) for g in gs)
    return pl.pallas_call(
        body, name="comm_sibling_swap", out_shape=shapes, in_specs=[ANY] * n_t, out_specs=(ANY,) * n_t,
        scratch_shapes=[pltpu.SemaphoreType.DMA((n_t,)), pltpu.SemaphoreType.DMA((n_t,))],
    )(*gs)


N_PAIRS = 4
SCATTER_COPIES = 3 * (N_PAIRS + 1)
SCATTER_SEMS = [pltpu.SemaphoreType.DMA((SCATTER_COPIES,)), pltpu.SemaphoreType.DMA((SCATTER_COPIES,))]


def _scatter_copies(ins, outs, sems):
    send_sems, recv_sems = sems
    n_l = N_PAIRS - 1
    CG = ins[n_l].shape[2] // N_CHIPS
    x, y, c = _position()
    cps = []
    for j, (px, py) in enumerate(_other_chips(x, y)):
        p = 2 * px + py
        for t in range(n_l):
            cps.append(_remote(ins[t].at[p], outs[t].at[j], send_sems, recv_sems, len(cps), (px, py, c)))
        for k in range(2):
            cps.append(_remote(ins[n_l].at[k, :, pl.ds(p * CG, CG)], outs[n_l].at[j, k], send_sems, recv_sems, len(cps),
                               (px, py, c)))
    return cps


def _scatter_shapes(pairs):
    gu = pairs[-1]
    return tuple(jax.ShapeDtypeStruct((3,) + g.shape[1:], g.dtype) for g in pairs[:-1]) + (
        jax.ShapeDtypeStruct((3, 2, gu.shape[1], gu.shape[2] // N_CHIPS), gu.dtype),)


def _chip_scatter(pairs):
    def body(*refs):
        cps = _scatter_copies(refs[:N_PAIRS], refs[N_PAIRS:2 * N_PAIRS], refs[2 * N_PAIRS:])
        for cp in cps:
            cp.start()
        for cp in cps:
            cp.wait()

    return pl.pallas_call(
        body, name="comm_chip_scatter", out_shape=_scatter_shapes(pairs), in_specs=[ANY] * N_PAIRS,
        out_specs=(ANY,) * N_PAIRS, scratch_shapes=SCATTER_SEMS,
    )(*pairs)


def _sibling_exchange(vs):
    n_t = len(vs)

    def body(*refs):
        ins, outs, (send_sems, recv_sems) = refs[:n_t], refs[n_t:2 * n_t], refs[2 * n_t:]
        x, y, c = _position()
        cps = [_remote(ins[t], outs[t], send_sems, recv_sems, t, (x, y, 1 - c)) for t in range(n_t)]
        for cp in cps:
            cp.start()
        for cp in cps:
            cp.wait()

    shapes = tuple(jax.ShapeDtypeStruct(v.shape, v.dtype) for v in vs)
    return pl.pallas_call(
        body, name="comm_sibling_exchange", out_shape=shapes, in_specs=[ANY] * n_t, out_specs=(ANY,) * n_t,
        scratch_shapes=[pltpu.SemaphoreType.DMA((n_t,)), pltpu.SemaphoreType.DMA((n_t,))],
    )(*vs)


def _join_halves(mine, theirs, sel):
    n, rows, cols = mine.shape
    tr = _row_tile(rows, cols)
    nb = rows // tr

    def body(sel_ref, a_ref, b_ref, o_ref):
        @pl.when(pl.program_id(1) == sel_ref[0])
        def _():
            o_ref[...] = a_ref[...]

        @pl.when(pl.program_id(1) != sel_ref[0])
        def _():
            o_ref[...] = b_ref[...]

    grid_spec = pltpu.PrefetchScalarGridSpec(
        num_scalar_prefetch=1, grid=(n, 2, nb),
        in_specs=[pl.BlockSpec((None, tr, cols), lambda k, h, i, s: (k, jnp.where(h == s[0], i, 0), 0)),
                  pl.BlockSpec((None, tr, cols), lambda k, h, i, s: (k, jnp.where(h == s[0], 0, i), 0))],
        out_specs=pl.BlockSpec((None, tr, cols), lambda k, h, i, s: (k, h * nb + i, 0)))
    return pl.pallas_call(
        body, name="join_halves", grid_spec=grid_spec, out_shape=jax.ShapeDtypeStruct((n, 2 * rows, cols), mine.dtype),
        compiler_params=_params(("parallel", "arbitrary", "arbitrary")),
    )(sel, mine, theirs)


def _reduce_start(grads, sel):
    got = _sibling_swap_halves(grads)
    return [_pair_sum(g, r, sel) for g, r in zip(grads, got)]


def _reduce_finish(pairs, arrived, sel):
    mine = [_sum_parts([(p, (1, 1, 0)), (a, 0), (a, 1), (a, 2)], sel, p.shape[1], p.shape[2], "chip_sum")
            for p, a in zip(pairs[:-1], arrived[:-1])]
    mine.append(_chip_sum_cols(pairs[-1], arrived[-1], sel))
    theirs = _sibling_exchange(mine)
    joined = [_join_halves(m.reshape((-1,) + m.shape[-2:]), t.reshape((-1,) + t.shape[-2:]), sel)
              for m, t in zip(mine, theirs)]
    return [j[0] for j in joined[:-1]] + [joined[-1]]


def _layer_fwd(x, mod, p, next_shards=None):
    sh1, sc1, gt1, sh2, sc2, gt2 = mod
    h1 = _norm_mod(x, p["g_mix"], sc1, sh1)
    z = _matmul(h1, p["w_in"], name="mm_in")
    y_pool = _pool_fwd(z, p["pool_w"], p["pool_scale"])
    F = _forget_cumsum(z, p["b_f"])
    Fh = F[:, :N_HEADS].T
    f_col, f_row = Fh[:, :, None], Fh[:, None, :]
    o_t, lse, *gathered = _flash_fwd(z, z[:, ZV:ZV + ATTN_WIDTH].T.astype(BF16), f_col, f_row, next_shards)
    y_lru, hs = _lru_fwd(z, p["lru_conv_w"], p["lru_conv_b"], p["lru_wa"], p["lru_ba"], p["lru_wi"], p["lru_bi"],
                         p["lru_lambda"])
    y = jnp.concatenate([y_pool.astype(BF16), o_t.T.astype(BF16), y_lru.astype(BF16)], axis=1)
    m1, x_mid = _matmul(y, p["w_out"], res=x, gate=gt1, name="mm_out")
    h2 = _norm_mod(x_mid, p["g_ffn"], sc2, sh2)
    au = _matmul(h2, p["w_gu"], gu="b", name="mm_gu")
    pa = _ffn_act(au, p["ffn_conv_w"], p["ffn_conv_b"])
    m2, x_out = _matmul(pa, p["w_down"], res=x_mid, gate=gt2, name="mm_down")
    saved = dict(x=x, h1=h1, z=z, f_col=f_col, f_row=f_row, o_t=o_t, lse=lse, hs=hs, y=y, m1=m1, x_mid=x_mid, h2=h2, au=au,
                 pa=pa, m2=m2)
    return x_out, saved, gathered


def _layer_bwd(dx_out, mod, p, s, pairs=None):
    sh1, sc1, gt1, sh2, sc2, gt2 = mod
    g = {}
    dm2, dgt2 = _gate_bwd(dx_out, s["m2"], gt2)
    dpa = _matmul(dm2, p["w_down"], nt=True, name="mm_down_dx")
    g["w_down"] = _matmul(s["pa"], dm2, ta=True, name="mm_down_dw")
    dau, g["ffn_conv_w"], g["ffn_conv_b"] = _ffn_act_bwd(s["au"], dpa, p["ffn_conv_w"], p["ffn_conv_b"])
    dh2 = _matmul(dau, p["w_gu"], nt=True, gu="b", name="mm_gu_dx")
    g["w_gu"] = _matmul(s["h2"], dau, ta=True, gu="out", name="mm_gu_dw")
    dx_mid, g["g_ffn"], dsc2, dsh2 = _norm_mod_bwd(s["x_mid"], dh2, dx_out, p["g_ffn"], sc2)
    dm1, dgt1 = _gate_bwd(dx_mid, s["m1"], gt1)
    dy = _matmul(dm1, p["w_out"], nt=True, name="mm_out_dx")
    g["w_out"] = _matmul(s["y"], dm1, ta=True, name="mm_out_dw")
    z = s["z"]
    (dzx, dzy, g["lru_conv_w"], g["lru_conv_b"], g["lru_wa"], g["lru_ba"], g["lru_wi"], g["lru_bi"],
     g["lru_lambda"]) = _lru_bwd(z, dy, s["hs"], p["lru_conv_w"], p["lru_conv_b"], p["lru_wa"], p["lru_ba"], p["lru_wi"],
                                 p["lru_bi"], p["lru_lambda"])
    k_t = z[:, ZK:ZK + ATTN_WIDTH].T.astype(BF16)
    do_t = dy[:, POOL_WIDTH:POOL_WIDTH + ATTN_WIDTH].T
    dq_t, dk, dv, dfk, dfq, *arrived = _flash_bwd(z, k_t, dy, do_t, s["o_t"], s["lse"], s["f_col"], s["f_row"], pairs)
    dq = dq_t.T
    dF_pad = jnp.pad((dfq[:, 0, :] + dfk[:, :, 0]).T, ((0, 0), (0, 128 - N_HEADS)))
    dzf, db_f = _forget_cumsum_bwd(z, p["b_f"], dF_pad)
    g["b_f"] = db_f[:, :N_HEADS]
    dzp, g["pool_w"], g["pool_scale"] = _pool_bwd(z, dy, p["pool_w"], p["pool_scale"])
    S = z.shape[0]
    dz = jnp.concatenate([dzp, dq, dk, dv, dzx, dzy, dzf, jnp.zeros((S, ZW - ZF - 128), BF16)], axis=1)
    dh1 = _matmul(dz, p["w_in"], nt=True, name="mm_in_dx")
    g["w_in"] = _matmul(s["h1"], dz, ta=True, name="mm_in_dw")
    dx_in, g["g_mix"], dsc1, dsh1 = _norm_mod_bwd(s["x"], dh1, dx_mid, p["g_mix"], sc1)
    return dx_in, g, (dsh1, dsc1, dgt1, dsh2, dsc2, dgt2), arrived


def _big_weights(gathered):
    win4, wout4, gu, down4 = gathered
    D, F = win4.shape[1], gu.shape[2]
    return dict(w_in=_pad_in_cols(jnp.transpose(win4, (1, 0, 2)).reshape(D, N_IN)), w_out=wout4.reshape(D, D), w_gu=gu,
                w_down=down4.reshape(F, D))


def _big_grads(g):
    D, F = g["w_out"].shape[0], g["w_down"].shape[0]
    dwin4 = jnp.transpose(_unpad_in_cols(g["w_in"]).reshape(D, N_CHIPS, N_IN // N_CHIPS), (1, 0, 2))
    return [dwin4, g["w_out"].reshape(N_CHIPS, D // N_CHIPS, D), g["w_down"].reshape(N_CHIPS, F // N_CHIPS, D), g["w_gu"]]


def _local_step(x, target, mods, layers, final_g, shards=None, sel=None):
    L = len(layers)
    saved, params = [], []
    gathered = _gather_weights(shards[0]) if shards else None
    for l in range(L):
        p = {**layers[l], **_big_weights(gathered)} if shards else layers[l]
        x, s, gathered = _layer_fwd(x, mods[l], p, shards[l + 1] if shards and l + 1 < L else None)
        saved.append(s)
        params.append(p)
    dx, dfinal_g, loss = _loss_head(x, target, final_g)
    grads, dmods, reduced, pairs = [None] * L, [None] * L, [None] * L, None
    for l in reversed(range(L)):
        dx, grads[l], dmods[l], arrived = _layer_bwd(dx, mods[l], params[l], saved[l], pairs)
        if shards:
            if pairs is not None:
                reduced[l + 1] = _reduce_finish(pairs, arrived, sel)
            pairs = _reduce_start(_big_grads(grads[l]), sel)
    if shards:
        reduced[0] = _reduce_finish(pairs, _chip_scatter(pairs), sel)
    return loss, dx, grads, dmods, dfinal_g, reduced


def _pad_in_cols(w):
    D = w.shape[0]
    return jnp.concatenate([w[:, :3584], w[:, 3592:N_IN], w[:, 3584:3592], jnp.zeros((D, ZW - N_IN), w.dtype)], axis=1)


def _unpad_in_cols(w):
    return jnp.concatenate([w[:, :3584], w[:, ZF:ZF + N_HEADS], w[:, 3584:ZF]], axis=1)


BIG = ("w_in", "w_out", "w_ffn_gate", "w_ffn_up", "w_ffn_down")
SMALL = ("b_ada", "g_mix", "b_f", "pool_w", "pool_scale", "lru_conv_w", "lru_conv_b", "lru_wa", "lru_ba", "lru_wi",
         "lru_bi", "lru_lambda", "g_ffn", "ffn_conv_w", "ffn_conv_b", "final_g")
SHARDED_SMALL = ("lru_conv_w", "ffn_conv_w")
WEIGHTS = ("w_ada", "b_ada", "g_mix", "w_in", "b_f", "pool_w", "pool_scale", "lru_conv_w", "lru_conv_b", "lru_wa",
           "lru_ba", "lru_wi", "lru_bi", "lru_lambda", "w_out", "g_ffn", "w_ffn_gate", "w_ffn_up", "ffn_conv_w",
           "ffn_conv_b", "w_ffn_down", "final_g")


PACK_QUANTUM = 512 * 128


def _pack(arrays):
    flat = jnp.concatenate([a.reshape(-1).astype(F32) for a in arrays])
    n = -(-flat.shape[0] // PACK_QUANTUM) * PACK_QUANTUM
    return jnp.pad(flat, (0, n - flat.shape[0])).reshape(n // 128, 128)


def _unpack(packed, shapes):
    flat = packed.reshape(-1)
    out, off = [], 0
    for shp in shapes:
        n = int(np.prod(shp))
        out.append(flat[off:off + n].reshape(shp))
        off += n
    return out


def kernel(x, c, w_ada, b_ada, g_mix, w_in, b_f, pool_w, pool_scale, lru_conv_w, lru_conv_b, lru_wa, lru_ba, lru_wi, lru_bi, lru_lambda, w_out, g_ffn, w_ffn_gate, w_ffn_up, ffn_conv_w, ffn_conv_b, w_ffn_down, final_g, loss_target, m_w_ada, m_b_ada, m_g_mix, m_w_in, m_b_f, m_pool_w, m_pool_scale, m_lru_conv_w, m_lru_conv_b, m_lru_wa, m_lru_ba, m_lru_wi, m_lru_bi, m_lru_lambda, m_w_out, m_g_ffn, m_w_ffn_gate, m_w_ffn_up, m_ffn_conv_w, m_ffn_conv_b, m_w_ffn_down, m_final_g, v_w_ada, v_b_ada, v_g_mix, v_w_in, v_b_f, v_pool_w, v_pool_scale, v_lru_conv_w, v_lru_conv_b, v_lru_wa, v_lru_ba, v_lru_wi, v_lru_bi, v_lru_lambda, v_w_out, v_g_ffn, v_w_ffn_gate, v_w_ffn_up, v_ffn_conv_w, v_ffn_conv_b, v_w_ffn_down, v_final_g):
    env = dict(locals())
    W = {n: env[n] for n in WEIGHTS}
    M = {n: env["m_" + n] for n in WEIGHTS}
    V = {n: env["v_" + n] for n in WEIGHTS}
    L, D = g_mix.shape
    S = x.shape[1]
    F = 4 * w_ffn_gate.shape[2]
    ix, iy, ic = lax.axis_index("x"), lax.axis_index("y"), lax.axis_index("c")
    q = 2 * ix + iy
    me = 2 * q + ic

    head = _allgather8(_pack([c, lru_conv_w, ffn_conv_w]))
    nlc, nfc = lru_conv_w.size, ffn_conv_w.size
    c_all = head.reshape(N_DEV, -1)[:, :D]
    lru_cw = jnp.concatenate([head[2 * k].reshape(-1)[D:D + nlc].reshape(L, 4, -1) for k in range(N_CHIPS)], axis=2)
    ffn_cw = jnp.concatenate([head[2 * k].reshape(-1)[D + nlc:D + nlc + nfc].reshape(L, 3, -1) for k in range(N_CHIPS)],
                             axis=2)
    cact = _silu_rows(c_all)

    NA = w_ada.shape[2]
    mod_part = jnp.stack([_matmul(cact, w_ada[l], name="mm_ada") for l in range(L)])
    mod_all = _allgather8(mod_part.reshape(L * N_DEV, NA)).reshape(N_DEV, L, N_DEV, NA)
    mod_full = jnp.concatenate([mod_all[2 * k] for k in range(N_CHIPS)], axis=2)
    mod_mine = lax.dynamic_index_in_dim(mod_full, me, axis=1, keepdims=False) + b_ada
    mods = [[mod_mine[l, k * D:(k + 1) * D].reshape(1, D) for k in range(6)] for l in range(L)]

    shards_all = [W[n].astype(BF16) for n in BIG]
    shards = [[w[l] for w in shards_all] for l in range(L)]
    layers = []
    for l in range(L):
        layers.append(dict(
            g_mix=g_mix[l][None], g_ffn=g_ffn[l][None], b_f=jnp.pad(b_f[l], (0, 128 - N_HEADS))[None],
            pool_w=pool_w[l], pool_scale=pool_scale[l][None], lru_conv_w=lru_cw[l], lru_conv_b=lru_conv_b[l][None],
            lru_wa=lru_wa[l], lru_ba=lru_ba[l][None], lru_wi=lru_wi[l], lru_bi=lru_bi[l][None],
            lru_lambda=lru_lambda[l][None], ffn_conv_w=ffn_cw[l], ffn_conv_b=ffn_conv_b[l][None]))

    sel = jnp.stack([ic, q]).astype(jnp.int32)
    loss, dx, grads, dmods, dfinal_g, reduced = _local_step(x[0], loss_target[0], mods, layers, final_g[None], shards, sel)

    G = {n: [] for n in BIG}
    for g_in, g_out, g_down, g_gu in reduced:
        for n, a in zip(BIG, (g_in, g_out, g_gu[0], g_gu[1], g_down)):
            G[n].append(a)
    G = {n: jnp.stack(G[n]) for n in BIG}

    stack = lambda name: jnp.stack([grads[l][name] for l in range(L)])
    dmod = jnp.stack([jnp.concatenate(dmods[l], axis=1)[0] for l in range(L)])
    small = dict(b_ada=dmod, g_mix=stack("g_mix"), b_f=stack("b_f"), pool_w=stack("pool_w"),
                 pool_scale=stack("pool_scale"), lru_conv_w=stack("lru_conv_w"), lru_conv_b=stack("lru_conv_b"),
                 lru_wa=stack("lru_wa"), lru_ba=stack("lru_ba"), lru_wi=stack("lru_wi"), lru_bi=stack("lru_bi"),
                 lru_lambda=stack("lru_lambda"), g_ffn=stack("g_ffn"), ffn_conv_w=stack("ffn_conv_w"),
                 ffn_conv_b=stack("ffn_conv_b"), final_g=dfinal_g)
    packed = _pack([small[n] for n in SMALL] + [loss[0, :1]])
    everyone = _allgather8(packed)
    zero_sel = jnp.zeros((2,), jnp.int32)
    total = _sum_parts([(everyone, k) for k in range(N_DEV)], zero_sel, packed.shape[0], 128, "device_sum")
    sums = _unpack(total, [small[n].shape for n in SMALL] + [(1,)])
    loss_total = sums[-1][0]
    for n, a in zip(SMALL, sums[:-1]):
        a = a.reshape((L, -1, a.shape[-1])) if n in SHARDED_SMALL else a.reshape(W[n].shape)
        if n in SHARDED_SMALL:
            a = lax.dynamic_slice_in_dim(a, q * W[n].shape[2], W[n].shape[2], axis=2)
        G[n] = a

    dmod_all = everyone.reshape(N_DEV, -1)[:, :L * 6 * D].reshape(N_DEV, L, 6 * D)
    dmod_cols = jnp.transpose(lax.dynamic_slice_in_dim(dmod_all, q * NA, NA, axis=2), (1, 0, 2))
    G["w_ada"], d_ada, m_ada, v_ada = _ada_grad_adamw(cact.T, dmod_cols, w_ada, m_w_ada, v_w_ada)
    delta, new_m, new_v = {"w_ada": d_ada}, {"w_ada": m_ada}, {"w_ada": v_ada}

    for n in BIG:
        cols = W[n].shape[-1]
        outs = _adamw(*[a.reshape(-1, cols) for a in (W[n], G[n], M[n], V[n])])
        delta[n], new_m[n], new_v[n] = [o.reshape(W[n].shape) for o in outs]
    outs = _adamw(*[_pack([t[n] for n in SMALL]) for t in (W, G, M, V)])
    shapes = [W[n].shape for n in SMALL]
    for tgt, o in zip((delta, new_m, new_v), outs):
        for n, a in zip(SMALL, _unpack(o, shapes)):
            tgt[n] = a

    return (loss_total, dx[None], *[G[n] for n in WEIGHTS], *[delta[n] for n in WEIGHTS],
            *[new_m[n] for n in WEIGHTS], *[new_v[n] for n in WEIGHTS])
```
